```python
import jax, jax.numpy as jnp
from jax import lax
import numpy as np

D_MODEL = 4096
BATCH = 8
SEQ = 2048
DEPTH = 2
DEC_BATCH = 8
DEC_SEQ = 64
PAST_LEN = 4096

CHUNK = 64
CONV_W = 4
N_MEM = 256
EPS = 1e-6
LRU_WIDTH = 1024
LRU_BLOCKS = 4
LRU_BLOCK = LRU_WIDTH // LRU_BLOCKS
LRU_C = 8.0
RET_HEADS = 4
RET_DK = 128
RET_DV = 256
RET_QK = RET_HEADS * RET_DK
RET_V = RET_HEADS * RET_DV
ROPE_BASE = 10000.0
SSD_INNER = 2048
SSD_HEADDIM = 64
SSD_HEADS = SSD_INNER // SSD_HEADDIM
SSD_STATE = 128
SSD_GROUPS = 4
SSD_CONV_DIM = SSD_INNER + 2 * SSD_GROUPS * SSD_STATE
D_MIX = LRU_WIDTH + RET_V + SSD_INNER
IN_SIZES = (LRU_WIDTH, LRU_WIDTH, RET_QK, RET_QK, RET_V, RET_V, SSD_INNER, SSD_CONV_DIM, SSD_HEADS)
IN_DIM = sum(IN_SIZES)
XATT_HEADS = 4
XATT_HD = 128
XATT_DIM = XATT_HEADS * XATT_HD

kernel_name = "hymba_style_lru_retention_ssd_stream_step"


def _split_points(sizes):
    pts, acc = [], 0
    for s in sizes[:-1]:
        acc += s
        pts.append(acc)
    return pts


def rmsnorm(x, w):
    xf = x.astype(jnp.float32)
    return xf * lax.rsqrt(jnp.mean(xf * xf, axis=-1, keepdims=True) + EPS) * w.astype(jnp.float32)


def causal_conv(x, prev, w, b):
    xp = jnp.concatenate([prev.astype(jnp.float32), x.astype(jnp.float32)], axis=1)
    T = x.shape[1]
    y = b + sum(xp[:, k:k + T] * w[k] for k in range(CONV_W))
    return y, xp[:, -(CONV_W - 1):]


def rotary(x, pos):
    half = x.shape[-1] // 2
    inv = ROPE_BASE ** (-jnp.arange(half, dtype=jnp.float32) / half)
    ang = pos[:, None] * inv[None, :]
    cos = jnp.cos(ang)[None, :, None, :]
    sin = jnp.sin(ang)[None, :, None, :]
    x1, x2 = x[..., :half], x[..., half:]
    return jnp.concatenate([x1 * cos - x2 * sin, x2 * cos + x1 * sin], axis=-1)


def decay_linear_attn(q, k, v, g, s0):
    f32 = jnp.float32
    q, k, v, g, s0 = q.astype(f32), k.astype(f32), v.astype(f32), g.astype(f32), s0.astype(f32)
    B, T, H, K = q.shape
    V = v.shape[-1]
    c = min(CHUNK, T)
    n = T // c
    q = q.reshape(B, n, c, H, K)
    k = k.reshape(B, n, c, H, K)
    v = v.reshape(B, n, c, H, V)
    G = jnp.cumsum(g.reshape(B, n, c, H), axis=2)
    Gh = G.transpose(0, 1, 3, 2)
    mask = jnp.tril(jnp.ones((c, c), dtype=bool))
    decay = jnp.exp(jnp.where(mask, Gh[..., :, None] - Gh[..., None, :], -jnp.inf))
    scores = jnp.einsum('bnihk,bnjhk->bnhij', q, k) * decay
    y_intra = jnp.einsum('bnhij,bnjhv->bnihv', scores, v)
    w_tail = jnp.exp(G[:, :, -1:, :] - G)
    dS = jnp.einsum('bnjhk,bnjhv->bnhkv', k * w_tail[..., None], v)
    a_chunk = jnp.exp(G[:, :, -1, :])

    def step(S, inp):
        a, ds = inp
        return a[..., None, None] * S + ds, S

    s_fin, s_in = lax.scan(step, s0, (a_chunk.transpose(1, 0, 2), dS.transpose(1, 0, 2, 3, 4)))
    s_in = s_in.transpose(1, 0, 2, 3, 4)
    y_inter = jnp.einsum('bnihk,bnhkv->bnihv', q * jnp.exp(G)[..., None], s_in)
    return (y_intra + y_inter).reshape(B, T, H, V), s_fin


def rg_lru(x, h0, wa, ba, wx, bx, lam):
    x = x.astype(jnp.float32)
    B, T, W = x.shape
    xr = x.reshape(B, T, LRU_BLOCKS, LRU_BLOCK)
    r = jax.nn.sigmoid(jnp.einsum('btnh,nhk->btnk', xr, wa).reshape(B, T, W) + ba)
    i = jax.nn.sigmoid(jnp.einsum('btnh,nhk->btnk', xr, wx).reshape(B, T, W) + bx)
    log_a = -LRU_C * r * jax.nn.softplus(-lam)
    a = jnp.exp(log_a)
    b = jnp.sqrt(-jnp.expm1(2.0 * log_a)) * (i * x)
    b = b.at[:, 0].add(a[:, 0] * h0.astype(jnp.float32))

    def comb(left, right):
        a1, b1 = left
        a2, b2 = right
        return a1 * a2, a2 * b1 + b2

    _, h = lax.associative_scan(comb, (a, b), axis=1)
    return h, h[:, -1]


def mixer_sublayer(x, pos, lru_h0, lru_conv0, ret_s0, ssd_s0, ssd_conv0, p):
    B, T, _ = x.shape
    hn = rmsnorm(x, p['norm_mix'])
    proj = hn @ p['w_in']
    lx, lg, rq, rk, rv, rg, sz, sxbc, sdt = jnp.split(proj, _split_points(IN_SIZES), axis=-1)

    lx, lru_conv1 = causal_conv(lx, lru_conv0, p['lru_conv_w'], p['lru_conv_b'])
    lh, lru_h1 = rg_lru(lx, lru_h0, p['lru_wa'], p['lru_ba'], p['lru_wx'], p['lru_bx'], p['lru_lambda'])
    lru_out = rmsnorm(lh, p['lru_norm']) * jax.nn.silu(lg)

    q = rotary(rq.reshape(B, T, RET_HEADS, RET_DK), pos)
    k = rotary(rk.reshape(B, T, RET_HEADS, RET_DK), pos) * (RET_DK ** -0.5)
    v = rv.reshape(B, T, RET_HEADS, RET_DV)
    log_gamma = jnp.log1p(-(2.0 ** (-5.0 - jnp.arange(RET_HEADS, dtype=jnp.float32))))
    g = jnp.broadcast_to(log_gamma, (B, T, RET_HEADS))
    ro, ret_s1 = decay_linear_attn(q, k, v, g, ret_s0)
    mu = jnp.mean(ro, axis=-1, keepdims=True)
    var = jnp.mean(jnp.square(ro - mu), axis=-1, keepdims=True)
    ro = ((ro - mu) * lax.rsqrt(var + EPS)).reshape(B, T, RET_V) * p['ret_norm']
    ret_out = ro * jax.nn.silu(rg)

    xbc, ssd_conv1 = causal_conv(sxbc, ssd_conv0, p['ssd_conv_w'], p['ssd_conv_b'])
    xbc = jax.nn.silu(xbc)
    sx, sb, sc = jnp.split(xbc, [SSD_INNER, SSD_INNER + SSD_GROUPS * SSD_STATE], axis=-1)
    sx = sx.reshape(B, T, SSD_HEADS, SSD_HEADDIM)
    rep = SSD_HEADS // SSD_GROUPS
    sb = jnp.repeat(sb.reshape(B, T, SSD_GROUPS, SSD_STATE), rep, axis=2)
    sc = jnp.repeat(sc.reshape(B, T, SSD_GROUPS, SSD_STATE), rep, axis=2)
    dt = jax.nn.softplus(sdt.astype(jnp.float32) + p['ssd_dt_bias'])
    A = -jnp.exp(p['ssd_a_log'].astype(jnp.float32))
    so, ssd_s1 = decay_linear_attn(sc, sb, sx * dt[..., None], dt * A, ssd_s0)
    so = so + p['ssd_d'][:, None] * sx
    ssd_out = rmsnorm(so.reshape(B, T, SSD_INNER) * jax.nn.silu(sz), p['ssd_norm'])

    mix = jnp.concatenate([lru_out, ret_out, ssd_out], axis=-1)
    return x + mix @ p['w_out'], (lru_h1, lru_conv1, ret_s1, ssd_s1, ssd_conv1)


def memory_kv(mem, norm_mem, w_k, w_v):
    B = mem.shape[0]
    mn = rmsnorm(mem, norm_mem)
    mk = (mn @ w_k).reshape(B, N_MEM, XATT_HEADS, XATT_HD)
    mv = (mn @ w_v).reshape(B, N_MEM, XATT_HEADS, XATT_HD)
    return mk, mv


def cross_attn(x, mem_k, mem_v, norm_x, w_q, w_o):
    B, T, _ = x.shape
    q = (rmsnorm(x, norm_x) @ w_q).reshape(B, T, XATT_HEADS, XATT_HD)
    s = jnp.einsum('bthd,bmhd->bhtm', q, mem_k).astype(jnp.float32) * (XATT_HD ** -0.5)
    a = jax.nn.softmax(s, axis=-1)
    o = jnp.einsum('bhtm,bmhd->bthd', a, mem_v).reshape(B, T, XATT_DIM)
    return x + o @ w_o


def setup_inputs(seed: int = 0) -> dict:
    key = jax.random.key(seed)
    ks = jax.random.split(key, 40)
    f32 = jnp.float32
    nrm = lambda k, s, sc: jax.random.normal(k, s, f32) * sc
    gain = lambda k, s: 1.0 + 0.02 * jax.random.normal(k, s, f32)
    u_a = jax.random.uniform(ks[30], (DEPTH, LRU_WIDTH), f32, 0.9, 0.999)
    a0 = u_a ** (1.0 / LRU_C)
    lru_lambda = jnp.log(a0) - jnp.log1p(-a0)
    dt0 = jnp.exp(jax.random.uniform(ks[31], (DEPTH, SSD_HEADS), f32, np.log(1e-3), np.log(1e-1)))
    ssd_dt_bias = dt0 + jnp.log(-jnp.expm1(-dt0))
    ssd_a_log = jnp.log(jax.random.uniform(ks[32], (DEPTH, SSD_HEADS), f32, 1.0, 16.0))
    return {
        "x_prompt": nrm(ks[0], (BATCH, SEQ, D_MODEL), 1.0),
        "x_sample": nrm(ks[1], (DEC_BATCH, DEC_SEQ, D_MODEL), 1.0),
        "mem_prompt": nrm(ks[2], (BATCH, N_MEM, D_MODEL), 1.0),
        "state_lru_h": nrm(ks[3], (DEPTH, DEC_BATCH, LRU_WIDTH), 0.5),
        "state_lru_conv": nrm(ks[4], (DEPTH, DEC_BATCH, CONV_W - 1, LRU_WIDTH), 1.0),
        "state_ret": nrm(ks[5], (DEPTH, DEC_BATCH, RET_HEADS, RET_DK, RET_DV), 0.5),
        "state_ssd": nrm(ks[6], (DEPTH, DEC_BATCH, SSD_HEADS, SSD_STATE, SSD_HEADDIM), 0.1),
        "state_ssd_conv": nrm(ks[7], (DEPTH, DEC_BATCH, CONV_W - 1, SSD_CONV_DIM), 1.0),
        "cache_mem_k": nrm(ks[8], (DEPTH, DEC_BATCH, N_MEM, XATT_HEADS, XATT_HD), 1.0),
        "cache_mem_v": nrm(ks[9], (DEPTH, DEC_BATCH, N_MEM, XATT_HEADS, XATT_HD), 1.0),
        "norm_mix": gain(ks[10], (DEPTH, D_MODEL)),
        "w_in": nrm(ks[11], (DEPTH, D_MODEL, IN_DIM), D_MODEL ** -0.5),
        "lru_conv_w": nrm(ks[12], (DEPTH, CONV_W, LRU_WIDTH), CONV_W ** -0.5),
        "lru_conv_b": nrm(ks[13], (DEPTH, LRU_WIDTH), 0.01),
        "lru_wa": nrm(ks[14], (DEPTH, LRU_BLOCKS, LRU_BLOCK, LRU_BLOCK), LRU_BLOCK ** -0.5),
        "lru_ba": nrm(ks[15], (DEPTH, LRU_WIDTH), 0.01),
        "lru_wx": nrm(ks[16], (DEPTH, LRU_BLOCKS, LRU_BLOCK, LRU_BLOCK), LRU_BLOCK ** -0.5),
        "lru_bx": nrm(ks[17], (DEPTH, LRU_WIDTH), 0.01),
        "lru_lambda": lru_lambda,
        "lru_norm": gain(ks[18], (DEPTH, LRU_WIDTH)),
        "ret_norm": gain(ks[19], (DEPTH, RET_V)),
        "ssd_conv_w": nrm(ks[20], (DEPTH, CONV_W, SSD_CONV_DIM), CONV_W ** -0.5),
        "ssd_conv_b": nrm(ks[21], (DEPTH, SSD_CONV_DIM), 0.01),
        "ssd_dt_bias": ssd_dt_bias,
        "ssd_a_log": ssd_a_log,
        "ssd_d": gain(ks[22], (DEPTH, SSD_HEADS)),
        "ssd_norm": gain(ks[23], (DEPTH, SSD_INNER)),
        "w_out": nrm(ks[24], (DEPTH, D_MIX, D_MODEL), D_MIX ** -0.5),
        "norm_xattn": gain(ks[25], (DEPTH, D_MODEL)),
        "norm_mem": gain(ks[26], (DEPTH, D_MODEL)),
        "w_q_mem": nrm(ks[27], (DEPTH, D_MODEL, XATT_DIM), D_MODEL ** -0.5),
        "w_k_mem": nrm(ks[28], (DEPTH, D_MODEL, XATT_DIM), D_MODEL ** -0.5),
        "w_v_mem": nrm(ks[29], (DEPTH, D_MODEL, XATT_DIM), D_MODEL ** -0.5),
        "w_o_mem": nrm(ks[33], (DEPTH, XATT_DIM, D_MODEL), XATT_DIM ** -0.5),
        "norm_final": gain(ks[34], (D_MODEL,)),
    }


def reference(x_prompt, x_sample, mem_prompt, state_lru_h, state_lru_conv, state_ret, state_ssd,
              state_ssd_conv, cache_mem_k, cache_mem_v, norm_mix, w_in, lru_conv_w, lru_conv_b,
              lru_wa, lru_ba, lru_wx, lru_bx, lru_lambda, lru_norm, ret_norm, ssd_conv_w, ssd_conv_b,
              ssd_dt_bias, ssd_a_log, ssd_d, ssd_norm, w_out, norm_xattn, norm_mem, w_q_mem, w_k_mem,
              w_v_mem, w_o_mem, norm_final):
    f32 = jnp.float32
    Bp, Tp = x_prompt.shape[0], x_prompt.shape[1]
    Ts = x_sample.shape[1]
    pos_p = jnp.arange(Tp, dtype=f32)
    pos_s = PAST_LEN + jnp.arange(Ts, dtype=f32)
    z_lru_h = jnp.zeros((Bp, LRU_WIDTH), f32)
    z_lru_conv = jnp.zeros((Bp, CONV_W - 1, LRU_WIDTH), f32)
    z_ret = jnp.zeros((Bp, RET_HEADS, RET_DK, RET_DV), f32)
    z_ssd = jnp.zeros((Bp, SSD_HEADS, SSD_STATE, SSD_HEADDIM), f32)
    z_ssd_conv = jnp.zeros((Bp, CONV_W - 1, SSD_CONV_DIM), f32)

    xp, xs = x_prompt, x_sample
    lh_p, lc_p, rs_p, ss_p, sc_p, mk_p, mv_p = [], [], [], [], [], [], []
    lh_s, lc_s, rs_s, ss_s, sc_s = [], [], [], [], []
    for l in range(DEPTH):
        p = dict(norm_mix=norm_mix[l], w_in=w_in[l], lru_conv_w=lru_conv_w[l], lru_conv_b=lru_conv_b[l],
                 lru_wa=lru_wa[l], lru_ba=lru_ba[l], lru_wx=lru_wx[l], lru_bx=lru_bx[l],
                 lru_lambda=lru_lambda[l], lru_norm=lru_norm[l], ret_norm=ret_norm[l],
                 ssd_conv_w=ssd_conv_w[l], ssd_conv_b=ssd_conv_b[l], ssd_dt_bias=ssd_dt_bias[l],
                 ssd_a_log=ssd_a_log[l], ssd_d=ssd_d[l], ssd_norm=ssd_norm[l], w_out=w_out[l])
        xp, st = mixer_sublayer(xp, pos_p, z_lru_h, z_lru_conv, z_ret, z_ssd, z_ssd_conv, p)
        mk, mv = memory_kv(mem_prompt, norm_mem[l], w_k_mem[l], w_v_mem[l])
        xp = cross_attn(xp, mk, mv, norm_xattn[l], w_q_mem[l], w_o_mem[l])
        lh_p.append(st[0]); lc_p.append(st[1]); rs_p.append(st[2]); ss_p.append(st[3]); sc_p.append(st[4])
        mk_p.append(mk); mv_p.append(mv)
        xs, st = mixer_sublayer(xs, pos_s, state_lru_h[l], state_lru_conv[l], state_ret[l], state_ssd[l],
                                state_ssd_conv[l], p)
        xs = cross_attn(xs, cache_mem_k[l], cache_mem_v[l], norm_xattn[l], w_q_mem[l], w_o_mem[l])
        lh_s.append(st[0]); lc_s.append(st[1]); rs_s.append(st[2]); ss_s.append(st[3]); sc_s.append(st[4])

    y_prompt = rmsnorm(xp, norm_final)
    y_sample = rmsnorm(xs, norm_final)
    return (y_prompt, y_sample,
            jnp.stack(lh_p), jnp.stack(lc_p), jnp.stack(rs_p), jnp.stack(ss_p), jnp.stack(sc_p),
            jnp.stack(mk_p), jnp.stack(mv_p),
            jnp.stack(lh_s), jnp.stack(lc_s), jnp.stack(rs_s), jnp.stack(ss_s), jnp.stack(sc_s))
```

```python
import functools
import math

import jax
import jax.numpy as jnp
from jax import lax
from jax.experimental import pallas as pl
from jax.experimental.pallas import tpu as pltpu

F32 = jnp.float32
BF16 = jnp.bfloat16

EPS = 1e-6
CONV_W = 4
PAST_LEN = 4096
N_MEM = 256
LRU_WIDTH = 1024
LRU_BLOCKS = 4
LRU_BLOCK = LRU_WIDTH // LRU_BLOCKS
LRU_C = 8.0
RET_HEADS = 4
RET_DK = 128
RET_DV = 256
RET_QK = RET_HEADS * RET_DK
RET_V = RET_HEADS * RET_DV
ROPE_BASE = 10000.0
SSD_INNER = 2048
SSD_HEADDIM = 64
SSD_HEADS = SSD_INNER // SSD_HEADDIM
SSD_STATE = 128
SSD_GROUPS = 4
SSD_GROUP_HEADS = SSD_HEADS // SSD_GROUPS
SSD_BC = SSD_GROUPS * SSD_STATE
SSD_CONV_DIM = SSD_INNER + 2 * SSD_BC
XATT_HEADS = 4
XATT_HD = 128
XATT_DIM = XATT_HEADS * XATT_HD

SUBLANES = 8
LANES = 128
VMEM_LIMIT_BYTES = 56 * 1024 * 1024

PROJ_MAIN = 2 * SSD_INNER + 2 * LRU_WIDTH + 2 * RET_V + 2 * RET_QK + 2 * SSD_BC
NEG_BIG = -1e30

RET_LOG_GAMMA = tuple(math.log1p(-(2.0 ** (-5.0 - h))) for h in range(RET_HEADS))


def _params(semantics):
    return pltpu.CompilerParams(dimension_semantics=semantics, vmem_limit_bytes=VMEM_LIMIT_BYTES)


def _sigmoid(x):
    return 1.0 / (1.0 + jnp.exp(-x))


def _silu(x):
    return x * _sigmoid(x)


def _softplus(x):
    return jnp.maximum(x, 0.0) + jnp.log1p(jnp.exp(-jnp.abs(x)))


def _dot(a, b):
    return jnp.dot(a, b, preferred_element_type=F32)


def _dot_nt(a, b):
    return lax.dot_general(a, b, (((1,), (1,)), ((), ())), preferred_element_type=F32)


def _dot_tn(a, b):
    return lax.dot_general(a, b, (((0,), (0,)), ((), ())), preferred_element_type=F32)


def _rmsnorm_rows_to(dst_ref, src_ref, gain_ref, row_chunk):
    n = src_ref.shape[0] // row_chunk

    def body(r, carry):
        rows = pl.ds(pl.multiple_of(r * row_chunk, row_chunk), row_chunk)
        x = src_ref[rows, :].astype(F32)
        ms = jnp.mean(x * x, axis=-1, keepdims=True)
        dst_ref[rows, :] = (x * lax.rsqrt(ms + EPS) * gain_ref[...]).astype(dst_ref.dtype)
        return carry

    lax.fori_loop(0, n, body, 0)


def _norm_matmul_body(*refs, has_extra, row_chunk):
    if has_extra:
        x_ref, g_ref, w_ref, w2_ref, o_ref, o2_ref, hn_ref = refs
    else:
        x_ref, g_ref, w_ref, o_ref, hn_ref = refs

    @pl.when(pl.program_id(1) == 0)
    def _():
        _rmsnorm_rows_to(hn_ref, x_ref, g_ref, row_chunk)
        if has_extra:
            o2_ref[...] = _dot(hn_ref[...], w2_ref[...])

    o_ref[...] = _dot(hn_ref[...], w_ref[...]).astype(o_ref.dtype)


def _norm_matmul(x, g, w, w2=None, *, out_dtype, tm, tn):
    m, k = x.shape
    n = w.shape[1]
    tm = min(tm, m)
    tn = min(tn, n)
    assert m % tm == 0 and n % tn == 0
    has_extra = w2 is not None
    in_specs = [
        pl.BlockSpec((tm, k), lambda i, j: (i, 0)),
        pl.BlockSpec((1, k), lambda i, j: (0, 0)),
        pl.BlockSpec((k, tn), lambda i, j: (0, j)),
    ]
    out_specs = [pl.BlockSpec((tm, tn), lambda i, j: (i, j))]
    out_shape = [jax.ShapeDtypeStruct((m, n), out_dtype)]
    args = [x, g.reshape(1, k), w]
    if has_extra:
        n2 = w2.shape[1]
        in_specs.append(pl.BlockSpec((k, n2), lambda i, j: (0, 0)))
        out_specs.append(pl.BlockSpec((tm, n2), lambda i, j: (i, 0)))
        out_shape.append(jax.ShapeDtypeStruct((m, n2), F32))
        args.append(w2)
    res = pl.pallas_call(
        functools.partial(_norm_matmul_body, has_extra=has_extra, row_chunk=min(32, tm)),
        grid=(m // tm, n // tn),
        in_specs=in_specs,
        out_specs=out_specs,
        out_shape=out_shape,
        scratch_shapes=[pltpu.VMEM((tm, k), BF16)],
        compiler_params=_params(("arbitrary", "arbitrary")),
        name="norm_matmul",
    )(*args)
    return res if has_extra else res[0]


def _out_proj_body(x_ref, a1_ref, a2_ref, a3_ref, w1_ref, w2_ref, w3_ref, o_ref):
    acc = _dot(a1_ref[...], w1_ref[...])
    acc = acc + _dot(a2_ref[...], w2_ref[...])
    acc = acc + _dot(a3_ref[...], w3_ref[...])
    o_ref[...] = x_ref[...] + acc


def _out_proj(x, lru_out, ret_out, ssd_out, w_out_bf16, *, tm, tn):
    m, n = x.shape
    tm = min(tm, m)
    tn = min(tn, n)
    assert m % tm == 0 and n % tn == 0
    return pl.pallas_call(
        _out_proj_body,
        grid=(m // tm, n // tn),
        in_specs=[
            pl.BlockSpec((tm, tn), lambda i, j: (i, j)),
            pl.BlockSpec((tm, LRU_WIDTH), lambda i, j: (i, 0)),
            pl.BlockSpec((tm, RET_V), lambda i, j: (i, 0)),
            pl.BlockSpec((tm, SSD_INNER), lambda i, j: (i, 0)),
            pl.BlockSpec((LRU_WIDTH, tn), lambda i, j: (0, j)),
            pl.BlockSpec((RET_V, tn), lambda i, j: (1, j)),
            pl.BlockSpec((SSD_INNER, tn), lambda i, j: (1, j)),
        ],
        out_specs=pl.BlockSpec((tm, tn), lambda i, j: (i, j)),
        out_shape=jax.ShapeDtypeStruct((m, n), F32),
        compiler_params=_params(("arbitrary", "arbitrary")),
        name="out_proj",
    )(x, lru_out, ret_out, ssd_out, w_out_bf16, w_out_bf16, w_out_bf16)


def _causal_conv(zbuf, n_rows, cw_ref, cb_ref):
    acc = zbuf[pl.ds(SUBLANES - 3, n_rows), :] * cw_ref[0:1, :]
    acc = acc + zbuf[pl.ds(SUBLANES - 2, n_rows), :] * cw_ref[1:2, :]
    acc = acc + zbuf[pl.ds(SUBLANES - 1, n_rows), :] * cw_ref[2:3, :]
    acc = acc + zbuf[pl.ds(SUBLANES, n_rows), :] * cw_ref[3:4, :]
    return cb_ref[...] + acc


def _lru_body(lx_ref, lg_ref, conv0_ref, h0_ref, cw_ref, cb_ref, wa_ref, ba_ref, wx_ref, bx_ref, lam_ref, nrm_ref,
              out_ref, hlast_ref, ctail_ref, zbuf, hcar):
    tb = lx_ref.shape[0]

    @pl.when(pl.program_id(1) == 0)
    def _():
        zbuf[0:SUBLANES, :] = conv0_ref[...]
        hcar[...] = h0_ref[...]

    zbuf[pl.ds(SUBLANES, tb), :] = lx_ref[...].astype(F32)
    xc = _causal_conv(zbuf, tb, cw_ref, cb_ref)
    tail = zbuf[pl.ds(tb, SUBLANES), :]
    zbuf[0:SUBLANES, :] = tail
    ctail_ref[...] = tail

    xb = xc.astype(BF16)
    ra, ri = [], []
    for n in range(LRU_BLOCKS):
        xn = xb[:, n * LRU_BLOCK:(n + 1) * LRU_BLOCK]
        ra.append(_dot(xn, wa_ref[n]))
        ri.append(_dot(xn, wx_ref[n]))
    r = _sigmoid(jnp.concatenate(ra, axis=-1) + ba_ref[...])
    i = _sigmoid(jnp.concatenate(ri, axis=-1) + bx_ref[...])
    log_a = (-LRU_C) * r * _softplus(-lam_ref[...])
    a = jnp.exp(log_a)
    th = jnp.tanh(log_a)
    b = jnp.sqrt(-2.0 * th / (1.0 - th)) * (i * xc)

    rows = lax.broadcasted_iota(jnp.int32, (tb, LRU_WIDTH), 0) & (SUBLANES - 1)
    s = 1
    while s < SUBLANES:
        keep = rows >= s
        a_sh = pltpu.roll(a, s, 0)
        b_sh = pltpu.roll(b, s, 0)
        b = jnp.where(keep, a * b_sh + b, b)
        a = jnp.where(keep, a * a_sh, a)
        s *= 2
    hprev = hcar[...]
    hs = []
    for g in range(tb // SUBLANES):
        hg = b[g * SUBLANES:(g + 1) * SUBLANES] + a[g * SUBLANES:(g + 1) * SUBLANES] * hprev
        hprev = hg[SUBLANES - 1:SUBLANES]
        hs.append(hg)
    h = jnp.concatenate(hs, axis=0)
    hcar[...] = hprev
    hlast_ref[...] = hprev

    ms = jnp.mean(h * h, axis=-1, keepdims=True)
    out = h * lax.rsqrt(ms + EPS) * nrm_ref[...]
    out_ref[...] = (out * _silu(lg_ref[...].astype(F32))).astype(out_ref.dtype)


def _lru_mixer(proj, conv0, h0, p, *, batch, seq, tb):
    tb = min(tb, seq)
    nt = seq // tb
    w = LRU_WIDTH
    row = lambda b, t: b * nt + t
    const2 = lambda b, t: (0, 0)
    return pl.pallas_call(
        _lru_body,
        grid=(batch, nt),
        in_specs=[
            pl.BlockSpec((tb, w), lambda b, t: (row(b, t), 4)),
            pl.BlockSpec((tb, w), lambda b, t: (row(b, t), 5)),
            pl.BlockSpec((None, SUBLANES, w), lambda b, t: (b, 0, 0)),
            pl.BlockSpec((None, 1, w), lambda b, t: (b, 0, 0)),
            pl.BlockSpec((CONV_W, w), const2),
            pl.BlockSpec((1, w), const2),
            pl.BlockSpec((LRU_BLOCKS, LRU_BLOCK, LRU_BLOCK), lambda b, t: (0, 0, 0)),
            pl.BlockSpec((1, w), const2),
            pl.BlockSpec((LRU_BLOCKS, LRU_BLOCK, LRU_BLOCK), lambda b, t: (0, 0, 0)),
            pl.BlockSpec((1, w), const2),
            pl.BlockSpec((1, w), const2),
            pl.BlockSpec((1, w), const2),
        ],
        out_specs=[
            pl.BlockSpec((tb, w), lambda b, t: (row(b, t), 0)),
            pl.BlockSpec((None, 1, w), lambda b, t: (b, 0, 0)),
            pl.BlockSpec((None, SUBLANES, w), lambda b, t: (b, 0, 0)),
        ],
        out_shape=[
            jax.ShapeDtypeStruct((batch * seq, w), BF16),
            jax.ShapeDtypeStruct((batch, 1, w), F32),
            jax.ShapeDtypeStruct((batch, SUBLANES, w), F32),
        ],
        scratch_shapes=[pltpu.VMEM((tb + SUBLANES, w), F32), pltpu.VMEM((1, w), F32)],
        compiler_params=_params(("arbitrary", "arbitrary")),
        name="lru_mixer",
    )(proj, proj, conv0, h0, p["lru_conv_w"], p["lru_conv_b"], p["lru_wa"], p["lru_ba"], p["lru_wx"], p["lru_bx"],
      p["lru_lambda"], p["lru_norm"])


def _ret_body(q_ref, k_ref, v_ref, g_ref, cq_ref, sq_ref, ck_ref, sk_ref, s0_ref, nrm_ref,
              out_ref, snew_ref, state):
    c = q_ref.shape[0]

    @pl.when(pl.program_id(1) == 0)
    def _():
        state[...] = s0_ref[...]

    ii = lax.broadcasted_iota(jnp.int32, (c, c), 0)
    jj = lax.broadcasted_iota(jnp.int32, (c, c), 1)
    tri = ii >= jj
    dpos = jnp.maximum(ii - jj, 0).astype(F32)
    rpos = lax.broadcasted_iota(jnp.int32, (c, RET_DK), 0).astype(F32)
    cq, sq, ck, sk = cq_ref[...], sq_ref[...], ck_ref[...], sk_ref[...]
    outs = []
    for h in range(RET_HEADS):
        lg = RET_LOG_GAMMA[h]
        decay = jnp.where(tri, jnp.exp(dpos * lg), 0.0)
        e_g = jnp.exp((rpos + 1.0) * lg)
        w_tail = jnp.exp((c - 1.0 - rpos) * lg)
        a_chunk = math.exp(c * lg)
        qh = q_ref[:, h * RET_DK:(h + 1) * RET_DK].astype(F32)
        kh = k_ref[:, h * RET_DK:(h + 1) * RET_DK].astype(F32)
        qr = qh * cq + pltpu.roll(qh, RET_DK // 2, 1) * sq
        kr = kh * ck + pltpu.roll(kh, RET_DK // 2, 1) * sk
        vh = v_ref[:, h * RET_DV:(h + 1) * RET_DV]
        sh = state[h]
        scores = _dot_nt(qr.astype(BF16), kr.astype(BF16)) * decay
        y = _dot(scores.astype(BF16), vh) + _dot((qr * e_g).astype(BF16), sh.astype(BF16))
        state[h] = a_chunk * sh + _dot_tn((kr * w_tail).astype(BF16), vh)
        mu = jnp.mean(y, axis=-1, keepdims=True)
        d = y - mu
        var = jnp.mean(d * d, axis=-1, keepdims=True)
        outs.append(d * lax.rsqrt(var + EPS))
    ro = jnp.concatenate(outs, axis=-1) * nrm_ref[...]
    out_ref[...] = (ro * _silu(g_ref[...].astype(F32))).astype(out_ref.dtype)
    snew_ref[...] = state[...]


def _ret_mixer(proj, s0, rope, p, *, batch, seq, chunk):
    c = min(chunk, seq)
    nt = seq // c
    row = lambda b, t: b * nt + t
    tab = pl.BlockSpec((c, RET_DK), lambda b, t: (t, 0))
    return pl.pallas_call(
        _ret_body,
        grid=(batch, nt),
        in_specs=[
            pl.BlockSpec((c, RET_QK), lambda b, t: (row(b, t), 16)),
            pl.BlockSpec((c, RET_QK), lambda b, t: (row(b, t), 17)),
            pl.BlockSpec((c, RET_V), lambda b, t: (row(b, t), 6)),
            pl.BlockSpec((c, RET_V), lambda b, t: (row(b, t), 7)),
            tab, tab, tab, tab,
            pl.BlockSpec((None, RET_HEADS, RET_DK, RET_DV), lambda b, t: (b, 0, 0, 0)),
            pl.BlockSpec((1, RET_V), lambda b, t: (0, 0)),
        ],
        out_specs=[
            pl.BlockSpec((c, RET_V), lambda b, t: (row(b, t), 0)),
            pl.BlockSpec((None, RET_HEADS, RET_DK, RET_DV), lambda b, t: (b, 0, 0, 0)),
        ],
        out_shape=[
            jax.ShapeDtypeStruct((batch * seq, RET_V), BF16),
            jax.ShapeDtypeStruct((batch, RET_HEADS, RET_DK, RET_DV), F32),
        ],
        scratch_shapes=[pltpu.VMEM((RET_HEADS, RET_DK, RET_DV), F32)],
        compiler_params=_params(("arbitrary", "arbitrary")),
        name="ret_mixer",
    )(proj, proj, proj, proj, rope[0], rope[1], rope[2], rope[3], s0, p["ret_norm"])


def _cumsum_rows(x):
    n = x.shape[0]
    rows = lax.broadcasted_iota(jnp.int32, x.shape, 0)
    s = 1
    while s < n:
        x = x + jnp.where(rows >= s, pltpu.roll(x, s, 0), 0.0)
        s *= 2
    return x


def _ssd_body(z_ref, x_ref, b_ref, c_ref, dt_ref, conv0_ref, s0_ref, cw_ref, cb_ref, dtb_ref, alog_ref, dexp_ref,
              nrm_ref, out_ref, snew_ref, ctail_ref, zbuf, state, *, n_steps):
    c = x_ref.shape[0]
    gh, hd = SSD_GROUP_HEADS, SSD_HEADDIM

    @pl.when(pl.program_id(1) == 0)
    def _():
        zbuf[0:SUBLANES, :] = conv0_ref[...]
        for g in range(SSD_GROUPS):
            state[g] = jnp.concatenate([s0_ref[g * gh + hh] for hh in range(gh)], axis=-1)

    zbuf[pl.ds(SUBLANES, c), 0:SSD_INNER] = x_ref[...].astype(F32)
    zbuf[pl.ds(SUBLANES, c), SSD_INNER:SSD_INNER + SSD_BC] = b_ref[...].astype(F32)
    zbuf[pl.ds(SUBLANES, c), SSD_INNER + SSD_BC:SSD_CONV_DIM] = c_ref[...].astype(F32)
    xbc = _silu(_causal_conv(zbuf, c, cw_ref, cb_ref))
    tail = zbuf[pl.ds(c, SUBLANES), :]
    zbuf[0:SUBLANES, :] = tail
    ctail_ref[...] = tail

    xs = xbc[:, 0:SSD_INNER]
    bm = xbc[:, SSD_INNER:SSD_INNER + SSD_BC]
    cm = xbc[:, SSD_INNER + SSD_BC:SSD_CONV_DIM]

    dt = _softplus(dt_ref[...] + dtb_ref[...])
    a_neg = -jnp.exp(alog_ref[...])
    g_cum = _cumsum_rows(dt * a_neg)
    g_cum_t = g_cum.T
    g_last = g_cum[c - 1:c, :]
    e_g = jnp.exp(g_cum)
    w_tail = jnp.exp(g_last - g_cum)
    a_chunk = jnp.exp(g_last)

    ii = lax.broadcasted_iota(jnp.int32, (c, c), 0)
    jj = lax.broadcasted_iota(jnp.int32, (c, c), 1)
    tri = ii >= jj

    ys = []
    for g in range(SSD_GROUPS):
        cg = cm[:, g * SSD_STATE:(g + 1) * SSD_STATE].astype(BF16)
        bg = bm[:, g * SSD_STATE:(g + 1) * SSD_STATE].astype(BF16)
        cb = _dot_nt(cg, bg)
        sg = state[g]
        y_inter = _dot(cg, sg.astype(BF16))
        xw, arow = [], []
        for hh in range(gh):
            h = g * gh + hh
            xh = xs[:, h * hd:(h + 1) * hd]
            xdt = xh * dt[:, h:h + 1]
            decay = jnp.exp(jnp.where(tri, g_cum[:, h:h + 1] - g_cum_t[h:h + 1, :], NEG_BIG))
            m = (cb * decay).astype(BF16)
            ys.append(_dot(m, xdt.astype(BF16)) + e_g[:, h:h + 1] * y_inter[:, hh * hd:(hh + 1) * hd])
            xw.append((xdt * w_tail[:, h:h + 1]).astype(BF16))
            arow.append(jnp.broadcast_to(a_chunk[:, h:h + 1], (1, hd)))
        d_state = _dot_tn(bg, jnp.concatenate(xw, axis=-1))
        state[g] = jnp.concatenate(arow, axis=-1) * sg + d_state

    so = jnp.concatenate(ys, axis=-1) + dexp_ref[...] * xs
    u = so * _silu(z_ref[...].astype(F32))
    ms = jnp.mean(u * u, axis=-1, keepdims=True)
    out_ref[...] = (u * lax.rsqrt(ms + EPS) * nrm_ref[...]).astype(out_ref.dtype)

    @pl.when(pl.program_id(1) == n_steps - 1)
    def _():
        for g in range(SSD_GROUPS):
            sg = state[g]
            for hh in range(gh):
                snew_ref[g * gh + hh] = sg[:, hh * hd:(hh + 1) * hd]


def _ssd_mixer(proj, dt_raw, conv0, s0, p, *, batch, seq, chunk):
    c = min(chunk, seq)
    nt = seq // c
    row = lambda b, t: b * nt + t
    const2 = lambda b, t: (0, 0)
    return pl.pallas_call(
        functools.partial(_ssd_body, n_steps=nt),
        grid=(batch, nt),
        in_specs=[
            pl.BlockSpec((c, SSD_INNER), lambda b, t: (row(b, t), 0)),
            pl.BlockSpec((c, SSD_INNER), lambda b, t: (row(b, t), 1)),
            pl.BlockSpec((c, SSD_BC), lambda b, t: (row(b, t), 18)),
            pl.BlockSpec((c, SSD_BC), lambda b, t: (row(b, t), 19)),
            pl.BlockSpec((c, LANES), lambda b, t: (row(b, t), 0)),
            pl.BlockSpec((None, SUBLANES, SSD_CONV_DIM), lambda b, t: (b, 0, 0)),
            pl.BlockSpec((None, SSD_HEADS, SSD_STATE, SSD_HEADDIM), lambda b, t: (b, 0, 0, 0)),
            pl.BlockSpec((CONV_W, SSD_CONV_DIM), const2),
            pl.BlockSpec((1, SSD_CONV_DIM), const2),
            pl.BlockSpec((1, LANES), const2),
            pl.BlockSpec((1, LANES), const2),
            pl.BlockSpec((1, SSD_INNER), const2),
            pl.BlockSpec((1, SSD_INNER), const2),
        ],
        out_specs=[
            pl.BlockSpec((c, SSD_INNER), lambda b, t: (row(b, t), 0)),
            pl.BlockSpec((None, SSD_HEADS, SSD_STATE, SSD_HEADDIM), lambda b, t: (b, 0, 0, 0)),
            pl.BlockSpec((None, SUBLANES, SSD_CONV_DIM), lambda b, t: (b, 0, 0)),
        ],
        out_shape=[
            jax.ShapeDtypeStruct((batch * seq, SSD_INNER), BF16),
            jax.ShapeDtypeStruct((batch, SSD_HEADS, SSD_STATE, SSD_HEADDIM), F32),
            jax.ShapeDtypeStruct((batch, SUBLANES, SSD_CONV_DIM), F32),
        ],
        scratch_shapes=[
            pltpu.VMEM((c + SUBLANES, SSD_CONV_DIM), F32),
            pltpu.VMEM((SSD_GROUPS, SSD_STATE, SSD_GROUP_HEADS * SSD_HEADDIM), F32),
        ],
        compiler_params=_params(("arbitrary", "arbitrary")),
        name="ssd_mixer",
    )(proj, proj, proj, proj, dt_raw, conv0, s0, p["ssd_conv_w"], p["ssd_conv_b"], p["ssd_dt_bias"], p["ssd_a_log"],
      p["ssd_d_exp"], p["ssd_norm"])


def _xattn_body(x_ref, nx_ref, wq_ref, mk_ref, mv_ref, wo_ref, nf_ref, o_ref, hn_ref, *, final, row_chunk):
    _rmsnorm_rows_to(hn_ref, x_ref, nx_ref, row_chunk)
    q = _dot(hn_ref[...], wq_ref[...])
    scale = XATT_HD ** -0.5
    os = []
    for h in range(XATT_HEADS):
        sl = slice(h * XATT_HD, (h + 1) * XATT_HD)
        s = _dot_nt(q[:, sl].astype(BF16), mk_ref[:, sl].astype(BF16)) * scale
        mx = jnp.max(s, axis=-1, keepdims=True)
        pr = jnp.exp(s - mx)
        den = jnp.sum(pr, axis=-1, keepdims=True)
        os.append(_dot(pr.astype(BF16), mv_ref[:, sl].astype(BF16)) / den)
    o = jnp.concatenate(os, axis=-1).astype(BF16)
    o_ref[...] = x_ref[...] + _dot(o, wo_ref[...])
    if final:
        _rmsnorm_rows_to(o_ref, o_ref, nf_ref, row_chunk)


def _cross_attn(x, mem_k, mem_v, nx, wq, wo, nf, *, batch, seq, tm, final):
    d = x.shape[1]
    tm = min(tm, seq)
    nt = seq // tm
    const2 = lambda b, t: (0, 0)
    return pl.pallas_call(
        functools.partial(_xattn_body, final=final, row_chunk=min(32, tm)),
        grid=(batch, nt),
        in_specs=[
            pl.BlockSpec((tm, d), lambda b, t: (b * nt + t, 0)),
            pl.BlockSpec((1, d), const2),
            pl.BlockSpec((d, XATT_DIM), const2),
            pl.BlockSpec((None, N_MEM, XATT_DIM), lambda b, t: (b, 0, 0)),
            pl.BlockSpec((None, N_MEM, XATT_DIM), lambda b, t: (b, 0, 0)),
            pl.BlockSpec((XATT_DIM, d), const2),
            pl.BlockSpec((1, d), const2),
        ],
        out_specs=pl.BlockSpec((tm, d), lambda b, t: (b * nt + t, 0)),
        out_shape=jax.ShapeDtypeStruct(x.shape, F32),
        scratch_shapes=[pltpu.VMEM((tm, d), BF16)],
        compiler_params=_params(("arbitrary", "arbitrary")),
        name="cross_attn",
    )(x, nx.reshape(1, d), wq, mem_k, mem_v, wo, nf.reshape(1, d))


def _pad_state_rows(s):
    return jnp.pad(s.astype(F32), ((0, 0), (SUBLANES - (CONV_W - 1), 0), (0, 0)))


def _rope_tables(pos):
    half = RET_DK // 2
    inv = ROPE_BASE ** (-jnp.arange(half, dtype=F32) / half)
    ang = pos[:, None] * inv[None, :]
    cos, sin = jnp.cos(ang), jnp.sin(ang)
    cq = jnp.concatenate([cos, cos], axis=-1)
    sq = jnp.concatenate([-sin, sin], axis=-1)
    ks = RET_DK ** -0.5
    return cq, sq, cq * ks, sq * ks


def _layer_params(l, norm_mix, w_in, lru_conv_w, lru_conv_b, lru_wa, lru_ba, lru_wx, lru_bx, lru_lambda, lru_norm,
                  ret_norm, ssd_conv_w, ssd_conv_b, ssd_dt_bias, ssd_a_log, ssd_d, ssd_norm, w_out, norm_xattn,
                  norm_mem, w_q_mem, w_k_mem, w_v_mem, w_o_mem):
    w = w_in[l]
    o_lx, o_lg, o_rq, o_rk, o_rv, o_rg = 0, 1024, 2048, 2560, 3072, 4096
    o_sz, o_sx = 5120, 7168
    o_sb, o_sc, o_dt = o_sx + SSD_INNER, o_sx + SSD_INNER + SSD_BC, o_sx + SSD_CONV_DIM
    sl = lambda a, n: w[:, a:a + n]
    w_main = jnp.concatenate([
        sl(o_sz, SSD_INNER), sl(o_sx, SSD_INNER), sl(o_lx, LRU_WIDTH), sl(o_lg, LRU_WIDTH), sl(o_rv, RET_V),
        sl(o_rg, RET_V), sl(o_rq, RET_QK), sl(o_rk, RET_QK), sl(o_sb, SSD_BC), sl(o_sc, SSD_BC)], axis=1).astype(BF16)
    w_dt = jnp.pad(sl(o_dt, SSD_HEADS), ((0, 0), (0, LANES - SSD_HEADS))).astype(BF16)
    row = lambda v: v.reshape(1, -1).astype(F32)
    pad_lanes = lambda v: jnp.pad(v.astype(F32), (0, LANES - v.shape[0])).reshape(1, LANES)
    return dict(
        norm_mix=norm_mix[l], w_main=w_main, w_dt=w_dt,
        lru_conv_w=lru_conv_w[l], lru_conv_b=row(lru_conv_b[l]), lru_wa=lru_wa[l].astype(BF16), lru_ba=row(lru_ba[l]),
        lru_wx=lru_wx[l].astype(BF16), lru_bx=row(lru_bx[l]), lru_lambda=row(lru_lambda[l]), lru_norm=row(lru_norm[l]),
        ret_norm=row(ret_norm[l]),
        ssd_conv_w=ssd_conv_w[l], ssd_conv_b=row(ssd_conv_b[l]), ssd_dt_bias=pad_lanes(ssd_dt_bias[l]),
        ssd_a_log=pad_lanes(ssd_a_log[l]), ssd_d_exp=row(jnp.repeat(ssd_d[l], SSD_HEADDIM)), ssd_norm=row(ssd_norm[l]),
        w_out=w_out[l].astype(BF16), norm_xattn=norm_xattn[l], norm_mem=norm_mem[l],
        w_q=w_q_mem[l].astype(BF16), w_kv=jnp.concatenate([w_k_mem[l], w_v_mem[l]], axis=1).astype(BF16),
        w_o=w_o_mem[l].astype(BF16))


def _mixer_sublayer(x, rope, lru_h0, lru_conv0, ret_s0, ssd_s0, ssd_conv0, p, *, batch, seq):
    proj, dt_raw = _norm_matmul(x, p["norm_mix"], p["w_main"], p["w_dt"], out_dtype=BF16, tm=512, tn=1024)
    lru_out, lru_h1, lru_ct = _lru_mixer(proj, _pad_state_rows(lru_conv0), lru_h0.reshape(batch, 1, LRU_WIDTH), p,
                                         batch=batch, seq=seq, tb=128)
    ret_out, ret_s1 = _ret_mixer(proj, ret_s0, rope, p, batch=batch, seq=seq, chunk=128)
    ssd_out, ssd_s1, ssd_ct = _ssd_mixer(proj, dt_raw, _pad_state_rows(ssd_conv0), ssd_s0, p,
                                         batch=batch, seq=seq, chunk=128)
    x1 = _out_proj(x, lru_out, ret_out, ssd_out, p["w_out"], tm=512, tn=1024)
    tail = SUBLANES - (CONV_W - 1)
    states = (lru_h1.reshape(batch, LRU_WIDTH), lru_ct[:, tail:, :], ret_s1, ssd_s1, ssd_ct[:, tail:, :])
    return x1, states


def kernel(x_prompt, x_sample, mem_prompt, state_lru_h, state_lru_conv, state_ret, state_ssd, state_ssd_conv, cache_mem_k, cache_mem_v, norm_mix, w_in, lru_conv_w, lru_conv_b, lru_wa, lru_ba, lru_wx, lru_bx, lru_lambda, lru_norm, ret_norm, ssd_conv_w, ssd_conv_b, ssd_dt_bias, ssd_a_log, ssd_d, ssd_norm, w_out, norm_xattn, norm_mem, w_q_mem, w_k_mem, w_v_mem, w_o_mem, norm_final):
    bp, tp, d = x_prompt.shape
    bs, ts, _ = x_sample.shape
    depth = w_in.shape[0]
    assert tp % SUBLANES == 0 and ts % SUBLANES == 0 and tp >= SUBLANES and ts >= SUBLANES
    rope_p = _rope_tables(jnp.arange(tp, dtype=F32))
    rope_s = _rope_tables(PAST_LEN + jnp.arange(ts, dtype=F32))
    z_lru_h = jnp.zeros((bp, LRU_WIDTH), F32)
    z_lru_conv = jnp.zeros((bp, CONV_W - 1, LRU_WIDTH), F32)
    z_ret = jnp.zeros((bp, RET_HEADS, RET_DK, RET_DV), F32)
    z_ssd = jnp.zeros((bp, SSD_HEADS, SSD_STATE, SSD_HEADDIM), F32)
    z_ssd_conv = jnp.zeros((bp, CONV_W - 1, SSD_CONV_DIM), F32)

    xp = x_prompt.reshape(bp * tp, d)
    xs = x_sample.reshape(bs * ts, d)
    mem = mem_prompt.reshape(bp * N_MEM, d)
    st_p = [[] for _ in range(5)]
    st_s = [[] for _ in range(5)]
    mk_p, mv_p = [], []
    for l in range(depth):
        p = _layer_params(l, norm_mix, w_in, lru_conv_w, lru_conv_b, lru_wa, lru_ba, lru_wx, lru_bx, lru_lambda,
                          lru_norm, ret_norm, ssd_conv_w, ssd_conv_b, ssd_dt_bias, ssd_a_log, ssd_d, ssd_norm, w_out,
                          norm_xattn, norm_mem, w_q_mem, w_k_mem, w_v_mem, w_o_mem)
        final = l == depth - 1
        xp, st = _mixer_sublayer(xp, rope_p, z_lru_h, z_lru_conv, z_ret, z_ssd, z_ssd_conv, p, batch=bp, seq=tp)
        mkv = _norm_matmul(mem, p["norm_mem"], p["w_kv"], out_dtype=F32, tm=512, tn=1024)
        mk = mkv[:, :XATT_DIM].reshape(bp, N_MEM, XATT_DIM)
        mv = mkv[:, XATT_DIM:].reshape(bp, N_MEM, XATT_DIM)
        xp = _cross_attn(xp, mk, mv, p["norm_xattn"], p["w_q"], p["w_o"], norm_final, batch=bp, seq=tp, tm=256,
                         final=final)
        for acc, v in zip(st_p, st):
            acc.append(v)
        mk_p.append(mk.reshape(bp, N_MEM, XATT_HEADS, XATT_HD))
        mv_p.append(mv.reshape(bp, N_MEM, XATT_HEADS, XATT_HD))
        xs, st = _mixer_sublayer(xs, rope_s, state_lru_h[l], state_lru_conv[l], state_ret[l], state_ssd[l],
                                 state_ssd_conv[l], p, batch=bs, seq=ts)
        xs = _cross_attn(xs, cache_mem_k[l].reshape(bs, N_MEM, XATT_DIM), cache_mem_v[l].reshape(bs, N_MEM, XATT_DIM),
                         p["norm_xattn"], p["w_q"], p["w_o"], norm_final, batch=bs, seq=ts, tm=256, final=final)
        for acc, v in zip(st_s, st):
            acc.append(v)

    y_prompt = xp.reshape(bp, tp, d)
    y_sample = xs.reshape(bs, ts, d)
    return (y_prompt, y_sample,
            jnp.stack(st_p[0]), jnp.stack(st_p[1]), jnp.stack(st_p[2]), jnp.stack(st_p[3]), jnp.stack(st_p[4]),
            jnp.stack(mk_p), jnp.stack(mv_p),
            jnp.stack(st_s[0]), jnp.stack(st_s[1]), jnp.stack(st_s[2]), jnp.stack(st_s[3]), jnp.stack(st_s[4]))
```

```python
import functools
import math

import numpy as np
import jax
import jax.numpy as jnp
from jax import lax
from jax.experimental import pallas as pl
from jax.experimental.pallas import tpu as pltpu

F32 = jnp.float32
BF16 = jnp.bfloat16

EPS = 1e-6
CONV_W = 4
PAST_LEN = 4096
N_MEM = 256
LRU_WIDTH = 1024
LRU_BLOCKS = 4
LRU_BLOCK = LRU_WIDTH // LRU_BLOCKS
LRU_C = 8.0
RET_HEADS = 4
RET_DK = 128
RET_DV = 256
RET_QK = RET_HEADS * RET_DK
RET_V = RET_HEADS * RET_DV
ROPE_BASE = 10000.0
SSD_INNER = 2048
SSD_HEADDIM = 64
SSD_HEADS = SSD_INNER // SSD_HEADDIM
SSD_STATE = 128
SSD_GROUPS = 4
SSD_GROUP_HEADS = SSD_HEADS // SSD_GROUPS
SSD_GROUP_WIDTH = SSD_GROUP_HEADS * SSD_HEADDIM
SSD_BC = SSD_GROUPS * SSD_STATE
SSD_CONV_DIM = SSD_INNER + 2 * SSD_BC
XATT_HEADS = 4
XATT_HD = 128
XATT_DIM = XATT_HEADS * XATT_HD

SUBLANES = 8
LANES = 128
BF16_ROWS = 2 * SUBLANES
VMEM_LIMIT_BYTES = 56 * 1024 * 1024

OFF_LX, OFF_LG, OFF_RQ, OFF_RK, OFF_RV, OFF_RG, OFF_SZ, OFF_SX = 0, 1024, 2048, 2560, 3072, 4096, 5120, 7168
OFF_SB = OFF_SX + SSD_INNER
OFF_SC = OFF_SB + SSD_BC
OFF_DT = OFF_SX + SSD_CONV_DIM

RET_LOG_GAMMA = tuple(math.log1p(-(2.0 ** (-5.0 - h))) for h in range(RET_HEADS))
LOG2E = math.log2(math.e)

TM_PROJ, TN_PROJ = 512, 1024
TM_IN_PROJ = 1024
TM_XATT = 256
CHUNK = 128


def _params(semantics):
    return pltpu.CompilerParams(dimension_semantics=semantics, vmem_limit_bytes=VMEM_LIMIT_BYTES)


def _sigmoid(x):
    return 1.0 / (1.0 + jnp.exp(-x))


def _silu(x):
    return x * _sigmoid(x)


def _softplus(x):
    return jnp.maximum(x, 0.0) + jnp.log1p(jnp.exp(-jnp.abs(x)))


def _dot(a, b):
    return jnp.dot(a, b, preferred_element_type=F32)


def _dot_nt(a, b):
    return lax.dot_general(a, b, (((1,), (1,)), ((), ())), preferred_element_type=F32)


def _dot_tn(a, b):
    return lax.dot_general(a, b, (((0,), (0,)), ((), ())), preferred_element_type=F32)


def _rmsnorm_rows_to(dst_ref, src_ref, gain_ref, part_ref):
    tm, d = src_ref.shape
    rc = BF16_ROWS
    lane_tiles = d // LANES

    def sum_squares(r, carry):
        rows = pl.ds(pl.multiple_of(r * rc, rc), rc)
        acc = None
        for j in range(lane_tiles):
            v = src_ref[rows, j * LANES:(j + 1) * LANES].astype(F32)
            acc = v * v if acc is None else acc + v * v
        part_ref[rows, :] = acc
        return carry

    lax.fori_loop(0, tm // rc, sum_squares, 0)
    ms = jnp.sum(part_ref[...], axis=-1, keepdims=True) * (1.0 / d)
    part_ref[...] = jnp.broadcast_to(lax.rsqrt(ms + EPS), part_ref.shape)

    def scale(r, carry):
        rows = pl.ds(pl.multiple_of(r * rc, rc), rc)
        rs = part_ref[rows, :]
        for j in range(lane_tiles):
            sl = slice(j * LANES, (j + 1) * LANES)
            dst_ref[rows, sl] = (src_ref[rows, sl].astype(F32) * rs * gain_ref[:, sl]).astype(dst_ref.dtype)
        return carry

    lax.fori_loop(0, tm // rc, scale, 0)


def _norm_matmul_body(x_ref, g_ref, w_ref, o_ref, hn_ref, part_ref):
    @pl.when(pl.program_id(1) == 0)
    def _():
        _rmsnorm_rows_to(hn_ref, x_ref, g_ref, part_ref)

    o_ref[...] = _dot(hn_ref[...], w_ref[...]).astype(o_ref.dtype)


def _norm_matmul(x, g, w, *, out_dtype, tm, tn):
    m, k = x.shape
    n = w.shape[1]
    tm = min(tm, m)
    tn = min(tn, n)
    assert m % tm == 0 and n % tn == 0
    return pl.pallas_call(
        _norm_matmul_body,
        grid=(m // tm, n // tn),
        in_specs=[
            pl.BlockSpec((tm, k), lambda i, j: (i, 0)),
            pl.BlockSpec((1, k), lambda i, j: (0, 0)),
            pl.BlockSpec((k, tn), lambda i, j: (0, j)),
        ],
        out_specs=pl.BlockSpec((tm, tn), lambda i, j: (i, j)),
        out_shape=jax.ShapeDtypeStruct((m, n), out_dtype),
        scratch_shapes=[pltpu.VMEM((tm, k), BF16), pltpu.VMEM((tm, LANES), F32)],
        compiler_params=_params(("arbitrary", "arbitrary")),
        name="norm_matmul",
    )(x, g.reshape(1, k), w)


def _rmsnorm_body(x_ref, g_ref, o_ref, part_ref):
    _rmsnorm_rows_to(o_ref, x_ref, g_ref, part_ref)


def _rmsnorm_bf16(x, g, *, tm):
    m, k = x.shape
    tm = min(tm, m)
    assert m % tm == 0
    return pl.pallas_call(
        _rmsnorm_body,
        grid=(m // tm,),
        in_specs=[pl.BlockSpec((tm, k), lambda i: (i, 0)), pl.BlockSpec((1, k), lambda i: (0, 0))],
        out_specs=pl.BlockSpec((tm, k), lambda i: (i, 0)),
        out_shape=jax.ShapeDtypeStruct((m, k), BF16),
        scratch_shapes=[pltpu.VMEM((tm, LANES), F32)],
        compiler_params=_params(("arbitrary",)),
        name="rmsnorm_bf16",
    )(x, g.reshape(1, k))


def _in_proj_body(hn_ref, w_ref, w2_ref, o_ref, o2_ref):
    @pl.when(pl.program_id(1) == 0)
    def _():
        o2_ref[...] = _dot(hn_ref[...], w2_ref[...])

    o_ref[...] = _dot(hn_ref[...], w_ref[...]).astype(o_ref.dtype)


def _in_proj(hn, w, w2, *, tm, tn):
    m, k = hn.shape
    n, n2 = w.shape[1], w2.shape[1]
    tm = min(tm, m)
    tn = min(tn, n)
    assert m % tm == 0 and n % tn == 0
    return pl.pallas_call(
        _in_proj_body,
        grid=(m // tm, n // tn),
        in_specs=[
            pl.BlockSpec((tm, k), lambda i, j: (i, 0)),
            pl.BlockSpec((k, tn), lambda i, j: (0, j)),
            pl.BlockSpec((k, n2), lambda i, j: (0, 0)),
        ],
        out_specs=[pl.BlockSpec((tm, tn), lambda i, j: (i, j)), pl.BlockSpec((tm, n2), lambda i, j: (i, 0))],
        out_shape=[jax.ShapeDtypeStruct((m, n), BF16), jax.ShapeDtypeStruct((m, n2), F32)],
        compiler_params=_params(("arbitrary", "arbitrary")),
        name="in_proj",
    )(hn, w, w2)


def _out_proj_body(x_ref, a1_ref, a2_ref, a3_ref, w1_ref, w2_ref, w3_ref, o_ref):
    acc = _dot(a1_ref[...], w1_ref[...])
    acc = acc + _dot(a2_ref[...], w2_ref[...])
    acc = acc + _dot(a3_ref[...], w3_ref[...])
    o_ref[...] = x_ref[...] + acc


def _out_proj(x, lru_out, ret_out, ssd_out, w_out_bf16, *, tm, tn):
    m, n = x.shape
    tm = min(tm, m)
    tn = min(tn, n)
    assert m % tm == 0 and n % tn == 0
    return pl.pallas_call(
        _out_proj_body,
        grid=(m // tm, n // tn),
        in_specs=[
            pl.BlockSpec((tm, tn), lambda i, j: (i, j)),
            pl.BlockSpec((tm, LRU_WIDTH), lambda i, j: (i, 0)),
            pl.BlockSpec((tm, RET_V), lambda i, j: (i, 0)),
            pl.BlockSpec((tm, SSD_INNER), lambda i, j: (i, 0)),
            pl.BlockSpec((LRU_WIDTH, tn), lambda i, j: (0, j)),
            pl.BlockSpec((RET_V, tn), lambda i, j: (1, j)),
            pl.BlockSpec((SSD_INNER, tn), lambda i, j: (1, j)),
        ],
        out_specs=pl.BlockSpec((tm, tn), lambda i, j: (i, j)),
        out_shape=jax.ShapeDtypeStruct((m, n), F32),
        compiler_params=_params(("arbitrary", "arbitrary")),
        name="out_proj",
    )(x, lru_out, ret_out, ssd_out, w_out_bf16, w_out_bf16, w_out_bf16)


HIST_ROWS = BF16_ROWS


def _stage_rows(c):
    return -(-(HIST_ROWS + c) // LANES) * LANES


def _shift_matrix(c):
    s = np.zeros((CONV_W - 1, c, _stage_rows(c)), np.float32)
    for k in range(CONV_W - 1):
        back = k + 1
        for t in range(c):
            if t >= back:
                s[k, t, HIST_ROWS + t - back] = 1.0
            else:
                j = (CONV_W - 1) + t - back
                s[k, t, j] = s[k, t, 3 + j] = s[k, t, SUBLANES + j] = 1.0
    return jnp.asarray(s.reshape((CONV_W - 1) * c, -1), dtype=BF16)


def _history_tile(tail8):
    r = pltpu.roll(tail8, CONV_W - 1, 0)
    hi = r.astype(BF16).astype(F32)
    rest = r - hi
    mid = rest.astype(BF16).astype(F32)
    lo = rest - mid
    rows = lax.broadcasted_iota(jnp.int32, r.shape, 0)
    top = jnp.where(rows < 3, hi, jnp.where(rows < 6, pltpu.roll(mid, CONV_W - 1, 0), 0.0))
    bot = jnp.where(rows < 3, lo, 0.0)
    return jnp.concatenate([top, bot], axis=0).astype(BF16)


def _conv_from_stage(stage, shift_ref, x_f32, c, cw_ref, cb_ref):
    sh = _dot(shift_ref[...], stage[...])
    acc = sh[2 * c:3 * c] * cw_ref[0:1, :]
    acc = acc + sh[c:2 * c] * cw_ref[1:2, :]
    acc = acc + sh[0:c] * cw_ref[2:3, :]
    acc = acc + x_f32 * cw_ref[3:4, :]
    return cb_ref[...] + acc


def _lru_body(lx_ref, lg_ref, conv0_ref, h0_ref, shift_ref, cw_ref, cb_ref, wa_ref, ba_ref, wx_ref, bx_ref, lam_ref,
              nrm_ref, out_ref, hlast_ref, ctail_ref, stage, hcar):
    tb = lx_ref.shape[0]

    @pl.when(pl.program_id(1) == 0)
    def _():
        stage[...] = jnp.zeros(stage.shape, stage.dtype)
        stage[0:HIST_ROWS, :] = _history_tile(conv0_ref[...])
        hcar[...] = h0_ref[...]

    stage[pl.ds(HIST_ROWS, tb), :] = lx_ref[...]
    x_in = lx_ref[...].astype(F32)
    xc = _conv_from_stage(stage, shift_ref, x_in, tb, cw_ref, cb_ref)
    tail = x_in[tb - SUBLANES:tb]
    stage[0:HIST_ROWS, :] = _history_tile(tail)
    ctail_ref[...] = tail

    xb = xc.astype(BF16)
    ra, ri = [], []
    for n in range(LRU_BLOCKS):
        xn = xb[:, n * LRU_BLOCK:(n + 1) * LRU_BLOCK]
        ra.append(_dot(xn, wa_ref[n]))
        ri.append(_dot(xn, wx_ref[n]))
    r = _sigmoid(jnp.concatenate(ra, axis=-1) + ba_ref[...])
    i = _sigmoid(jnp.concatenate(ri, axis=-1) + bx_ref[...])
    log_a = r * ((-LRU_C) * _softplus(-lam_ref[...]))
    a = jnp.exp(log_a)
    th = jnp.tanh(log_a)
    b = jnp.sqrt(-2.0 * th / (1.0 - th)) * (i * xc)

    ng = tb // SUBLANES
    a3 = a.reshape(ng, SUBLANES, LRU_WIDTH)
    b3 = b.reshape(ng, SUBLANES, LRU_WIDTH)
    rows = lax.broadcasted_iota(jnp.int32, a3.shape, 1)
    s = 1
    while s < SUBLANES:
        keep = rows >= s
        a_sh = jnp.where(keep, pltpu.roll(a3, s, 1), 1.0)
        b_sh = jnp.where(keep, pltpu.roll(b3, s, 1), 0.0)
        b3 = a3 * b_sh + b3
        a3 = a3 * a_sh
        s *= 2
    hprev = hcar[...]
    hs = []
    for g in range(ng):
        hg = b3[g] + a3[g] * hprev
        hprev = hg[SUBLANES - 1:SUBLANES]
        hs.append(hg)
    h = jnp.concatenate(hs, axis=0)
    hcar[...] = hprev
    hlast_ref[...] = hprev

    ms = jnp.mean(h * h, axis=-1, keepdims=True)
    out = h * lax.rsqrt(ms + EPS) * nrm_ref[...]
    out_ref[...] = (out * _silu(lg_ref[...].astype(F32))).astype(out_ref.dtype)


def _lru_mixer(proj, conv0, h0, p, *, batch, seq, tb):
    tb = min(tb, seq)
    nt = seq // tb
    w = LRU_WIDTH
    row = lambda b, t: b * nt + t
    const2 = lambda b, t: (0, 0)
    shift = _shift_matrix(tb)
    return pl.pallas_call(
        _lru_body,
        grid=(batch, nt),
        in_specs=[
            pl.BlockSpec((tb, w), lambda b, t: (row(b, t), OFF_LX // w)),
            pl.BlockSpec((tb, w), lambda b, t: (row(b, t), OFF_LG // w)),
            pl.BlockSpec((None, SUBLANES, w), lambda b, t: (b, 0, 0)),
            pl.BlockSpec((None, 1, w), lambda b, t: (b, 0, 0)),
            pl.BlockSpec(shift.shape, const2),
            pl.BlockSpec((CONV_W, w), const2),
            pl.BlockSpec((1, w), const2),
            pl.BlockSpec((LRU_BLOCKS, LRU_BLOCK, LRU_BLOCK), lambda b, t: (0, 0, 0)),
            pl.BlockSpec((1, w), const2),
            pl.BlockSpec((LRU_BLOCKS, LRU_BLOCK, LRU_BLOCK), lambda b, t: (0, 0, 0)),
            pl.BlockSpec((1, w), const2),
            pl.BlockSpec((1, w), const2),
            pl.BlockSpec((1, w), const2),
        ],
        out_specs=[
            pl.BlockSpec((tb, w), lambda b, t: (row(b, t), 0)),
            pl.BlockSpec((None, 1, w), lambda b, t: (b, 0, 0)),
            pl.BlockSpec((None, SUBLANES, w), lambda b, t: (b, 0, 0)),
        ],
        out_shape=[
            jax.ShapeDtypeStruct((batch * seq, w), BF16),
            jax.ShapeDtypeStruct((batch, 1, w), F32),
            jax.ShapeDtypeStruct((batch, SUBLANES, w), F32),
        ],
        scratch_shapes=[pltpu.VMEM((_stage_rows(tb), w), BF16), pltpu.VMEM((1, w), F32)],
        compiler_params=_params(("arbitrary", "arbitrary")),
        name="lru_mixer",
    )(proj, proj, conv0, h0, shift, p["lru_conv_w"], p["lru_conv_b"], p["lru_wa"], p["lru_ba"], p["lru_wx"],
      p["lru_bx"], p["lru_lambda"], p["lru_norm"])


def _ret_body(q_ref, k_ref, v_ref, g_ref, cq_ref, sq_ref, ck_ref, sk_ref, s0_ref, nrm_ref,
              out_ref, snew_ref, state):
    c = q_ref.shape[0]

    @pl.when(pl.program_id(1) == 0)
    def _():
        state[...] = s0_ref[...]

    ii = lax.broadcasted_iota(jnp.int32, (c, c), 0)
    jj = lax.broadcasted_iota(jnp.int32, (c, c), 1)
    tri = ii >= jj
    dpos = jnp.maximum(ii - jj, 0).astype(F32)
    rpos = lax.broadcasted_iota(jnp.int32, (c, RET_DK), 0).astype(F32)
    cq, sq, ck, sk = cq_ref[...], sq_ref[...], ck_ref[...], sk_ref[...]
    outs = []
    for h in range(RET_HEADS):
        lg = RET_LOG_GAMMA[h]
        decay = jnp.where(tri, jnp.exp(dpos * lg), 0.0)
        e_g = jnp.exp((rpos + 1.0) * lg)
        w_tail = jnp.exp((c - 1.0 - rpos) * lg)
        a_chunk = math.exp(c * lg)
        qh = q_ref[:, h * RET_DK:(h + 1) * RET_DK].astype(F32)
        kh = k_ref[:, h * RET_DK:(h + 1) * RET_DK].astype(F32)
        qr = qh * cq + pltpu.roll(qh, RET_DK // 2, 1) * sq
        kr = kh * ck + pltpu.roll(kh, RET_DK // 2, 1) * sk
        vh = v_ref[:, h * RET_DV:(h + 1) * RET_DV]
        sh = state[h]
        scores = _dot_nt(qr.astype(BF16), kr.astype(BF16)) * decay
        y = _dot(scores.astype(BF16), vh) + _dot((qr * e_g).astype(BF16), sh.astype(BF16))
        state[h] = a_chunk * sh + _dot_tn((kr * w_tail).astype(BF16), vh)
        mu = jnp.mean(y, axis=-1, keepdims=True)
        d = y - mu
        var = jnp.mean(d * d, axis=-1, keepdims=True)
        outs.append(d * lax.rsqrt(var + EPS))
    ro = jnp.concatenate(outs, axis=-1) * nrm_ref[...]
    out_ref[...] = (ro * _silu(g_ref[...].astype(F32))).astype(out_ref.dtype)
    snew_ref[...] = state[...]


def _ret_mixer(proj, s0, rope, p, *, batch, seq, chunk):
    c = min(chunk, seq)
    nt = seq // c
    row = lambda b, t: b * nt + t
    tab = pl.BlockSpec((c, RET_DK), lambda b, t: (t, 0))
    return pl.pallas_call(
        _ret_body,
        grid=(batch, nt),
        in_specs=[
            pl.BlockSpec((c, RET_QK), lambda b, t: (row(b, t), OFF_RQ // RET_QK)),
            pl.BlockSpec((c, RET_QK), lambda b, t: (row(b, t), OFF_RK // RET_QK)),
            pl.BlockSpec((c, RET_V), lambda b, t: (row(b, t), OFF_RV // RET_V)),
            pl.BlockSpec((c, RET_V), lambda b, t: (row(b, t), OFF_RG // RET_V)),
            tab, tab, tab, tab,
            pl.BlockSpec((None, RET_HEADS, RET_DK, RET_DV), lambda b, t: (b, 0, 0, 0)),
            pl.BlockSpec((1, RET_V), lambda b, t: (0, 0)),
        ],
        out_specs=[
            pl.BlockSpec((c, RET_V), lambda b, t: (row(b, t), 0)),
            pl.BlockSpec((None, RET_HEADS, RET_DK, RET_DV), lambda b, t: (b, 0, 0, 0)),
        ],
        out_shape=[
            jax.ShapeDtypeStruct((batch * seq, RET_V), BF16),
            jax.ShapeDtypeStruct((batch, RET_HEADS, RET_DK, RET_DV), F32),
        ],
        scratch_shapes=[pltpu.VMEM((RET_HEADS, RET_DK, RET_DV), F32)],
        compiler_params=_params(("arbitrary", "arbitrary")),
        name="ret_mixer",
    )(proj, proj, proj, proj, rope[0], rope[1], rope[2], rope[3], s0, p["ret_norm"])


def _cumsum_rows(x):
    n = x.shape[0]
    rows = lax.broadcasted_iota(jnp.int32, x.shape, 0)
    s = 1
    while s < n:
        x = x + jnp.where(rows >= s, pltpu.roll(x, s, 0), 0.0)
        s *= 2
    return x


def _ssd_body(z0_ref, z1_ref, x0_ref, x1_ref, b_ref, c_ref, dt_ref, conv0_ref, s0_ref, shift_ref, cw_ref, cb_ref,
              dtb_ref, alog_ref, dexp_ref, nrm_ref, out_ref, snew_ref, ctail_ref, stage, state, xw_scr, u_scr, *,
              n_steps):
    c = x0_ref.shape[0]
    gh, hd, gw = SSD_GROUP_HEADS, SSD_HEADDIM, SSD_GROUP_WIDTH
    half = SSD_INNER // 2

    @pl.when(pl.program_id(1) == 0)
    def _():
        stage[...] = jnp.zeros(stage.shape, stage.dtype)
        stage[0:HIST_ROWS, :] = _history_tile(conv0_ref[...])
        for g in range(SSD_GROUPS):
            state[g] = jnp.concatenate([s0_ref[g * gh + hh] for hh in range(gh)], axis=-1)

    blk = pl.ds(HIST_ROWS, c)
    stage[blk, 0:half] = x0_ref[...]
    stage[blk, half:SSD_INNER] = x1_ref[...]
    stage[blk, SSD_INNER:SSD_INNER + SSD_BC] = b_ref[...]
    stage[blk, SSD_INNER + SSD_BC:SSD_CONV_DIM] = c_ref[...]
    sh = _dot(shift_ref[...], stage[...])
    tail = stage[pl.ds(HIST_ROWS + c - BF16_ROWS, BF16_ROWS), :].astype(F32)[SUBLANES:]
    stage[0:HIST_ROWS, :] = _history_tile(tail)
    ctail_ref[...] = tail

    def conv_tile(x_bf16, j):
        sl = slice(j * LANES, (j + 1) * LANES)
        acc = sh[2 * c:3 * c, sl] * cw_ref[0:1, sl]
        acc = acc + sh[c:2 * c, sl] * cw_ref[1:2, sl]
        acc = acc + sh[0:c, sl] * cw_ref[2:3, sl]
        acc = acc + x_bf16.astype(F32) * cw_ref[3:4, sl]
        return _silu(cb_ref[:, sl] + acc)

    x_tiles = SSD_INNER // LANES
    bc_tiles = SSD_BC // LANES

    dt = _softplus(dt_ref[...] + dtb_ref[...])
    a_neg = -jnp.exp(alog_ref[...])
    g2 = _cumsum_rows(dt * a_neg) * LOG2E
    g2_t = g2.T
    lane_lo = lax.broadcasted_iota(jnp.int32, (c, LANES), 1) < hd
    ii = lax.broadcasted_iota(jnp.int32, (c, c), 0)
    jj = lax.broadcasted_iota(jnp.int32, (c, c), 1)
    tri = ii >= jj

    bgs, cbs, y_inter = [], [], []
    for g in range(SSD_GROUPS):
        sl = slice(g * SSD_STATE, (g + 1) * SSD_STATE)
        bg = conv_tile(b_ref[:, sl], x_tiles + g).astype(BF16)
        cg = conv_tile(c_ref[:, sl], x_tiles + bc_tiles + g).astype(BF16)
        bgs.append(bg)
        cbs.append(jnp.where(tri, _dot_nt(cg, bg), 0.0))
        y_inter.append(_dot(cg, state[g].astype(BF16)))

    ssq = None
    a_tiles = []
    for p in range(SSD_HEADS // 2):
        g, pp = divmod(p, gh // 2)
        sl = slice(p * LANES, (p + 1) * LANES)
        x_ref, off = (x0_ref, p * LANES) if p < x_tiles // 2 else (x1_ref, (p - x_tiles // 2) * LANES)
        z_ref = z0_ref if p < x_tiles // 2 else z1_ref
        xs = conv_tile(x_ref[:, off:off + LANES], p)
        gcol = [jnp.broadcast_to(g2[:, h:h + 1], (c, LANES)) for h in (2 * p, 2 * p + 1)]
        dcol = [jnp.broadcast_to(dt[:, h:h + 1], (c, LANES)) for h in (2 * p, 2 * p + 1)]
        g_x = jnp.where(lane_lo, gcol[0], gcol[1])
        xdt = xs * jnp.where(lane_lo, dcol[0], dcol[1])
        g_last = g_x[c - 1:c, :]
        xw_scr[:, sl] = (xdt * jnp.exp2(g_last - g_x)).astype(BF16)
        a_tiles.append(jnp.exp2(g_last))
        ms = []
        for k, h in enumerate((2 * p, 2 * p + 1)):
            decay = jnp.exp2(jnp.minimum(gcol[k][:, 0:c] - g2_t[h:h + 1, :], 0.0))
            ms.append((cbs[g] * decay).astype(BF16))
        rhs = jnp.concatenate([jnp.where(lane_lo, xdt, 0.0), jnp.where(lane_lo, 0.0, xdt)], axis=0).astype(BF16)
        y = _dot(jnp.concatenate(ms, axis=1), rhs) + jnp.exp2(g_x) * y_inter[g][:, pp * LANES:(pp + 1) * LANES]
        so = y + dexp_ref[:, sl] * xs
        u = so * _silu(z_ref[:, off:off + LANES].astype(F32))
        ssq = u * u if ssq is None else ssq + u * u
        u_scr[:, sl] = u

    for g in range(SSD_GROUPS):
        gsl = slice(g * gw, (g + 1) * gw)
        a_row = jnp.concatenate(a_tiles[g * (gh // 2):(g + 1) * (gh // 2)], axis=-1)
        state[g] = a_row * state[g] + _dot_tn(bgs[g], xw_scr[:, gsl])

    rs = lax.rsqrt(jnp.sum(ssq, axis=-1, keepdims=True) * (1.0 / SSD_INNER) + EPS)
    out_ref[...] = (u_scr[...] * rs * nrm_ref[...]).astype(out_ref.dtype)

    @pl.when(pl.program_id(1) == n_steps - 1)
    def _():
        for g in range(SSD_GROUPS):
            sg = state[g]
            for hh in range(gh):
                snew_ref[g * gh + hh] = sg[:, hh * hd:(hh + 1) * hd]


def _ssd_mixer(proj, dt_raw, conv0, s0, p, *, batch, seq, chunk):
    c = min(chunk, seq)
    nt = seq // c
    row = lambda b, t: b * nt + t
    const2 = lambda b, t: (0, 0)
    half = SSD_INNER // 2
    shift = _shift_matrix(c)
    return pl.pallas_call(
        functools.partial(_ssd_body, n_steps=nt),
        grid=(batch, nt),
        in_specs=[
            pl.BlockSpec((c, half), lambda b, t: (row(b, t), OFF_SZ // half)),
            pl.BlockSpec((c, half), lambda b, t: (row(b, t), OFF_SZ // half + 1)),
            pl.BlockSpec((c, half), lambda b, t: (row(b, t), OFF_SX // half)),
            pl.BlockSpec((c, half), lambda b, t: (row(b, t), OFF_SX // half + 1)),
            pl.BlockSpec((c, SSD_BC), lambda b, t: (row(b, t), OFF_SB // SSD_BC)),
            pl.BlockSpec((c, SSD_BC), lambda b, t: (row(b, t), OFF_SC // SSD_BC)),
            pl.BlockSpec((c, LANES), lambda b, t: (row(b, t), 0)),
            pl.BlockSpec((None, SUBLANES, SSD_CONV_DIM), lambda b, t: (b, 0, 0)),
            pl.BlockSpec((None, SSD_HEADS, SSD_STATE, SSD_HEADDIM), lambda b, t: (b, 0, 0, 0)),
            pl.BlockSpec(shift.shape, const2),
            pl.BlockSpec((CONV_W, SSD_CONV_DIM), const2),
            pl.BlockSpec((1, SSD_CONV_DIM), const2),
            pl.BlockSpec((1, LANES), const2),
            pl.BlockSpec((1, LANES), const2),
            pl.BlockSpec((1, SSD_INNER), const2),
            pl.BlockSpec((1, SSD_INNER), const2),
        ],
        out_specs=[
            pl.BlockSpec((c, SSD_INNER), lambda b, t: (row(b, t), 0)),
            pl.BlockSpec((None, SSD_HEADS, SSD_STATE, SSD_HEADDIM), lambda b, t: (b, 0, 0, 0)),
            pl.BlockSpec((None, SUBLANES, SSD_CONV_DIM), lambda b, t: (b, 0, 0)),
        ],
        out_shape=[
            jax.ShapeDtypeStruct((batch * seq, SSD_INNER), BF16),
            jax.ShapeDtypeStruct((batch, SSD_HEADS, SSD_STATE, SSD_HEADDIM), F32),
            jax.ShapeDtypeStruct((batch, SUBLANES, SSD_CONV_DIM), F32),
        ],
        scratch_shapes=[
            pltpu.VMEM((_stage_rows(c), SSD_CONV_DIM), BF16),
            pltpu.VMEM((SSD_GROUPS, SSD_STATE, SSD_GROUP_WIDTH), F32),
            pltpu.VMEM((c, SSD_INNER), BF16),
            pltpu.VMEM((c, SSD_INNER), F32),
        ],
        compiler_params=_params(("arbitrary", "arbitrary")),
        name="ssd_mixer",
    )(proj, proj, proj, proj, proj, proj, dt_raw, conv0, s0, shift, p["ssd_conv_w"], p["ssd_conv_b"], p["ssd_dt_bias"],
      p["ssd_a_log"], p["ssd_d_exp"], p["ssd_norm"])


def _xattn_body(x_ref, nx_ref, wq_ref, mk_ref, mv_ref, wo_ref, nf_ref, o_ref, *rest, final):
    if final:
        hn_ref, part_ref = rest
    else:
        hn_next_ref, hn_ref, part_ref = rest
    _rmsnorm_rows_to(hn_ref, x_ref, nx_ref, part_ref)
    q = _dot(hn_ref[...], wq_ref[...])
    scale = XATT_HD ** -0.5
    os = []
    for h in range(XATT_HEADS):
        sl = slice(h * XATT_HD, (h + 1) * XATT_HD)
        s = _dot_nt(q[:, sl].astype(BF16), mk_ref[:, sl].astype(BF16)) * scale
        mx = jnp.max(s, axis=-1, keepdims=True)
        pr = jnp.exp(s - mx)
        den = jnp.sum(pr, axis=-1, keepdims=True)
        os.append(_dot(pr.astype(BF16), mv_ref[:, sl].astype(BF16)) / den)
    o = jnp.concatenate(os, axis=-1).astype(BF16)
    o_ref[...] = x_ref[...] + _dot(o, wo_ref[...])
    if final:
        _rmsnorm_rows_to(o_ref, o_ref, nf_ref, part_ref)
    else:
        _rmsnorm_rows_to(hn_next_ref, o_ref, nf_ref, part_ref)


def _cross_attn(x, mem_k, mem_v, nx, wq, wo, nf, *, batch, seq, tm, final):
    d = x.shape[1]
    tm = min(tm, seq)
    nt = seq // tm
    const2 = lambda b, t: (0, 0)
    x_spec = pl.BlockSpec((tm, d), lambda b, t: (b * nt + t, 0))
    out_specs = x_spec if final else [x_spec, x_spec]
    out_shape = jax.ShapeDtypeStruct(x.shape, F32)
    if not final:
        out_shape = [out_shape, jax.ShapeDtypeStruct(x.shape, BF16)]
    return pl.pallas_call(
        functools.partial(_xattn_body, final=final),
        grid=(batch, nt),
        in_specs=[
            x_spec,
            pl.BlockSpec((1, d), const2),
            pl.BlockSpec((d, XATT_DIM), const2),
            pl.BlockSpec((None, N_MEM, XATT_DIM), lambda b, t: (b, 0, 0)),
            pl.BlockSpec((None, N_MEM, XATT_DIM), lambda b, t: (b, 0, 0)),
            pl.BlockSpec((XATT_DIM, d), const2),
            pl.BlockSpec((1, d), const2),
        ],
        out_specs=out_specs,
        out_shape=out_shape,
        scratch_shapes=[pltpu.VMEM((tm, d), BF16), pltpu.VMEM((tm, LANES), F32)],
        compiler_params=_params(("arbitrary", "arbitrary")),
        name="cross_attn",
    )(x, nx.reshape(1, d), wq, mem_k, mem_v, wo, nf.reshape(1, d))


def _pad_state_rows(s):
    return jnp.pad(s.astype(F32), ((0, 0), (SUBLANES - (CONV_W - 1), 0), (0, 0)))


def _rope_tables(pos):
    half = RET_DK // 2
    inv = ROPE_BASE ** (-jnp.arange(half, dtype=F32) / half)
    ang = pos[:, None] * inv[None, :]
    cos, sin = jnp.cos(ang), jnp.sin(ang)
    cq = jnp.concatenate([cos, cos], axis=-1)
    sq = jnp.concatenate([-sin, sin], axis=-1)
    ks = RET_DK ** -0.5
    return cq, sq, cq * ks, sq * ks


def _layer_params(l, norm_mix, w_in, lru_conv_w, lru_conv_b, lru_wa, lru_ba, lru_wx, lru_bx, lru_lambda, lru_norm,
                  ret_norm, ssd_conv_w, ssd_conv_b, ssd_dt_bias, ssd_a_log, ssd_d, ssd_norm, w_out, norm_xattn,
                  norm_mem, w_q_mem, w_k_mem, w_v_mem, w_o_mem):
    w = w_in[l]
    w_main = w[:, :OFF_DT].astype(BF16)
    w_dt = jnp.pad(w[:, OFF_DT:], ((0, 0), (0, LANES - SSD_HEADS))).astype(BF16)
    row = lambda v: v.reshape(1, -1).astype(F32)
    pad_lanes = lambda v: jnp.pad(v.astype(F32), (0, LANES - v.shape[0])).reshape(1, LANES)
    return dict(
        norm_mix=norm_mix[l], w_main=w_main, w_dt=w_dt,
        lru_conv_w=lru_conv_w[l], lru_conv_b=row(lru_conv_b[l]), lru_wa=lru_wa[l].astype(BF16), lru_ba=row(lru_ba[l]),
        lru_wx=lru_wx[l].astype(BF16), lru_bx=row(lru_bx[l]), lru_lambda=row(lru_lambda[l]), lru_norm=row(lru_norm[l]),
        ret_norm=row(ret_norm[l]),
        ssd_conv_w=ssd_conv_w[l], ssd_conv_b=row(ssd_conv_b[l]), ssd_dt_bias=pad_lanes(ssd_dt_bias[l]),
        ssd_a_log=pad_lanes(ssd_a_log[l]), ssd_d_exp=row(jnp.repeat(ssd_d[l], SSD_HEADDIM)), ssd_norm=row(ssd_norm[l]),
        w_out=w_out[l].astype(BF16), norm_xattn=norm_xattn[l], norm_mem=norm_mem[l],
        w_q=w_q_mem[l].astype(BF16), w_kv=jnp.concatenate([w_k_mem[l], w_v_mem[l]], axis=1).astype(BF16),
        w_o=w_o_mem[l].astype(BF16))


def _mixer_sublayer(x, hn, rope, lru_h0, lru_conv0, ret_s0, ssd_s0, ssd_conv0, p, *, batch, seq):
    proj, dt_raw = _in_proj(hn, p["w_main"], p["w_dt"], tm=TM_IN_PROJ, tn=TN_PROJ)
    lru_out, lru_h1, lru_ct = _lru_mixer(proj, _pad_state_rows(lru_conv0), lru_h0.reshape(batch, 1, LRU_WIDTH), p,
                                         batch=batch, seq=seq, tb=CHUNK)
    ret_out, ret_s1 = _ret_mixer(proj, ret_s0, rope, p, batch=batch, seq=seq, chunk=CHUNK)
    ssd_out, ssd_s1, ssd_ct = _ssd_mixer(proj, dt_raw, _pad_state_rows(ssd_conv0), ssd_s0, p,
                                         batch=batch, seq=seq, chunk=CHUNK)
    x1 = _out_proj(x, lru_out, ret_out, ssd_out, p["w_out"], tm=TM_PROJ, tn=TN_PROJ)
    tail = SUBLANES - (CONV_W - 1)
    states = (lru_h1.reshape(batch, LRU_WIDTH), lru_ct[:, tail:, :], ret_s1, ssd_s1, ssd_ct[:, tail:, :])
    return x1, states


def kernel(x_prompt, x_sample, mem_prompt, state_lru_h, state_lru_conv, state_ret, state_ssd, state_ssd_conv, cache_mem_k, cache_mem_v, norm_mix, w_in, lru_conv_w, lru_conv_b, lru_wa, lru_ba, lru_wx, lru_bx, lru_lambda, lru_norm, ret_norm, ssd_conv_w, ssd_conv_b, ssd_dt_bias, ssd_a_log, ssd_d, ssd_norm, w_out, norm_xattn, norm_mem, w_q_mem, w_k_mem, w_v_mem, w_o_mem, norm_final):
    bp, tp, d = x_prompt.shape
    bs, ts, _ = x_sample.shape
    depth = w_in.shape[0]
    assert tp % BF16_ROWS == 0 and ts % BF16_ROWS == 0
    rope_p = _rope_tables(jnp.arange(tp, dtype=F32))
    rope_s = _rope_tables(PAST_LEN + jnp.arange(ts, dtype=F32))
    z_lru_h = jnp.zeros((bp, LRU_WIDTH), F32)
    z_lru_conv = jnp.zeros((bp, CONV_W - 1, LRU_WIDTH), F32)
    z_ret = jnp.zeros((bp, RET_HEADS, RET_DK, RET_DV), F32)
    z_ssd = jnp.zeros((bp, SSD_HEADS, SSD_STATE, SSD_HEADDIM), F32)
    z_ssd_conv = jnp.zeros((bp, CONV_W - 1, SSD_CONV_DIM), F32)

    xp = x_prompt.reshape(bp * tp, d)
    xs = x_sample.reshape(bs * ts, d)
    mem = mem_prompt.reshape(bp * N_MEM, d)
    hn_p = _rmsnorm_bf16(xp, norm_mix[0], tm=TM_XATT)
    hn_s = _rmsnorm_bf16(xs, norm_mix[0], tm=TM_XATT)
    st_p = [[] for _ in range(5)]
    st_s = [[] for _ in range(5)]
    mk_p, mv_p = [], []
    for l in range(depth):
        p = _layer_params(l, norm_mix, w_in, lru_conv_w, lru_conv_b, lru_wa, lru_ba, lru_wx, lru_bx, lru_lambda,
                          lru_norm, ret_norm, ssd_conv_w, ssd_conv_b, ssd_dt_bias, ssd_a_log, ssd_d, ssd_norm, w_out,
                          norm_xattn, norm_mem, w_q_mem, w_k_mem, w_v_mem, w_o_mem)
        final = l == depth - 1
        next_gain = norm_final if final else norm_mix[l + 1]
        xp, st = _mixer_sublayer(xp, hn_p, rope_p, z_lru_h, z_lru_conv, z_ret, z_ssd, z_ssd_conv, p, batch=bp, seq=tp)
        mkv = _norm_matmul(mem, p["norm_mem"], p["w_kv"], out_dtype=F32, tm=TM_PROJ, tn=TN_PROJ)
        mk = mkv[:, :XATT_DIM].reshape(bp, N_MEM, XATT_DIM)
        mv = mkv[:, XATT_DIM:].reshape(bp, N_MEM, XATT_DIM)
        res = _cross_attn(xp, mk, mv, p["norm_xattn"], p["w_q"], p["w_o"], next_gain, batch=bp, seq=tp, tm=TM_XATT,
                          final=final)
        xp, hn_p = (res, None) if final else res
        for acc, v in zip(st_p, st):
            acc.append(v)
        mk_p.append(mk.reshape(bp, N_MEM, XATT_HEADS, XATT_HD))
        mv_p.append(mv.reshape(bp, N_MEM, XATT_HEADS, XATT_HD))
        xs, st = _mixer_sublayer(xs, hn_s, rope_s, state_lru_h[l], state_lru_conv[l], state_ret[l], state_ssd[l],
                                 state_ssd_conv[l], p, batch=bs, seq=ts)
        res = _cross_attn(xs, cache_mem_k[l].reshape(bs, N_MEM, XATT_DIM), cache_mem_v[l].reshape(bs, N_MEM, XATT_DIM),
                          p["norm_xattn"], p["w_q"], p["w_o"], next_gain, batch=bs, seq=ts, tm=TM_XATT, final=final)
        xs, hn_s = (res, None) if final else res
        for acc, v in zip(st_s, st):
            acc.append(v)

    y_prompt = xp.reshape(bp, tp, d)
    y_sample = xs.reshape(bs, ts, d)
    return (y_prompt, y_sample,
            jnp.stack(st_p[0]), jnp.stack(st_p[1]), jnp.stack(st_p[2]), jnp.stack(st_p[3]), jnp.stack(st_p[4]),
            jnp.stack(mk_p), jnp.stack(mv_p),
            jnp.stack(st_s[0]), jnp.stack(st_s[1]), jnp.stack(st_s[2]), jnp.stack(st_s[3]), jnp.stack(st_s[4]))
```

```python
import functools
import math

import numpy as np
import jax
import jax.numpy as jnp
from jax import lax
from jax.experimental import pallas as pl
from jax.experimental.pallas import tpu as pltpu

F32 = jnp.float32
BF16 = jnp.bfloat16

EPS = 1e-6
CONV_W = 4
PAST_LEN = 4096
N_MEM = 256
LRU_WIDTH = 1024
LRU_BLOCKS = 4
LRU_BLOCK = LRU_WIDTH // LRU_BLOCKS
LRU_C = 8.0
RET_HEADS = 4
RET_DK = 128
RET_DV = 256
RET_QK = RET_HEADS * RET_DK
RET_V = RET_HEADS * RET_DV
ROPE_BASE = 10000.0
SSD_INNER = 2048
SSD_HEADDIM = 64
SSD_HEADS = SSD_INNER // SSD_HEADDIM
SSD_STATE = 128
SSD_GROUPS = 4
SSD_GROUP_HEADS = SSD_HEADS // SSD_GROUPS
SSD_GROUP_WIDTH = SSD_GROUP_HEADS * SSD_HEADDIM
SSD_BC = SSD_GROUPS * SSD_STATE
SSD_CONV_DIM = SSD_INNER + 2 * SSD_BC
XATT_HEADS = 4
XATT_HD = 128
XATT_DIM = XATT_HEADS * XATT_HD

SUBLANES = 8
LANES = 128
BF16_ROWS = 2 * SUBLANES
VMEM_LIMIT_BYTES = 56 * 1024 * 1024

OFF_LX, OFF_LG, OFF_RQ, OFF_RK, OFF_RV, OFF_RG, OFF_SZ, OFF_SX = 0, 1024, 2048, 2560, 3072, 4096, 5120, 7168
OFF_SB = OFF_SX + SSD_INNER
OFF_SC = OFF_SB + SSD_BC
OFF_DT = OFF_SX + SSD_CONV_DIM

RET_LOG_GAMMA = tuple(math.log1p(-(2.0 ** (-5.0 - h))) for h in range(RET_HEADS))
LOG2E = math.log2(math.e)

TM_PROJ, TN_PROJ = 512, 1024
TM_IN_PROJ = 1024
TM_OUT, TN_OUT = 1024, 512
CAST_ROWS = 256
TM_XATT = 256
CHUNK_SSD = 128
CHUNK_LRU = 128
CHUNK_RET = 256


def _params(semantics):
    return pltpu.CompilerParams(dimension_semantics=semantics, vmem_limit_bytes=VMEM_LIMIT_BYTES)


def _sigmoid(x):
    return 1.0 / (1.0 + jnp.exp(-x))


def _silu(x):
    return x * _sigmoid(x)


def _softplus(x):
    return jnp.maximum(x, 0.0) + jnp.log1p(jnp.exp(-jnp.abs(x)))


def _dot(a, b):
    return jnp.dot(a, b, preferred_element_type=F32)


def _dot_nt(a, b):
    return lax.dot_general(a, b, (((1,), (1,)), ((), ())), preferred_element_type=F32)


def _dot_tn(a, b):
    return lax.dot_general(a, b, (((0,), (0,)), ((), ())), preferred_element_type=F32)


def _rmsnorm_rows_to(dst_ref, src_ref, gain_ref, part_ref, unroll=False):
    tm, d = src_ref.shape
    rc = BF16_ROWS
    lane_tiles = d // LANES

    def sum_squares(r, carry):
        rows = pl.ds(pl.multiple_of(r * rc, rc), rc)
        acc = None
        for j in range(lane_tiles):
            v = src_ref[rows, j * LANES:(j + 1) * LANES].astype(F32)
            acc = v * v if acc is None else acc + v * v
        part_ref[rows, :] = acc
        return carry

    lax.fori_loop(0, tm // rc, sum_squares, 0, unroll=unroll)
    ms = jnp.sum(part_ref[...], axis=-1, keepdims=True) * (1.0 / d)
    part_ref[...] = jnp.broadcast_to(lax.rsqrt(ms + EPS), part_ref.shape)

    def scale(r, carry):
        rows = pl.ds(pl.multiple_of(r * rc, rc), rc)
        rs = part_ref[rows, :]
        for j in range(lane_tiles):
            sl = slice(j * LANES, (j + 1) * LANES)
            dst_ref[rows, sl] = (src_ref[rows, sl].astype(F32) * rs * gain_ref[:, sl]).astype(dst_ref.dtype)
        return carry

    lax.fori_loop(0, tm // rc, scale, 0, unroll=unroll)


def _cast_body(w_ref, o_ref):
    o_ref[...] = w_ref[...].astype(o_ref.dtype)


def _cast_layer_bf16(w_stack, l):
    _, r, c = w_stack.shape
    rows = min(CAST_ROWS, r)
    assert r % rows == 0
    return pl.pallas_call(
        _cast_body,
        grid=(r // rows,),
        in_specs=[pl.BlockSpec((None, rows, c), lambda i: (l, i, 0))],
        out_specs=pl.BlockSpec((rows, c), lambda i: (i, 0)),
        out_shape=jax.ShapeDtypeStruct((r, c), BF16),
        compiler_params=_params(("arbitrary",)),
        name="cast_bf16",
    )(w_stack)


def _cast_w_in_body(w_ref, o_ref, o2_ref):
    o_ref[...] = w_ref[:, 0:OFF_DT].astype(o_ref.dtype)
    dt_cols = w_ref[:, OFF_DT:OFF_DT + SSD_HEADS]
    pad = jnp.zeros((dt_cols.shape[0], LANES - SSD_HEADS), dt_cols.dtype)
    o2_ref[...] = jnp.concatenate([dt_cols, pad], axis=-1).astype(o2_ref.dtype)


def _cast_w_in(w_in, l):
    _, r, c = w_in.shape
    assert c == OFF_DT + SSD_HEADS
    rows = min(CAST_ROWS, r)
    assert r % rows == 0
    return pl.pallas_call(
        _cast_w_in_body,
        grid=(r // rows,),
        in_specs=[pl.BlockSpec((None, rows, c), lambda i: (l, i, 0))],
        out_specs=[pl.BlockSpec((rows, OFF_DT), lambda i: (i, 0)), pl.BlockSpec((rows, LANES), lambda i: (i, 0))],
        out_shape=[jax.ShapeDtypeStruct((r, OFF_DT), BF16), jax.ShapeDtypeStruct((r, LANES), BF16)],
        compiler_params=_params(("arbitrary",)),
        name="cast_w_in",
    )(w_in)


def _norm_matmul_body(x_ref, g_ref, w_ref, o_ref, hn_ref, part_ref):
    @pl.when(pl.program_id(1) == 0)
    def _():
        _rmsnorm_rows_to(hn_ref, x_ref, g_ref, part_ref)

    o_ref[...] = _dot(hn_ref[...], w_ref[...]).astype(o_ref.dtype)


def _norm_matmul(x, g, w, *, out_dtype, tm, tn):
    m, k = x.shape
    n = w.shape[1]
    tm = min(tm, m)
    tn = min(tn, n)
    assert m % tm == 0 and n % tn == 0
    return pl.pallas_call(
        _norm_matmul_body,
        grid=(m // tm, n // tn),
        in_specs=[
            pl.BlockSpec((tm, k), lambda i, j: (i, 0)),
            pl.BlockSpec((1, k), lambda i, j: (0, 0)),
            pl.BlockSpec((k, tn), lambda i, j: (0, j)),
        ],
        out_specs=pl.BlockSpec((tm, tn), lambda i, j: (i, j)),
        out_shape=jax.ShapeDtypeStruct((m, n), out_dtype),
        scratch_shapes=[pltpu.VMEM((tm, k), BF16), pltpu.VMEM((tm, LANES), F32)],
        compiler_params=_params(("arbitrary", "arbitrary")),
        name="norm_matmul",
    )(x, g.reshape(1, k), w)


def _rmsnorm_body(x_ref, g_ref, o_ref, part_ref):
    _rmsnorm_rows_to(o_ref, x_ref, g_ref, part_ref)


def _rmsnorm_bf16(x, g, *, tm):
    m, k = x.shape
    tm = min(tm, m)
    assert m % tm == 0
    return pl.pallas_call(
        _rmsnorm_body,
        grid=(m // tm,),
        in_specs=[pl.BlockSpec((tm, k), lambda i: (i, 0)), pl.BlockSpec((1, k), lambda i: (0, 0))],
        out_specs=pl.BlockSpec((tm, k), lambda i: (i, 0)),
        out_shape=jax.ShapeDtypeStruct((m, k), BF16),
        scratch_shapes=[pltpu.VMEM((tm, LANES), F32)],
        compiler_params=_params(("arbitrary",)),
        name="rmsnorm_bf16",
    )(x, g.reshape(1, k))


def _in_proj_body(hn_ref, w_ref, w2_ref, o_ref, o2_ref):
    @pl.when(pl.program_id(1) == 0)
    def _():
        o2_ref[...] = _dot(hn_ref[...], w2_ref[...])

    o_ref[...] = _dot(hn_ref[...], w_ref[...]).astype(o_ref.dtype)


def _in_proj(hn, w, w2, *, tm, tn):
    m, k = hn.shape
    n, n2 = w.shape[1], w2.shape[1]
    tm = min(tm, m)
    tn = min(tn, n)
    assert m % tm == 0 and n % tn == 0
    return pl.pallas_call(
        _in_proj_body,
        grid=(m // tm, n // tn),
        in_specs=[
            pl.BlockSpec((tm, k), lambda i, j: (i, 0)),
            pl.BlockSpec((k, tn), lambda i, j: (0, j)),
            pl.BlockSpec((k, n2), lambda i, j: (0, 0)),
        ],
        out_specs=[pl.BlockSpec((tm, tn), lambda i, j: (i, j)), pl.BlockSpec((tm, n2), lambda i, j: (i, 0))],
        out_shape=[jax.ShapeDtypeStruct((m, n), BF16), jax.ShapeDtypeStruct((m, n2), F32)],
        compiler_params=_params(("arbitrary", "arbitrary")),
        name="in_proj",
    )(hn, w, w2)


def _out_proj_body(x_ref, a1_ref, a2_ref, a3_ref, w1_ref, w2_ref, w3_ref, o_ref):
    acc = _dot(a1_ref[...], w1_ref[...])
    acc = acc + _dot(a2_ref[...], w2_ref[...])
    acc = acc + _dot(a3_ref[...], w3_ref[...])
    o_ref[...] = x_ref[...] + acc


def _out_proj(x, lru_out, ret_out, ssd_out, w_out_bf16, *, tm, tn):
    m, n = x.shape
    tm = min(tm, m)
    tn = min(tn, n)
    assert m % tm == 0 and n % tn == 0
    return pl.pallas_call(
        _out_proj_body,
        grid=(m // tm, n // tn),
        in_specs=[
            pl.BlockSpec((tm, tn), lambda i, j: (i, j)),
            pl.BlockSpec((tm, LRU_WIDTH), lambda i, j: (i, 0)),
            pl.BlockSpec((tm, RET_V), lambda i, j: (i, 0)),
            pl.BlockSpec((tm, SSD_INNER), lambda i, j: (i, 0)),
            pl.BlockSpec((LRU_WIDTH, tn), lambda i, j: (0, j)),
            pl.BlockSpec((RET_V, tn), lambda i, j: (1, j)),
            pl.BlockSpec((SSD_INNER, tn), lambda i, j: (1, j)),
        ],
        out_specs=pl.BlockSpec((tm, tn), lambda i, j: (i, j)),
        out_shape=jax.ShapeDtypeStruct((m, n), F32),
        compiler_params=_params(("arbitrary", "arbitrary")),
        name="out_proj",
    )(x, lru_out, ret_out, ssd_out, w_out_bf16, w_out_bf16, w_out_bf16)


HIST_ROWS = BF16_ROWS


def _stage_rows(c):
    return -(-(HIST_ROWS + c) // LANES) * LANES


def _shift_matrix(c):
    s = np.zeros((CONV_W - 1, c, _stage_rows(c)), np.float32)
    for k in range(CONV_W - 1):
        back = k + 1
        for t in range(c):
            if t >= back:
                s[k, t, HIST_ROWS + t - back] = 1.0
            else:
                j = (CONV_W - 1) + t - back
                s[k, t, j] = s[k, t, 3 + j] = s[k, t, SUBLANES + j] = 1.0
    return jnp.asarray(s.reshape((CONV_W - 1) * c, -1), dtype=BF16)


def _history_tile(tail8):
    r = pltpu.roll(tail8, CONV_W - 1, 0)
    hi = r.astype(BF16).astype(F32)
    rest = r - hi
    mid = rest.astype(BF16).astype(F32)
    lo = rest - mid
    rows = lax.broadcasted_iota(jnp.int32, r.shape, 0)
    top = jnp.where(rows < 3, hi, jnp.where(rows < 6, pltpu.roll(mid, CONV_W - 1, 0), 0.0))
    bot = jnp.where(rows < 3, lo, 0.0)
    return jnp.concatenate([top, bot], axis=0).astype(BF16)


def _conv_from_stage(stage, shift_ref, x_f32, c, cw_ref, cb_ref):
    sh = _dot(shift_ref[...], stage[...])
    acc = sh[2 * c:3 * c] * cw_ref[0:1, :]
    acc = acc + sh[c:2 * c] * cw_ref[1:2, :]
    acc = acc + sh[0:c] * cw_ref[2:3, :]
    acc = acc + x_f32 * cw_ref[3:4, :]
    return cb_ref[...] + acc


def _lru_body(lx_ref, lg_ref, conv0_ref, h0_ref, shift_ref, cw_ref, cb_ref, wa_ref, ba_ref, wx_ref, bx_ref, lam_ref,
              nrm_ref, out_ref, hlast_ref, ctail_ref, stage, hcar):
    tb = lx_ref.shape[0]

    @pl.when(pl.program_id(1) == 0)
    def _():
        stage[...] = jnp.zeros(stage.shape, stage.dtype)
        stage[0:HIST_ROWS, :] = _history_tile(conv0_ref[...])
        hcar[...] = h0_ref[...]

    stage[pl.ds(HIST_ROWS, tb), :] = lx_ref[...]
    x_in = lx_ref[...].astype(F32)
    xc = _conv_from_stage(stage, shift_ref, x_in, tb, cw_ref, cb_ref)
    tail = x_in[tb - SUBLANES:tb]
    stage[0:HIST_ROWS, :] = _history_tile(tail)
    ctail_ref[...] = tail

    xb = xc.astype(BF16)
    ra, ri = [], []
    for n in range(LRU_BLOCKS):
        xn = xb[:, n * LRU_BLOCK:(n + 1) * LRU_BLOCK]
        ra.append(_dot(xn, wa_ref[n]))
        ri.append(_dot(xn, wx_ref[n]))
    r = _sigmoid(jnp.concatenate(ra, axis=-1) + ba_ref[...])
    i = _sigmoid(jnp.concatenate(ri, axis=-1) + bx_ref[...])
    log_a = r * ((-LRU_C) * _softplus(-lam_ref[...]))
    a = jnp.exp(log_a)
    th = jnp.tanh(log_a)
    b = jnp.sqrt(-2.0 * th / (1.0 - th)) * (i * xc)

    ng = tb // SUBLANES
    a3 = a.reshape(ng, SUBLANES, LRU_WIDTH)
    b3 = b.reshape(ng, SUBLANES, LRU_WIDTH)
    rows = lax.broadcasted_iota(jnp.int32, a3.shape, 1)
    s = 1
    while s < SUBLANES:
        keep = rows >= s
        a_sh = jnp.where(keep, pltpu.roll(a3, s, 1), 1.0)
        b_sh = jnp.where(keep, pltpu.roll(b3, s, 1), 0.0)
        b3 = a3 * b_sh + b3
        a3 = a3 * a_sh
        s *= 2
    hprev = hcar[...]
    hs = []
    for g in range(ng):
        hg = b3[g] + a3[g] * hprev
        hprev = hg[SUBLANES - 1:SUBLANES]
        hs.append(hg)
    h = jnp.concatenate(hs, axis=0)
    hcar[...] = hprev
    hlast_ref[...] = hprev

    ms = jnp.mean(h * h, axis=-1, keepdims=True)
    out = h * lax.rsqrt(ms + EPS) * nrm_ref[...]
    out_ref[...] = (out * _silu(lg_ref[...].astype(F32))).astype(out_ref.dtype)


def _lru_mixer(proj, conv0, h0, p, *, batch, seq, tb):
    tb = min(tb, seq)
    nt = seq // tb
    w = LRU_WIDTH
    row = lambda b, t: b * nt + t
    const2 = lambda b, t: (0, 0)
    shift = _shift_matrix(tb)
    return pl.pallas_call(
        _lru_body,
        grid=(batch, nt),
        in_specs=[
            pl.BlockSpec((tb, w), lambda b, t: (row(b, t), OFF_LX // w)),
            pl.BlockSpec((tb, w), lambda b, t: (row(b, t), OFF_LG // w)),
            pl.BlockSpec((None, SUBLANES, w), lambda b, t: (b, 0, 0)),
            pl.BlockSpec((None, 1, w), lambda b, t: (b, 0, 0)),
            pl.BlockSpec(shift.shape, const2),
            pl.BlockSpec((CONV_W, w), const2),
            pl.BlockSpec((1, w), const2),
            pl.BlockSpec((LRU_BLOCKS, LRU_BLOCK, LRU_BLOCK), lambda b, t: (0, 0, 0)),
            pl.BlockSpec((1, w), const2),
            pl.BlockSpec((LRU_BLOCKS, LRU_BLOCK, LRU_BLOCK), lambda b, t: (0, 0, 0)),
            pl.BlockSpec((1, w), const2),
            pl.BlockSpec((1, w), const2),
            pl.BlockSpec((1, w), const2),
        ],
        out_specs=[
            pl.BlockSpec((tb, w), lambda b, t: (row(b, t), 0)),
            pl.BlockSpec((None, 1, w), lambda b, t: (b, 0, 0)),
            pl.BlockSpec((None, SUBLANES, w), lambda b, t: (b, 0, 0)),
        ],
        out_shape=[
            jax.ShapeDtypeStruct((batch * seq, w), BF16),
            jax.ShapeDtypeStruct((batch, 1, w), F32),
            jax.ShapeDtypeStruct((batch, SUBLANES, w), F32),
        ],
        scratch_shapes=[pltpu.VMEM((_stage_rows(tb), w), BF16), pltpu.VMEM((1, w), F32)],
        compiler_params=_params(("arbitrary", "arbitrary")),
        name="lru_mixer",
    )(proj, proj, conv0, h0, shift, p["lru_conv_w"], p["lru_conv_b"], p["lru_wa"], p["lru_ba"], p["lru_wx"],
      p["lru_bx"], p["lru_lambda"], p["lru_norm"])


def _ret_body(q_ref, k_ref, v_ref, g_ref, cq_ref, sq_ref, ck_ref, sk_ref, s0_ref, nrm_ref,
              out_ref, snew_ref, state, decay_tab, eg_tab, wt_tab):
    c = q_ref.shape[0]

    @pl.when(jnp.logical_and(pl.program_id(0) == 0, pl.program_id(1) == 0))
    def _():
        ii = lax.broadcasted_iota(jnp.int32, (c, c), 0)
        jj = lax.broadcasted_iota(jnp.int32, (c, c), 1)
        dpos = jnp.maximum(ii - jj, 0).astype(F32)
        rpos = lax.broadcasted_iota(jnp.int32, (c, RET_DK), 0).astype(F32)
        for h in range(RET_HEADS):
            lg = RET_LOG_GAMMA[h]
            decay_tab[h] = jnp.where(ii >= jj, jnp.exp(dpos * lg), 0.0)
            eg_tab[h] = jnp.exp((rpos + 1.0) * lg)
            wt_tab[h] = jnp.exp((c - 1.0 - rpos) * lg)

    @pl.when(pl.program_id(1) == 0)
    def _():
        state[...] = s0_ref[...]

    cq, sq, ck, sk = cq_ref[...], sq_ref[...], ck_ref[...], sk_ref[...]
    outs = []
    for h in range(RET_HEADS):
        decay, e_g, w_tail = decay_tab[h], eg_tab[h], wt_tab[h]
        a_chunk = math.exp(c * RET_LOG_GAMMA[h])
        qh = q_ref[:, h * RET_DK:(h + 1) * RET_DK].astype(F32)
        kh = k_ref[:, h * RET_DK:(h + 1) * RET_DK].astype(F32)
        qr = qh * cq + pltpu.roll(qh, RET_DK // 2, 1) * sq
        kr = kh * ck + pltpu.roll(kh, RET_DK // 2, 1) * sk
        vh = v_ref[:, h * RET_DV:(h + 1) * RET_DV]
        sh = state[h]
        scores = _dot_nt(qr.astype(BF16), kr.astype(BF16)) * decay
        y = _dot(scores.astype(BF16), vh) + _dot((qr * e_g).astype(BF16), sh.astype(BF16))
        state[h] = a_chunk * sh + _dot_tn((kr * w_tail).astype(BF16), vh)
        mu = jnp.mean(y, axis=-1, keepdims=True)
        d = y - mu
        var = jnp.mean(d * d, axis=-1, keepdims=True)
        outs.append(d * lax.rsqrt(var + EPS))
    ro = jnp.concatenate(outs, axis=-1) * nrm_ref[...]
    out_ref[...] = (ro * _silu(g_ref[...].astype(F32))).astype(out_ref.dtype)
    snew_ref[...] = state[...]


def _ret_mixer(proj, s0, rope, p, *, batch, seq, chunk):
    c = min(chunk, seq)
    nt = seq // c
    row = lambda b, t: b * nt + t
    tab = pl.BlockSpec((c, RET_DK), lambda b, t: (t, 0))
    return pl.pallas_call(
        _ret_body,
        grid=(batch, nt),
        in_specs=[
            pl.BlockSpec((c, RET_QK), lambda b, t: (row(b, t), OFF_RQ // RET_QK)),
            pl.BlockSpec((c, RET_QK), lambda b, t: (row(b, t), OFF_RK // RET_QK)),
            pl.BlockSpec((c, RET_V), lambda b, t: (row(b, t), OFF_RV // RET_V)),
            pl.BlockSpec((c, RET_V), lambda b, t: (row(b, t), OFF_RG // RET_V)),
            tab, tab, tab, tab,
            pl.BlockSpec((None, RET_HEADS, RET_DK, RET_DV), lambda b, t: (b, 0, 0, 0)),
            pl.BlockSpec((1, RET_V), lambda b, t: (0, 0)),
        ],
        out_specs=[
            pl.BlockSpec((c, RET_V), lambda b, t: (row(b, t), 0)),
            pl.BlockSpec((None, RET_HEADS, RET_DK, RET_DV), lambda b, t: (b, 0, 0, 0)),
        ],
        out_shape=[
            jax.ShapeDtypeStruct((batch * seq, RET_V), BF16),
            jax.ShapeDtypeStruct((batch, RET_HEADS, RET_DK, RET_DV), F32),
        ],
        scratch_shapes=[
            pltpu.VMEM((RET_HEADS, RET_DK, RET_DV), F32),
            pltpu.VMEM((RET_HEADS, c, c), F32),
            pltpu.VMEM((RET_HEADS, c, RET_DK), F32),
            pltpu.VMEM((RET_HEADS, c, RET_DK), F32),
        ],
        compiler_params=_params(("arbitrary", "arbitrary")),
        name="ret_mixer",
    )(proj, proj, proj, proj, rope[0], rope[1], rope[2], rope[3], s0, p["ret_norm"])


def _cumsum_rows(x):
    n = x.shape[0]
    rows = lax.broadcasted_iota(jnp.int32, x.shape, 0)
    s = 1
    while s < n:
        x = x + jnp.where(rows >= s, pltpu.roll(x, s, 0), 0.0)
        s *= 2
    return x


def _ssd_body(z0_ref, z1_ref, x0_ref, x1_ref, b_ref, c_ref, dt_ref, conv0_ref, s0_ref, shift_ref, cw_ref, cb_ref,
              dtb_ref, alog_ref, dexp_ref, nrm_ref, out_ref, snew_ref, ctail_ref, stage, state, xw_scr, u_scr, *,
              n_steps):
    c = x0_ref.shape[0]
    gh, hd, gw = SSD_GROUP_HEADS, SSD_HEADDIM, SSD_GROUP_WIDTH
    half = SSD_INNER // 2

    @pl.when(pl.program_id(1) == 0)
    def _():
        stage[...] = jnp.zeros(stage.shape, stage.dtype)
        stage[0:HIST_ROWS, :] = _history_tile(conv0_ref[...])
        for g in range(SSD_GROUPS):
            state[g] = jnp.concatenate([s0_ref[g * gh + hh] for hh in range(gh)], axis=-1)

    blk = pl.ds(HIST_ROWS, c)
    stage[blk, 0:half] = x0_ref[...]
    stage[blk, half:SSD_INNER] = x1_ref[...]
    stage[blk, SSD_INNER:SSD_INNER + SSD_BC] = b_ref[...]
    stage[blk, SSD_INNER + SSD_BC:SSD_CONV_DIM] = c_ref[...]
    sh = _dot(shift_ref[...], stage[...])
    tail = stage[pl.ds(HIST_ROWS + c - BF16_ROWS, BF16_ROWS), :].astype(F32)[SUBLANES:]
    stage[0:HIST_ROWS, :] = _history_tile(tail)
    ctail_ref[...] = tail

    def conv_tile(x_bf16, j):
        sl = slice(j * LANES, (j + 1) * LANES)
        acc = sh[2 * c:3 * c, sl] * cw_ref[0:1, sl]
        acc = acc + sh[c:2 * c, sl] * cw_ref[1:2, sl]
        acc = acc + sh[0:c, sl] * cw_ref[2:3, sl]
        acc = acc + x_bf16.astype(F32) * cw_ref[3:4, sl]
        return _silu(cb_ref[:, sl] + acc)

    x_tiles = SSD_INNER // LANES
    bc_tiles = SSD_BC // LANES

    dt = _softplus(dt_ref[...] + dtb_ref[...])
    a_neg = -jnp.exp(alog_ref[...])
    g2 = _cumsum_rows(dt * a_neg) * LOG2E
    g2_t = g2.T
    lane_lo = lax.broadcasted_iota(jnp.int32, (c, LANES), 1) < hd
    ii = lax.broadcasted_iota(jnp.int32, (c, c), 0)
    jj = lax.broadcasted_iota(jnp.int32, (c, c), 1)
    tri = ii >= jj

    bgs, cbs, y_inter = [], [], []
    for g in range(SSD_GROUPS):
        sl = slice(g * SSD_STATE, (g + 1) * SSD_STATE)
        bg = conv_tile(b_ref[:, sl], x_tiles + g).astype(BF16)
        cg = conv_tile(c_ref[:, sl], x_tiles + bc_tiles + g).astype(BF16)
        bgs.append(bg)
        cbs.append(jnp.where(tri, _dot_nt(cg, bg), 0.0))
        y_inter.append(_dot(cg, state[g].astype(BF16)))

    ssq = None
    a_tiles = []
    for p in range(SSD_HEADS // 2):
        g, pp = divmod(p, gh // 2)
        sl = slice(p * LANES, (p + 1) * LANES)
        x_ref, off = (x0_ref, p * LANES) if p < x_tiles // 2 else (x1_ref, (p - x_tiles // 2) * LANES)
        z_ref = z0_ref if p < x_tiles // 2 else z1_ref
        xs = conv_tile(x_ref[:, off:off + LANES], p)
        gcol = [jnp.broadcast_to(g2[:, h:h + 1], (c, LANES)) for h in (2 * p, 2 * p + 1)]
        dcol = [jnp.broadcast_to(dt[:, h:h + 1], (c, LANES)) for h in (2 * p, 2 * p + 1)]
        g_x = jnp.where(lane_lo, gcol[0], gcol[1])
        xdt = xs * jnp.where(lane_lo, dcol[0], dcol[1])
        g_last = g_x[c - 1:c, :]
        xw_scr[:, sl] = (xdt * jnp.exp2(g_last - g_x)).astype(BF16)
        a_tiles.append(jnp.exp2(g_last))
        ms = []
        for k, h in enumerate((2 * p, 2 * p + 1)):
            decay = jnp.exp2(jnp.minimum(gcol[k][:, 0:c] - g2_t[h:h + 1, :], 0.0))
            ms.append((cbs[g] * decay).astype(BF16))
        rhs = jnp.concatenate([jnp.where(lane_lo, xdt, 0.0), jnp.where(lane_lo, 0.0, xdt)], axis=0).astype(BF16)
        y = _dot(jnp.concatenate(ms, axis=1), rhs) + jnp.exp2(g_x) * y_inter[g][:, pp * LANES:(pp + 1) * LANES]
        so = y + dexp_ref[:, sl] * xs
        u = so * _silu(z_ref[:, off:off + LANES].astype(F32))
        ssq = u * u if ssq is None else ssq + u * u
        u_scr[:, sl] = u

    for g in range(SSD_GROUPS):
        gsl = slice(g * gw, (g + 1) * gw)
        a_row = jnp.concatenate(a_tiles[g * (gh // 2):(g + 1) * (gh // 2)], axis=-1)
        state[g] = a_row * state[g] + _dot_tn(bgs[g], xw_scr[:, gsl])

    rs = lax.rsqrt(jnp.sum(ssq, axis=-1, keepdims=True) * (1.0 / SSD_INNER) + EPS)
    out_ref[...] = (u_scr[...] * rs * nrm_ref[...]).astype(out_ref.dtype)

    @pl.when(pl.program_id(1) == n_steps - 1)
    def _():
        for g in range(SSD_GROUPS):
            sg = state[g]
            for hh in range(gh):
                snew_ref[g * gh + hh] = sg[:, hh * hd:(hh + 1) * hd]


def _ssd_mixer(proj, dt_raw, conv0, s0, p, *, batch, seq, chunk):
    c = min(chunk, seq)
    nt = seq // c
    row = lambda b, t: b * nt + t
    const2 = lambda b, t: (0, 0)
    half = SSD_INNER // 2
    shift = _shift_matrix(c)
    return pl.pallas_call(
        functools.partial(_ssd_body, n_steps=nt),
        grid=(batch, nt),
        in_specs=[
            pl.BlockSpec((c, half), lambda b, t: (row(b, t), OFF_SZ // half)),
            pl.BlockSpec((c, half), lambda b, t: (row(b, t), OFF_SZ // half + 1)),
            pl.BlockSpec((c, half), lambda b, t: (row(b, t), OFF_SX // half)),
            pl.BlockSpec((c, half), lambda b, t: (row(b, t), OFF_SX // half + 1)),
            pl.BlockSpec((c, SSD_BC), lambda b, t: (row(b, t), OFF_SB // SSD_BC)),
            pl.BlockSpec((c, SSD_BC), lambda b, t: (row(b, t), OFF_SC // SSD_BC)),
            pl.BlockSpec((c, LANES), lambda b, t: (row(b, t), 0)),
            pl.BlockSpec((None, SUBLANES, SSD_CONV_DIM), lambda b, t: (b, 0, 0)),
            pl.BlockSpec((None, SSD_HEADS, SSD_STATE, SSD_HEADDIM), lambda b, t: (b, 0, 0, 0)),
            pl.BlockSpec(shift.shape, const2),
            pl.BlockSpec((CONV_W, SSD_CONV_DIM), const2),
            pl.BlockSpec((1, SSD_CONV_DIM), const2),
            pl.BlockSpec((1, LANES), const2),
            pl.BlockSpec((1, LANES), const2),
            pl.BlockSpec((1, SSD_INNER), const2),
            pl.BlockSpec((1, SSD_INNER), const2),
        ],
        out_specs=[
            pl.BlockSpec((c, SSD_INNER), lambda b, t: (row(b, t), 0)),
            pl.BlockSpec((None, SSD_HEADS, SSD_STATE, SSD_HEADDIM), lambda b, t: (b, 0, 0, 0)),
            pl.BlockSpec((None, SUBLANES, SSD_CONV_DIM), lambda b, t: (b, 0, 0)),
        ],
        out_shape=[
            jax.ShapeDtypeStruct((batch * seq, SSD_INNER), BF16),
            jax.ShapeDtypeStruct((batch, SSD_HEADS, SSD_STATE, SSD_HEADDIM), F32),
            jax.ShapeDtypeStruct((batch, SUBLANES, SSD_CONV_DIM), F32),
        ],
        scratch_shapes=[
            pltpu.VMEM((_stage_rows(c), SSD_CONV_DIM), BF16),
            pltpu.VMEM((SSD_GROUPS, SSD_STATE, SSD_GROUP_WIDTH), F32),
            pltpu.VMEM((c, SSD_INNER), BF16),
            pltpu.VMEM((c, SSD_INNER), F32),
        ],
        compiler_params=_params(("arbitrary", "arbitrary")),
        name="ssd_mixer",
    )(proj, proj, proj, proj, proj, proj, dt_raw, conv0, s0, shift, p["ssd_conv_w"], p["ssd_conv_b"], p["ssd_dt_bias"],
      p["ssd_a_log"], p["ssd_d_exp"], p["ssd_norm"])


def _xattn_body(x_ref, nx_ref, wq_ref, mk_ref, mv_ref, wo_ref, nf_ref, o_ref, *rest, final):
    if final:
        hn_ref, part_ref = rest
    else:
        hn_next_ref, hn_ref, part_ref = rest
    _rmsnorm_rows_to(hn_ref, x_ref, nx_ref, part_ref)
    q = _dot(hn_ref[...], wq_ref[...])
    scale = XATT_HD ** -0.5
    os = []
    for h in range(XATT_HEADS):
        sl = slice(h * XATT_HD, (h + 1) * XATT_HD)
        s = _dot_nt(q[:, sl].astype(BF16), mk_ref[:, sl].astype(BF16)) * scale
        mx = jnp.max(s, axis=-1, keepdims=True)
        pr = jnp.exp(s - mx)
        den = jnp.sum(pr, axis=-1, keepdims=True)
        os.append(_dot(pr.astype(BF16), mv_ref[:, sl].astype(BF16)) / den)
    o = jnp.concatenate(os, axis=-1).astype(BF16)
    o_ref[...] = x_ref[...] + _dot(o, wo_ref[...])
    if final:
        _rmsnorm_rows_to(o_ref, o_ref, nf_ref, part_ref)
    else:
        _rmsnorm_rows_to(hn_next_ref, o_ref, nf_ref, part_ref)


def _cross_attn(x, mem_k, mem_v, nx, wq, wo, nf, *, batch, seq, tm, final):
    d = x.shape[1]
    tm = min(tm, seq)
    nt = seq // tm
    const2 = lambda b, t: (0, 0)
    x_spec = pl.BlockSpec((tm, d), lambda b, t: (b * nt + t, 0))
    out_specs = x_spec if final else [x_spec, x_spec]
    out_shape = jax.ShapeDtypeStruct(x.shape, F32)
    if not final:
        out_shape = [out_shape, jax.ShapeDtypeStruct(x.shape, BF16)]
    return pl.pallas_call(
        functools.partial(_xattn_body, final=final),
        grid=(batch, nt),
        in_specs=[
            x_spec,
            pl.BlockSpec((1, d), const2),
            pl.BlockSpec((d, XATT_DIM), const2),
            pl.BlockSpec((None, N_MEM, XATT_DIM), lambda b, t: (b, 0, 0)),
            pl.BlockSpec((None, N_MEM, XATT_DIM), lambda b, t: (b, 0, 0)),
            pl.BlockSpec((XATT_DIM, d), const2),
            pl.BlockSpec((1, d), const2),
        ],
        out_specs=out_specs,
        out_shape=out_shape,
        scratch_shapes=[pltpu.VMEM((tm, d), BF16), pltpu.VMEM((tm, LANES), F32)],
        compiler_params=_params(("arbitrary", "arbitrary")),
        name="cross_attn",
    )(x, nx.reshape(1, d), wq, mem_k, mem_v, wo, nf.reshape(1, d))


def _pad_state_rows(s):
    return jnp.pad(s.astype(F32), ((0, 0), (SUBLANES - (CONV_W - 1), 0), (0, 0)))


def _rope_tables(pos):
    half = RET_DK // 2
    inv = ROPE_BASE ** (-jnp.arange(half, dtype=F32) / half)
    ang = pos[:, None] * inv[None, :]
    cos, sin = jnp.cos(ang), jnp.sin(ang)
    cq = jnp.concatenate([cos, cos], axis=-1)
    sq = jnp.concatenate([-sin, sin], axis=-1)
    ks = RET_DK ** -0.5
    return cq, sq, cq * ks, sq * ks


def _layer_params(l, norm_mix, w_in, lru_conv_w, lru_conv_b, lru_wa, lru_ba, lru_wx, lru_bx, lru_lambda, lru_norm,
                  ret_norm, ssd_conv_w, ssd_conv_b, ssd_dt_bias, ssd_a_log, ssd_d, ssd_norm, w_out, norm_xattn,
                  norm_mem, w_q_mem, w_k_mem, w_v_mem, w_o_mem):
    w_main, w_dt = _cast_w_in(w_in, l)
    row = lambda v: v.reshape(1, -1).astype(F32)
    pad_lanes = lambda v: jnp.pad(v.astype(F32), (0, LANES - v.shape[0])).reshape(1, LANES)
    return dict(
        norm_mix=norm_mix[l], w_main=w_main, w_dt=w_dt,
        lru_conv_w=lru_conv_w[l], lru_conv_b=row(lru_conv_b[l]), lru_wa=lru_wa[l].astype(BF16), lru_ba=row(lru_ba[l]),
        lru_wx=lru_wx[l].astype(BF16), lru_bx=row(lru_bx[l]), lru_lambda=row(lru_lambda[l]), lru_norm=row(lru_norm[l]),
        ret_norm=row(ret_norm[l]),
        ssd_conv_w=ssd_conv_w[l], ssd_conv_b=row(ssd_conv_b[l]), ssd_dt_bias=pad_lanes(ssd_dt_bias[l]),
        ssd_a_log=pad_lanes(ssd_a_log[l]), ssd_d_exp=row(jnp.repeat(ssd_d[l], SSD_HEADDIM)), ssd_norm=row(ssd_norm[l]),
        w_out=_cast_layer_bf16(w_out, l), norm_xattn=norm_xattn[l], norm_mem=norm_mem[l],
        w_q=_cast_layer_bf16(w_q_mem, l),
        w_kv=jnp.concatenate([_cast_layer_bf16(w_k_mem, l), _cast_layer_bf16(w_v_mem, l)], axis=1),
        w_o=_cast_layer_bf16(w_o_mem, l))


def _mixer_sublayer(x, hn, rope, lru_h0, lru_conv0, ret_s0, ssd_s0, ssd_conv0, p, *, batch, seq):
    proj, dt_raw = _in_proj(hn, p["w_main"], p["w_dt"], tm=TM_IN_PROJ, tn=TN_PROJ)
    lru_out, lru_h1, lru_ct = _lru_mixer(proj, _pad_state_rows(lru_conv0), lru_h0.reshape(batch, 1, LRU_WIDTH), p,
                                         batch=batch, seq=seq, tb=CHUNK_LRU)
    ret_out, ret_s1 = _ret_mixer(proj, ret_s0, rope, p, batch=batch, seq=seq, chunk=CHUNK_RET)
    ssd_out, ssd_s1, ssd_ct = _ssd_mixer(proj, dt_raw, _pad_state_rows(ssd_conv0), ssd_s0, p,
                                         batch=batch, seq=seq, chunk=CHUNK_SSD)
    x1 = _out_proj(x, lru_out, ret_out, ssd_out, p["w_out"], tm=TM_OUT, tn=TN_OUT)
    tail = SUBLANES - (CONV_W - 1)
    states = (lru_h1.reshape(batch, LRU_WIDTH), lru_ct[:, tail:, :], ret_s1, ssd_s1, ssd_ct[:, tail:, :])
    return x1, states


def kernel(x_prompt, x_sample, mem_prompt, state_lru_h, state_lru_conv, state_ret, state_ssd, state_ssd_conv, cache_mem_k, cache_mem_v, norm_mix, w_in, lru_conv_w, lru_conv_b, lru_wa, lru_ba, lru_wx, lru_bx, lru_lambda, lru_norm, ret_norm, ssd_conv_w, ssd_conv_b, ssd_dt_bias, ssd_a_log, ssd_d, ssd_norm, w_out, norm_xattn, norm_mem, w_q_mem, w_k_mem, w_v_mem, w_o_mem, norm_final):
    bp, tp, d = x_prompt.shape
    bs, ts, _ = x_sample.shape
    depth = w_in.shape[0]
    assert tp % BF16_ROWS == 0 and ts % BF16_ROWS == 0
    rope_p = _rope_tables(jnp.arange(tp, dtype=F32))
    rope_s = _rope_tables(PAST_LEN + jnp.arange(ts, dtype=F32))
    z_lru_h = jnp.zeros((bp, LRU_WIDTH), F32)
    z_lru_conv = jnp.zeros((bp, CONV_W - 1, LRU_WIDTH), F32)
    z_ret = jnp.zeros((bp, RET_HEADS, RET_DK, RET_DV), F32)
    z_ssd = jnp.zeros((bp, SSD_HEADS, SSD_STATE, SSD_HEADDIM), F32)
    z_ssd_conv = jnp.zeros((bp, CONV_W - 1, SSD_CONV_DIM), F32)

    xp = x_prompt.reshape(bp * tp, d)
    xs = x_sample.reshape(bs * ts, d)
    mem = mem_prompt.reshape(bp * N_MEM, d)
    hn_p = _rmsnorm_bf16(xp, norm_mix[0], tm=TM_XATT)
    hn_s = _rmsnorm_bf16(xs, norm_mix[0], tm=TM_XATT)
    st_p = [[] for _ in range(5)]
    st_s = [[] for _ in range(5)]
    mk_p, mv_p = [], []
    for l in range(depth):
        p = _layer_params(l, norm_mix, w_in, lru_conv_w, lru_conv_b, lru_wa, lru_ba, lru_wx, lru_bx, lru_lambda,
                          lru_norm, ret_norm, ssd_conv_w, ssd_conv_b, ssd_dt_bias, ssd_a_log, ssd_d, ssd_norm, w_out,
                          norm_xattn, norm_mem, w_q_mem, w_k_mem, w_v_mem, w_o_mem)
        final = l == depth - 1
        next_gain = norm_final if final else norm_mix[l + 1]
        xp, st = _mixer_sublayer(xp, hn_p, rope_p, z_lru_h, z_lru_conv, z_ret, z_ssd, z_ssd_conv, p, batch=bp, seq=tp)
        mkv = _norm_matmul(mem, p["norm_mem"], p["w_kv"], out_dtype=F32, tm=TM_PROJ, tn=TN_PROJ)
        mk = mkv[:, :XATT_DIM].reshape(bp, N_MEM, XATT_DIM)
        mv = mkv[:, XATT_DIM:].reshape(bp, N_MEM, XATT_DIM)
        res = _cross_attn(xp, mk, mv, p["norm_xattn"], p["w_q"], p["w_o"], next_gain, batch=bp, seq=tp, tm=TM_XATT,
                          final=final)
        xp, hn_p = (res, None) if final else res
        for acc, v in zip(st_p, st):
            acc.append(v)
        mk_p.append(mk.reshape(bp, N_MEM, XATT_HEADS, XATT_HD))
        mv_p.append(mv.reshape(bp, N_MEM, XATT_HEADS, XATT_HD))
        xs, st = _mixer_sublayer(xs, hn_s, rope_s, state_lru_h[l], state_lru_conv[l], state_ret[l], state_ssd[l],
                                 state_ssd_conv[l], p, batch=bs, seq=ts)
        res = _cross_attn(xs, cache_mem_k[l].reshape(bs, N_MEM, XATT_DIM), cache_mem_v[l].reshape(bs, N_MEM, XATT_DIM),
                          p["norm_xattn"], p["w_q"], p["w_o"], next_gain, batch=bs, seq=ts, tm=TM_XATT, final=final)
        xs, hn_s = (res, None) if final else res
        for acc, v in zip(st_s, st):
            acc.append(v)

    y_prompt = xp.reshape(bp, tp, d)
    y_sample = xs.reshape(bs, ts, d)
    return (y_prompt, y_sample,
            jnp.stack(st_p[0]), jnp.stack(st_p[1]), jnp.stack(st_p[2]), jnp.stack(st_p[3]), jnp.stack(st_p[4]),
            jnp.stack(mk_p), jnp.stack(mv_p),
            jnp.stack(st_s[0]), jnp.stack(st_s[1]), jnp.stack(st_s[2]), jnp.stack(st_s[3]), jnp.stack(st_s[4]))
```

```python
import functools
import math
from typing import Callable, NamedTuple

import numpy as np
import jax
import jax.numpy as jnp
from jax import lax
from jax.experimental import pallas as pl
from jax.experimental.pallas import tpu as pltpu

F32 = jnp.float32
BF16 = jnp.bfloat16

EPS = 1e-6
CONV_W = 4
PAST_LEN = 4096
N_MEM = 256
LRU_WIDTH = 1024
LRU_BLOCKS = 4
LRU_BLOCK = LRU_WIDTH // LRU_BLOCKS
LRU_C = 8.0
RET_HEADS = 4
RET_DK = 128
RET_DV = 256
RET_QK = RET_HEADS * RET_DK
RET_V = RET_HEADS * RET_DV
ROPE_BASE = 10000.0
SSD_INNER = 2048
SSD_HEADDIM = 64
SSD_HEADS = SSD_INNER // SSD_HEADDIM
SSD_STATE = 128
SSD_GROUPS = 4
SSD_GROUP_HEADS = SSD_HEADS // SSD_GROUPS
SSD_GROUP_WIDTH = SSD_GROUP_HEADS * SSD_HEADDIM
SSD_BC = SSD_GROUPS * SSD_STATE
SSD_CONV_DIM = SSD_INNER + 2 * SSD_BC
XATT_HEADS = 4
XATT_HD = 128
XATT_DIM = XATT_HEADS * XATT_HD

SUBLANES = 8
LANES = 128
BF16_ROWS = 2 * SUBLANES
VMEM_LIMIT_BYTES = 56 * 1024 * 1024

OFF_LX, OFF_LG, OFF_RQ, OFF_RK, OFF_RV, OFF_RG, OFF_SZ, OFF_SX = 0, 1024, 2048, 2560, 3072, 4096, 5120, 7168
OFF_SB = OFF_SX + SSD_INNER
OFF_SC = OFF_SB + SSD_BC
OFF_DT = OFF_SX + SSD_CONV_DIM

RET_LOG_GAMMA = tuple(math.log1p(-(2.0 ** (-5.0 - h))) for h in range(RET_HEADS))
LOG2E = math.log2(math.e)

TM_PROJ, TN_PROJ = 512, 1024
TM_IN_PROJ = 1024
TM_OUT, TN_OUT = 1024, 512
CAST_ROWS = 256
TM_XATT = 256
CHUNK = 128


def _params(semantics):
    return pltpu.CompilerParams(dimension_semantics=semantics, vmem_limit_bytes=VMEM_LIMIT_BYTES)


def _sigmoid(x):
    return 1.0 / (1.0 + jnp.exp(-x))


def _silu(x):
    return x * _sigmoid(x)


def _softplus(x):
    return jnp.maximum(x, 0.0) + jnp.log1p(jnp.exp(-jnp.abs(x)))


def _dot(a, b):
    return jnp.dot(a, b, preferred_element_type=F32)


def _dot_nt(a, b):
    return lax.dot_general(a, b, (((1,), (1,)), ((), ())), preferred_element_type=F32)


def _dot_tn(a, b):
    return lax.dot_general(a, b, (((0,), (0,)), ((), ())), preferred_element_type=F32)


def _rmsnorm_rows_to(dst_ref, src_ref, gain_ref, part_ref, unroll=False):
    tm, d = src_ref.shape
    rc = BF16_ROWS
    lane_tiles = d // LANES

    def sum_squares(r, carry):
        rows = pl.ds(pl.multiple_of(r * rc, rc), rc)
        accs = [None] * min(4, lane_tiles)
        for j in range(lane_tiles):
            v = src_ref[rows, j * LANES:(j + 1) * LANES].astype(F32)
            k = j % len(accs)
            accs[k] = v * v if accs[k] is None else accs[k] + v * v
        while len(accs) > 1:
            accs = [a + b for a, b in zip(accs[0::2], accs[1::2])]
        part_ref[rows, :] = accs[0]
        return carry

    lax.fori_loop(0, tm // rc, sum_squares, 0, unroll=unroll)
    ms = jnp.sum(part_ref[...], axis=-1, keepdims=True) * (1.0 / d)
    part_ref[...] = jnp.broadcast_to(lax.rsqrt(ms + EPS), part_ref.shape)

    def scale(r, carry):
        rows = pl.ds(pl.multiple_of(r * rc, rc), rc)
        rs = part_ref[rows, :]
        for j in range(lane_tiles):
            sl = slice(j * LANES, (j + 1) * LANES)
            dst_ref[rows, sl] = (src_ref[rows, sl].astype(F32) * rs * gain_ref[:, sl]).astype(dst_ref.dtype)
        return carry

    lax.fori_loop(0, tm // rc, scale, 0, unroll=unroll)


def _cast_body(w_ref, o_ref):
    o_ref[...] = w_ref[...].astype(o_ref.dtype)


def _cast_layer_bf16(w_stack, l):
    _, r, c = w_stack.shape
    rows = min(CAST_ROWS, r)
    assert r % rows == 0
    return pl.pallas_call(
        _cast_body,
        grid=(r // rows,),
        in_specs=[pl.BlockSpec((None, rows, c), lambda i: (l, i, 0))],
        out_specs=pl.BlockSpec((rows, c), lambda i: (i, 0)),
        out_shape=jax.ShapeDtypeStruct((r, c), BF16),
        compiler_params=_params(("arbitrary",)),
        name="cast_bf16",
    )(w_stack)


def _norm_matmul_body(x_ref, g_ref, w_ref, o_ref, hn_ref, part_ref):
    @pl.when(pl.program_id(1) == 0)
    def _():
        _rmsnorm_rows_to(hn_ref, x_ref, g_ref, part_ref)

    o_ref[...] = _dot(hn_ref[...], w_ref[...]).astype(o_ref.dtype)


def _norm_matmul(x, g, w, *, out_dtype, tm, tn):
    m, k = x.shape
    n = w.shape[1]
    tm = min(tm, m)
    tn = min(tn, n)
    assert m % tm == 0 and n % tn == 0
    return pl.pallas_call(
        _norm_matmul_body,
        grid=(m // tm, n // tn),
        in_specs=[
            pl.BlockSpec((tm, k), lambda i, j: (i, 0)),
            pl.BlockSpec((1, k), lambda i, j: (0, 0)),
            pl.BlockSpec((k, tn), lambda i, j: (0, j)),
        ],
        out_specs=pl.BlockSpec((tm, tn), lambda i, j: (i, j)),
        out_shape=jax.ShapeDtypeStruct((m, n), out_dtype),
        scratch_shapes=[pltpu.VMEM((tm, k), BF16), pltpu.VMEM((tm, LANES), F32)],
        compiler_params=_params(("arbitrary", "arbitrary")),
        name="norm_matmul",
    )(x, g.reshape(1, k), w)


def _rmsnorm_body(x_ref, g_ref, o_ref, part_ref):
    _rmsnorm_rows_to(o_ref, x_ref, g_ref, part_ref)


def _rmsnorm_bf16(x, g, *, tm):
    m, k = x.shape
    tm = min(tm, m)
    assert m % tm == 0
    return pl.pallas_call(
        _rmsnorm_body,
        grid=(m // tm,),
        in_specs=[pl.BlockSpec((tm, k), lambda i: (i, 0)), pl.BlockSpec((1, k), lambda i: (0, 0))],
        out_specs=pl.BlockSpec((tm, k), lambda i: (i, 0)),
        out_shape=jax.ShapeDtypeStruct((m, k), BF16),
        scratch_shapes=[pltpu.VMEM((tm, LANES), F32)],
        compiler_params=_params(("arbitrary",)),
        name="rmsnorm_bf16",
    )(x, g.reshape(1, k))


def _in_proj_body(hn_ref, w_ref, w2_ref, o_ref, o2_ref):
    @pl.when(pl.program_id(1) == 0)
    def _():
        o2_ref[...] = _dot(hn_ref[...], w2_ref[...])

    o_ref[...] = _dot(hn_ref[...], w_ref[...]).astype(o_ref.dtype)


def _in_proj(hn, w, w2, *, tm, tn):
    m, k = hn.shape
    n, n2 = w.shape[1], w2.shape[1]
    tm = min(tm, m)
    tn = min(tn, n)
    assert m % tm == 0 and n % tn == 0
    return pl.pallas_call(
        _in_proj_body,
        grid=(m // tm, n // tn),
        in_specs=[
            pl.BlockSpec((tm, k), lambda i, j: (i, 0)),
            pl.BlockSpec((k, tn), lambda i, j: (0, j)),
            pl.BlockSpec((k, n2), lambda i, j: (0, 0)),
        ],
        out_specs=[pl.BlockSpec((tm, tn), lambda i, j: (i, j)), pl.BlockSpec((tm, n2), lambda i, j: (i, 0))],
        out_shape=[jax.ShapeDtypeStruct((m, n), BF16), jax.ShapeDtypeStruct((m, n2), F32)],
        compiler_params=_params(("arbitrary", "arbitrary")),
        name="in_proj",
    )(hn, w, w2)


def _out_proj_body(x_ref, a1_ref, a2_ref, a3_ref, w1_ref, w2_ref, w3_ref, o_ref):
    acc = _dot(a1_ref[...], w1_ref[...])
    acc = acc + _dot(a2_ref[...], w2_ref[...])
    acc = acc + _dot(a3_ref[...], w3_ref[...])
    o_ref[...] = x_ref[...] + acc


def _out_proj(x, lru_out, ret_out, ssd_out, w_out_bf16, *, tm, tn):
    m, n = x.shape
    tm = min(tm, m)
    tn = min(tn, n)
    assert m % tm == 0 and n % tn == 0
    return pl.pallas_call(
        _out_proj_body,
        grid=(m // tm, n // tn),
        in_specs=[
            pl.BlockSpec((tm, tn), lambda i, j: (i, j)),
            pl.BlockSpec((tm, LRU_WIDTH), lambda i, j: (i, 0)),
            pl.BlockSpec((tm, RET_V), lambda i, j: (i, 0)),
            pl.BlockSpec((tm, SSD_INNER), lambda i, j: (i, 0)),
            pl.BlockSpec((LRU_WIDTH, tn), lambda i, j: (0, j)),
            pl.BlockSpec((RET_V, tn), lambda i, j: (1, j)),
            pl.BlockSpec((SSD_INNER, tn), lambda i, j: (1, j)),
        ],
        out_specs=pl.BlockSpec((tm, tn), lambda i, j: (i, j)),
        out_shape=jax.ShapeDtypeStruct((m, n), F32),
        compiler_params=_params(("arbitrary", "arbitrary")),
        name="out_proj",
    )(x, lru_out, ret_out, ssd_out, w_out_bf16, w_out_bf16, w_out_bf16)


HIST_ROWS = BF16_ROWS


def _stage_rows(c):
    return -(-(HIST_ROWS + c) // LANES) * LANES


def _shift_matrix(c):
    s = np.zeros((CONV_W - 1, c, _stage_rows(c)), np.float32)
    for k in range(CONV_W - 1):
        back = k + 1
        for t in range(c):
            if t >= back:
                s[k, t, HIST_ROWS + t - back] = 1.0
            else:
                j = (CONV_W - 1) + t - back
                s[k, t, j] = s[k, t, 3 + j] = s[k, t, SUBLANES + j] = 1.0
    return jnp.asarray(s.reshape((CONV_W - 1) * c, -1), dtype=BF16)


def _history_tile(tail8):
    r = pltpu.roll(tail8, CONV_W - 1, 0)
    hi = r.astype(BF16).astype(F32)
    rest = r - hi
    mid = rest.astype(BF16).astype(F32)
    lo = rest - mid
    rows = lax.broadcasted_iota(jnp.int32, r.shape, 0)
    top = jnp.where(rows < 3, hi, jnp.where(rows < 6, pltpu.roll(mid, CONV_W - 1, 0), 0.0))
    bot = jnp.where(rows < 3, lo, 0.0)
    return jnp.concatenate([top, bot], axis=0).astype(BF16)


def _conv_from_stage(stage, shift_ref, x_f32, c, cw_ref, cb_ref):
    sh = _dot(shift_ref[...], stage[...])
    acc = sh[2 * c:3 * c] * cw_ref[0:1, :]
    acc = acc + sh[c:2 * c] * cw_ref[1:2, :]
    acc = acc + sh[0:c] * cw_ref[2:3, :]
    acc = acc + x_f32 * cw_ref[3:4, :]
    return cb_ref[...] + acc


class MixerParts(NamedTuple):
    body: Callable
    args: tuple
    in_specs: list
    out_specs: list
    out_shape: list
    scratch: list


def _mixers_body(*refs, parts):
    n_in = sum(len(p.args) for p in parts)
    n_out = sum(len(p.out_specs) for p in parts)
    ins, outs, scr = refs[:n_in], refs[n_in:n_in + n_out], refs[n_in + n_out:]
    i = o = s = 0
    for p in parts:
        ni, no, ns = len(p.args), len(p.out_specs), len(p.scratch)
        p.body(*ins[i:i + ni], *outs[o:o + no], *scr[s:s + ns])
        i, o, s = i + ni, o + no, s + ns


def _run_mixers(parts, *, batch, nt):
    flat = lambda field: [x for p in parts for x in getattr(p, field)]
    bodies = tuple(p._replace(args=tuple(None for _ in p.args), out_shape=[]) for p in parts)
    outs = pl.pallas_call(
        functools.partial(_mixers_body, parts=bodies),
        grid=(batch, nt),
        in_specs=flat("in_specs"),
        out_specs=flat("out_specs"),
        out_shape=flat("out_shape"),
        scratch_shapes=flat("scratch"),
        compiler_params=_params(("arbitrary", "arbitrary")),
        name="mixers",
    )(*flat("args"))
    res, o = [], 0
    for p in parts:
        res.append(outs[o:o + len(p.out_specs)])
        o += len(p.out_specs)
    return res


def _lru_body(lx_ref, lg_ref, conv0_ref, h0_ref, shift_ref, cw_ref, cb_ref, wa_ref, ba_ref, wx_ref, bx_ref, lam_ref,
              nrm_ref, out_ref, hlast_ref, ctail_ref, stage, hcar):
    tb = lx_ref.shape[0]

    @pl.when(pl.program_id(1) == 0)
    def _():
        stage[...] = jnp.zeros(stage.shape, stage.dtype)
        stage[0:HIST_ROWS, :] = _history_tile(conv0_ref[...])
        hcar[...] = h0_ref[...]

    stage[pl.ds(HIST_ROWS, tb), :] = lx_ref[...]
    x_in = lx_ref[...].astype(F32)
    xc = _conv_from_stage(stage, shift_ref, x_in, tb, cw_ref, cb_ref)
    tail = x_in[tb - SUBLANES:tb]
    stage[0:HIST_ROWS, :] = _history_tile(tail)
    ctail_ref[...] = tail

    xb = xc.astype(BF16)
    ra, ri = [], []
    for n in range(LRU_BLOCKS):
        xn = xb[:, n * LRU_BLOCK:(n + 1) * LRU_BLOCK]
        ra.append(_dot(xn, wa_ref[n]))
        ri.append(_dot(xn, wx_ref[n]))
    r = _sigmoid(jnp.concatenate(ra, axis=-1) + ba_ref[...])
    i = _sigmoid(jnp.concatenate(ri, axis=-1) + bx_ref[...])
    log_a = r * ((-LRU_C) * _softplus(-lam_ref[...]))
    a = jnp.exp(log_a)
    th = jnp.tanh(log_a)
    b = jnp.sqrt(-2.0 * th / (1.0 - th)) * (i * xc)

    ng = tb // SUBLANES
    a3 = a.reshape(ng, SUBLANES, LRU_WIDTH)
    b3 = b.reshape(ng, SUBLANES, LRU_WIDTH)
    rows = lax.broadcasted_iota(jnp.int32, a3.shape, 1)
    s = 1
    while s < SUBLANES:
        keep = rows >= s
        a_sh = jnp.where(keep, pltpu.roll(a3, s, 1), 1.0)
        b_sh = jnp.where(keep, pltpu.roll(b3, s, 1), 0.0)
        b3 = a3 * b_sh + b3
        a3 = a3 * a_sh
        s *= 2
    hprev = hcar[...]
    hs = []
    for g in range(ng):
        hg = b3[g] + a3[g] * hprev
        hprev = hg[SUBLANES - 1:SUBLANES]
        hs.append(hg)
    h = jnp.concatenate(hs, axis=0)
    hcar[...] = hprev
    hlast_ref[...] = hprev

    ms = jnp.mean(h * h, axis=-1, keepdims=True)
    out = h * lax.rsqrt(ms + EPS) * nrm_ref[...]
    out_ref[...] = (out * _silu(lg_ref[...].astype(F32))).astype(out_ref.dtype)


def _lru_parts(proj, conv0, h0, p, *, batch, seq, tb):
    assert seq % tb == 0
    nt = seq // tb
    w = LRU_WIDTH
    row = lambda b, t: b * nt + t
    const2 = lambda b, t: (0, 0)
    shift = _shift_matrix(tb)
    return MixerParts(
        body=_lru_body,
        in_specs=[
            pl.BlockSpec((tb, w), lambda b, t: (row(b, t), OFF_LX // w)),
            pl.BlockSpec((tb, w), lambda b, t: (row(b, t), OFF_LG // w)),
            pl.BlockSpec((None, SUBLANES, w), lambda b, t: (b, 0, 0)),
            pl.BlockSpec((None, 1, w), lambda b, t: (b, 0, 0)),
            pl.BlockSpec(shift.shape, const2),
            pl.BlockSpec((CONV_W, w), const2),
            pl.BlockSpec((1, w), const2),
            pl.BlockSpec((LRU_BLOCKS, LRU_BLOCK, LRU_BLOCK), lambda b, t: (0, 0, 0)),
            pl.BlockSpec((1, w), const2),
            pl.BlockSpec((LRU_BLOCKS, LRU_BLOCK, LRU_BLOCK), lambda b, t: (0, 0, 0)),
            pl.BlockSpec((1, w), const2),
            pl.BlockSpec((1, w), const2),
            pl.BlockSpec((1, w), const2),
        ],
        out_specs=[
            pl.BlockSpec((tb, w), lambda b, t: (row(b, t), 0)),
            pl.BlockSpec((None, 1, w), lambda b, t: (b, 0, 0)),
            pl.BlockSpec((None, SUBLANES, w), lambda b, t: (b, 0, 0)),
        ],
        out_shape=[
            jax.ShapeDtypeStruct((batch * seq, w), BF16),
            jax.ShapeDtypeStruct((batch, 1, w), F32),
            jax.ShapeDtypeStruct((batch, SUBLANES, w), F32),
        ],
        scratch=[pltpu.VMEM((_stage_rows(tb), w), BF16), pltpu.VMEM((1, w), F32)],
        args=(proj, proj, conv0, h0, shift, p["lru_conv_w"], p["lru_conv_b"], p["lru_wa"], p["lru_ba"], p["lru_wx"],
              p["lru_bx"], p["lru_lambda"], p["lru_norm"]))


def _ret_body(q_ref, k_ref, v_ref, g_ref, cq_ref, sq_ref, ck_ref, sk_ref, s0_ref, nrm_ref,
              out_ref, snew_ref, state, decay_tab, eg_tab, wt_tab):
    c = q_ref.shape[0]

    @pl.when(jnp.logical_and(pl.program_id(0) == 0, pl.program_id(1) == 0))
    def _():
        ii = lax.broadcasted_iota(jnp.int32, (c, c), 0)
        jj = lax.broadcasted_iota(jnp.int32, (c, c), 1)
        dpos = jnp.maximum(ii - jj, 0).astype(F32)
        rpos = lax.broadcasted_iota(jnp.int32, (c, RET_DK), 0).astype(F32)
        for h in range(RET_HEADS):
            lg = RET_LOG_GAMMA[h]
            decay_tab[h] = jnp.where(ii >= jj, jnp.exp(dpos * lg), 0.0)
            eg_tab[h] = jnp.exp((rpos + 1.0) * lg)
            wt_tab[h] = jnp.exp((c - 1.0 - rpos) * lg)

    @pl.when(pl.program_id(1) == 0)
    def _():
        state[...] = s0_ref[...]

    cq, sq, ck, sk = cq_ref[...], sq_ref[...], ck_ref[...], sk_ref[...]
    outs = []
    for h in range(RET_HEADS):
        decay, e_g, w_tail = decay_tab[h], eg_tab[h], wt_tab[h]
        a_chunk = math.exp(c * RET_LOG_GAMMA[h])
        qh = q_ref[:, h * RET_DK:(h + 1) * RET_DK].astype(F32)
        kh = k_ref[:, h * RET_DK:(h + 1) * RET_DK].astype(F32)
        qr = qh * cq + pltpu.roll(qh, RET_DK // 2, 1) * sq
        kr = kh * ck + pltpu.roll(kh, RET_DK // 2, 1) * sk
        vh = v_ref[:, h * RET_DV:(h + 1) * RET_DV]
        sh = state[h]
        scores = _dot_nt(qr.astype(BF16), kr.astype(BF16)) * decay
        y = _dot(scores.astype(BF16), vh) + _dot((qr * e_g).astype(BF16), sh.astype(BF16))
        state[h] = a_chunk * sh + _dot_tn((kr * w_tail).astype(BF16), vh)
        mu = jnp.mean(y, axis=-1, keepdims=True)
        d = y - mu
        var = jnp.mean(d * d, axis=-1, keepdims=True)
        outs.append(d * lax.rsqrt(var + EPS))
    ro = jnp.concatenate(outs, axis=-1) * nrm_ref[...]
    out_ref[...] = (ro * _silu(g_ref[...].astype(F32))).astype(out_ref.dtype)
    snew_ref[...] = state[...]


def _ret_parts(proj, s0, rope, p, *, batch, seq, c):
    assert seq % c == 0
    nt = seq // c
    row = lambda b, t: b * nt + t
    tab = pl.BlockSpec((c, RET_DK), lambda b, t: (t, 0))
    return MixerParts(
        body=_ret_body,
        in_specs=[
            pl.BlockSpec((c, RET_QK), lambda b, t: (row(b, t), OFF_RQ // RET_QK)),
            pl.BlockSpec((c, RET_QK), lambda b, t: (row(b, t), OFF_RK // RET_QK)),
            pl.BlockSpec((c, RET_V), lambda b, t: (row(b, t), OFF_RV // RET_V)),
            pl.BlockSpec((c, RET_V), lambda b, t: (row(b, t), OFF_RG // RET_V)),
            tab, tab, tab, tab,
            pl.BlockSpec((None, RET_HEADS, RET_DK, RET_DV), lambda b, t: (b, 0, 0, 0)),
            pl.BlockSpec((1, RET_V), lambda b, t: (0, 0)),
        ],
        out_specs=[
            pl.BlockSpec((c, RET_V), lambda b, t: (row(b, t), 0)),
            pl.BlockSpec((None, RET_HEADS, RET_DK, RET_DV), lambda b, t: (b, 0, 0, 0)),
        ],
        out_shape=[
            jax.ShapeDtypeStruct((batch * seq, RET_V), BF16),
            jax.ShapeDtypeStruct((batch, RET_HEADS, RET_DK, RET_DV), F32),
        ],
        scratch=[
            pltpu.VMEM((RET_HEADS, RET_DK, RET_DV), F32),
            pltpu.VMEM((RET_HEADS, c, c), F32),
            pltpu.VMEM((RET_HEADS, c, RET_DK), F32),
            pltpu.VMEM((RET_HEADS, c, RET_DK), F32),
        ],
        args=(proj, proj, proj, proj, rope[0], rope[1], rope[2], rope[3], s0, p["ret_norm"]))


def _cumsum_rows(x):
    n = x.shape[0]
    rows = lax.broadcasted_iota(jnp.int32, x.shape, 0)
    s = 1
    while s < n:
        x = x + jnp.where(rows >= s, pltpu.roll(x, s, 0), 0.0)
        s *= 2
    return x


def _ssd_body(z0_ref, z1_ref, x0_ref, x1_ref, b_ref, c_ref, dt_ref, conv0_ref, s0_ref, shift_ref, cw_ref, cb_ref,
              dtb_ref, alog_ref, dexp_ref, nrm_ref, out_ref, snew_ref, ctail_ref, stage, state, xw_scr, u_scr, *,
              n_steps):
    c = x0_ref.shape[0]
    gh, hd, gw = SSD_GROUP_HEADS, SSD_HEADDIM, SSD_GROUP_WIDTH
    half = SSD_INNER // 2

    @pl.when(pl.program_id(1) == 0)
    def _():
        stage[...] = jnp.zeros(stage.shape, stage.dtype)
        stage[0:HIST_ROWS, :] = _history_tile(conv0_ref[...])
        for g in range(SSD_GROUPS):
            state[g] = jnp.concatenate([s0_ref[g * gh + hh] for hh in range(gh)], axis=-1)

    blk = pl.ds(HIST_ROWS, c)
    stage[blk, 0:half] = x0_ref[...]
    stage[blk, half:SSD_INNER] = x1_ref[...]
    stage[blk, SSD_INNER:SSD_INNER + SSD_BC] = b_ref[...]
    stage[blk, SSD_INNER + SSD_BC:SSD_CONV_DIM] = c_ref[...]
    sh = _dot(shift_ref[...], stage[...])
    tail = stage[pl.ds(HIST_ROWS + c - BF16_ROWS, BF16_ROWS), :].astype(F32)[SUBLANES:]
    stage[0:HIST_ROWS, :] = _history_tile(tail)
    ctail_ref[...] = tail

    def conv_tile(x_bf16, j):
        sl = slice(j * LANES, (j + 1) * LANES)
        acc = sh[2 * c:3 * c, sl] * cw_ref[0:1, sl]
        acc = acc + sh[c:2 * c, sl] * cw_ref[1:2, sl]
        acc = acc + sh[0:c, sl] * cw_ref[2:3, sl]
        acc = acc + x_bf16.astype(F32) * cw_ref[3:4, sl]
        return _silu(cb_ref[:, sl] + acc)

    x_tiles = SSD_INNER // LANES
    bc_tiles = SSD_BC // LANES

    dt = _softplus(dt_ref[...] + dtb_ref[...])
    a_neg = -jnp.exp(alog_ref[...])
    g2 = _cumsum_rows(dt * a_neg) * LOG2E
    g2_t = g2.T
    lane_lo = lax.broadcasted_iota(jnp.int32, (c, LANES), 1) < hd
    ii = lax.broadcasted_iota(jnp.int32, (c, c), 0)
    jj = lax.broadcasted_iota(jnp.int32, (c, c), 1)
    tri = ii >= jj

    bgs, cbs, y_inter = [], [], []
    for g in range(SSD_GROUPS):
        sl = slice(g * SSD_STATE, (g + 1) * SSD_STATE)
        bg = conv_tile(b_ref[:, sl], x_tiles + g).astype(BF16)
        cg = conv_tile(c_ref[:, sl], x_tiles + bc_tiles + g).astype(BF16)
        bgs.append(bg)
        cbs.append(jnp.where(tri, _dot_nt(cg, bg), 0.0))
        y_inter.append(_dot(cg, state[g].astype(BF16)))

    ssq = None
    a_tiles = []
    for p in range(SSD_HEADS // 2):
        g, pp = divmod(p, gh // 2)
        sl = slice(p * LANES, (p + 1) * LANES)
        x_ref, off = (x0_ref, p * LANES) if p < x_tiles // 2 else (x1_ref, (p - x_tiles // 2) * LANES)
        z_ref = z0_ref if p < x_tiles // 2 else z1_ref
        xs = conv_tile(x_ref[:, off:off + LANES], p)
        gcol = [jnp.broadcast_to(g2[:, h:h + 1], (c, LANES)) for h in (2 * p, 2 * p + 1)]
        dcol = [jnp.broadcast_to(dt[:, h:h + 1], (c, LANES)) for h in (2 * p, 2 * p + 1)]
        g_x = jnp.where(lane_lo, gcol[0], gcol[1])
        xdt = xs * jnp.where(lane_lo, dcol[0], dcol[1])
        g_last = g_x[c - 1:c, :]
        xw_scr[:, sl] = (xdt * jnp.exp2(g_last - g_x)).astype(BF16)
        a_tiles.append(jnp.exp2(g_last))
        ms = []
        for k, h in enumerate((2 * p, 2 * p + 1)):
            decay = jnp.exp2(jnp.minimum(gcol[k][:, 0:c] - g2_t[h:h + 1, :], 0.0))
            ms.append((cbs[g] * decay).astype(BF16))
        rhs = jnp.concatenate([jnp.where(lane_lo, xdt, 0.0), jnp.where(lane_lo, 0.0, xdt)], axis=0).astype(BF16)
        y = _dot(jnp.concatenate(ms, axis=1), rhs) + jnp.exp2(g_x) * y_inter[g][:, pp * LANES:(pp + 1) * LANES]
        so = y + dexp_ref[:, sl] * xs
        u = so * _silu(z_ref[:, off:off + LANES].astype(F32))
        ssq = u * u if ssq is None else ssq + u * u
        u_scr[:, sl] = u

    for g in range(SSD_GROUPS):
        gsl = slice(g * gw, (g + 1) * gw)
        a_row = jnp.concatenate(a_tiles[g * (gh // 2):(g + 1) * (gh // 2)], axis=-1)
        state[g] = a_row * state[g] + _dot_tn(bgs[g], xw_scr[:, gsl])

    rs = lax.rsqrt(jnp.sum(ssq, axis=-1, keepdims=True) * (1.0 / SSD_INNER) + EPS)
    out_ref[...] = (u_scr[...] * rs * nrm_ref[...]).astype(out_ref.dtype)

    @pl.when(pl.program_id(1) == n_steps - 1)
    def _():
        for g in range(SSD_GROUPS):
            sg = state[g]
            for hh in range(gh):
                snew_ref[g * gh + hh] = sg[:, hh * hd:(hh + 1) * hd]


def _ssd_parts(proj, dt_raw, conv0, s0, p, *, batch, seq, c):
    assert seq % c == 0
    nt = seq // c
    row = lambda b, t: b * nt + t
    const2 = lambda b, t: (0, 0)
    half = SSD_INNER // 2
    shift = _shift_matrix(c)
    return MixerParts(
        body=functools.partial(_ssd_body, n_steps=nt),
        in_specs=[
            pl.BlockSpec((c, half), lambda b, t: (row(b, t), OFF_SZ // half)),
            pl.BlockSpec((c, half), lambda b, t: (row(b, t), OFF_SZ // half + 1)),
            pl.BlockSpec((c, half), lambda b, t: (row(b, t), OFF_SX // half)),
            pl.BlockSpec((c, half), lambda b, t: (row(b, t), OFF_SX // half + 1)),
            pl.BlockSpec((c, SSD_BC), lambda b, t: (row(b, t), OFF_SB // SSD_BC)),
            pl.BlockSpec((c, SSD_BC), lambda b, t: (row(b, t), OFF_SC // SSD_BC)),
            pl.BlockSpec((c, LANES), lambda b, t: (row(b, t), 0)),
            pl.BlockSpec((None, SUBLANES, SSD_CONV_DIM), lambda b, t: (b, 0, 0)),
            pl.BlockSpec((None, SSD_HEADS, SSD_STATE, SSD_HEADDIM), lambda b, t: (b, 0, 0, 0)),
            pl.BlockSpec(shift.shape, const2),
            pl.BlockSpec((CONV_W, SSD_CONV_DIM), const2),
            pl.BlockSpec((1, SSD_CONV_DIM), const2),
            pl.BlockSpec((1, LANES), const2),
            pl.BlockSpec((1, LANES), const2),
            pl.BlockSpec((1, SSD_INNER), const2),
            pl.BlockSpec((1, SSD_INNER), const2),
        ],
        out_specs=[
            pl.BlockSpec((c, SSD_INNER), lambda b, t: (row(b, t), 0)),
            pl.BlockSpec((None, SSD_HEADS, SSD_STATE, SSD_HEADDIM), lambda b, t: (b, 0, 0, 0)),
            pl.BlockSpec((None, SUBLANES, SSD_CONV_DIM), lambda b, t: (b, 0, 0)),
        ],
        out_shape=[
            jax.ShapeDtypeStruct((batch * seq, SSD_INNER), BF16),
            jax.ShapeDtypeStruct((batch, SSD_HEADS, SSD_STATE, SSD_HEADDIM), F32),
            jax.ShapeDtypeStruct((batch, SUBLANES, SSD_CONV_DIM), F32),
        ],
        scratch=[
            pltpu.VMEM((_stage_rows(c), SSD_CONV_DIM), BF16),
            pltpu.VMEM((SSD_GROUPS, SSD_STATE, SSD_GROUP_WIDTH), F32),
            pltpu.VMEM((c, SSD_INNER), BF16),
            pltpu.VMEM((c, SSD_INNER), F32),
        ],
        args=(proj, proj, proj, proj, proj, proj, dt_raw, conv0, s0, shift, p["ssd_conv_w"], p["ssd_conv_b"],
              p["ssd_dt_bias"], p["ssd_a_log"], p["ssd_d_exp"], p["ssd_norm"]))


def _xattn_body(x_ref, nx_ref, wq_ref, mk_ref, mv_ref, wo_ref, nf_ref, o_ref, *rest, final):
    if final:
        hn_ref, part_ref = rest
    else:
        hn_next_ref, hn_ref, part_ref = rest
    _rmsnorm_rows_to(hn_ref, x_ref, nx_ref, part_ref)
    q = _dot(hn_ref[...], wq_ref[...])
    scale = XATT_HD ** -0.5
    os = []
    for h in range(XATT_HEADS):
        sl = slice(h * XATT_HD, (h + 1) * XATT_HD)
        s = _dot_nt(q[:, sl].astype(BF16), mk_ref[:, sl].astype(BF16)) * scale
        mx = jnp.max(s, axis=-1, keepdims=True)
        pr = jnp.exp(s - mx)
        den = jnp.sum(pr, axis=-1, keepdims=True)
        os.append(_dot(pr.astype(BF16), mv_ref[:, sl].astype(BF16)) / den)
    o = jnp.concatenate(os, axis=-1).astype(BF16)
    o_ref[...] = x_ref[...] + _dot(o, wo_ref[...])
    if final:
        _rmsnorm_rows_to(o_ref, o_ref, nf_ref, part_ref)
    else:
        _rmsnorm_rows_to(hn_next_ref, o_ref, nf_ref, part_ref)


def _cross_attn(x, mem_k, mem_v, nx, wq, wo, nf, *, batch, seq, tm, final):
    d = x.shape[1]
    tm = min(tm, seq)
    nt = seq // tm
    const2 = lambda b, t: (0, 0)
    x_spec = pl.BlockSpec((tm, d), lambda b, t: (b * nt + t, 0))
    out_specs = x_spec if final else [x_spec, x_spec]
    out_shape = jax.ShapeDtypeStruct(x.shape, F32)
    if not final:
        out_shape = [out_shape, jax.ShapeDtypeStruct(x.shape, BF16)]
    return pl.pallas_call(
        functools.partial(_xattn_body, final=final),
        grid=(batch, nt),
        in_specs=[
            x_spec,
            pl.BlockSpec((1, d), const2),
            pl.BlockSpec((d, XATT_DIM), const2),
            pl.BlockSpec((None, N_MEM, XATT_DIM), lambda b, t: (b, 0, 0)),
            pl.BlockSpec((None, N_MEM, XATT_DIM), lambda b, t: (b, 0, 0)),
            pl.BlockSpec((XATT_DIM, d), const2),
            pl.BlockSpec((1, d), const2),
        ],
        out_specs=out_specs,
        out_shape=out_shape,
        scratch_shapes=[pltpu.VMEM((tm, d), BF16), pltpu.VMEM((tm, LANES), F32)],
        compiler_params=_params(("arbitrary", "arbitrary")),
        name="cross_attn",
    )(x, nx.reshape(1, d), wq, mem_k, mem_v, wo, nf.reshape(1, d))


def _pad_state_rows(s):
    return jnp.pad(s.astype(F32), ((0, 0), (SUBLANES - (CONV_W - 1), 0), (0, 0)))


def _rope_tables(pos):
    half = RET_DK // 2
    inv = ROPE_BASE ** (-jnp.arange(half, dtype=F32) / half)
    ang = pos[:, None] * inv[None, :]
    cos, sin = jnp.cos(ang), jnp.sin(ang)
    cq = jnp.concatenate([cos, cos], axis=-1)
    sq = jnp.concatenate([-sin, sin], axis=-1)
    ks = RET_DK ** -0.5
    return cq, sq, cq * ks, sq * ks


def _layer_params(l, norm_mix, w_in, lru_conv_w, lru_conv_b, lru_wa, lru_ba, lru_wx, lru_bx, lru_lambda, lru_norm,
                  ret_norm, ssd_conv_w, ssd_conv_b, ssd_dt_bias, ssd_a_log, ssd_d, ssd_norm, w_out, norm_xattn,
                  norm_mem, w_q_mem, w_k_mem, w_v_mem, w_o_mem):
    w = w_in[l]
    w_main = w[:, :OFF_DT].astype(BF16)
    w_dt = jnp.pad(w[:, OFF_DT:], ((0, 0), (0, LANES - SSD_HEADS))).astype(BF16)
    row = lambda v: v.reshape(1, -1).astype(F32)
    pad_lanes = lambda v: jnp.pad(v.astype(F32), (0, LANES - v.shape[0])).reshape(1, LANES)
    return dict(
        norm_mix=norm_mix[l], w_main=w_main, w_dt=w_dt,
        lru_conv_w=lru_conv_w[l], lru_conv_b=row(lru_conv_b[l]), lru_wa=lru_wa[l].astype(BF16), lru_ba=row(lru_ba[l]),
        lru_wx=lru_wx[l].astype(BF16), lru_bx=row(lru_bx[l]), lru_lambda=row(lru_lambda[l]), lru_norm=row(lru_norm[l]),
        ret_norm=row(ret_norm[l]),
        ssd_conv_w=ssd_conv_w[l], ssd_conv_b=row(ssd_conv_b[l]), ssd_dt_bias=pad_lanes(ssd_dt_bias[l]),
        ssd_a_log=pad_lanes(ssd_a_log[l]), ssd_d_exp=row(jnp.repeat(ssd_d[l], SSD_HEADDIM)), ssd_norm=row(ssd_norm[l]),
        w_out=_cast_layer_bf16(w_out, l), norm_xattn=norm_xattn[l], norm_mem=norm_mem[l],
        w_q=_cast_layer_bf16(w_q_mem, l),
        w_kv=jnp.concatenate([_cast_layer_bf16(w_k_mem, l), _cast_layer_bf16(w_v_mem, l)], axis=1),
        w_o=_cast_layer_bf16(w_o_mem, l))


def _mixer_sublayer(x, hn, rope, lru_h0, lru_conv0, ret_s0, ssd_s0, ssd_conv0, p, *, batch, seq):
    proj, dt_raw = _in_proj(hn, p["w_main"], p["w_dt"], tm=TM_IN_PROJ, tn=TN_PROJ)
    c = min(CHUNK, seq)
    (lru_out, lru_h1, lru_ct), (ret_out, ret_s1), (ssd_out, ssd_s1, ssd_ct) = _run_mixers(
        [_lru_parts(proj, _pad_state_rows(lru_conv0), lru_h0.reshape(batch, 1, LRU_WIDTH), p,
                    batch=batch, seq=seq, tb=c),
         _ret_parts(proj, ret_s0, rope, p, batch=batch, seq=seq, c=c),
         _ssd_parts(proj, dt_raw, _pad_state_rows(ssd_conv0), ssd_s0, p, batch=batch, seq=seq, c=c)],
        batch=batch, nt=seq // c)
    x1 = _out_proj(x, lru_out, ret_out, ssd_out, p["w_out"], tm=TM_OUT, tn=TN_OUT)
    tail = SUBLANES - (CONV_W - 1)
    states = (lru_h1.reshape(batch, LRU_WIDTH), lru_ct[:, tail:, :], ret_s1, ssd_s1, ssd_ct[:, tail:, :])
    return x1, states


def kernel(x_prompt, x_sample, mem_prompt, state_lru_h, state_lru_conv, state_ret, state_ssd, state_ssd_conv, cache_mem_k, cache_mem_v, norm_mix, w_in, lru_conv_w, lru_conv_b, lru_wa, lru_ba, lru_wx, lru_bx, lru_lambda, lru_norm, ret_norm, ssd_conv_w, ssd_conv_b, ssd_dt_bias, ssd_a_log, ssd_d, ssd_norm, w_out, norm_xattn, norm_mem, w_q_mem, w_k_mem, w_v_mem, w_o_mem, norm_final):
    bp, tp, d = x_prompt.shape
    bs, ts, _ = x_sample.shape
    depth = w_in.shape[0]
    assert tp % BF16_ROWS == 0 and ts % BF16_ROWS == 0
    rope_p = _rope_tables(jnp.arange(tp, dtype=F32))
    rope_s = _rope_tables(PAST_LEN + jnp.arange(ts, dtype=F32))
    z_lru_h = jnp.zeros((bp, LRU_WIDTH), F32)
    z_lru_conv = jnp.zeros((bp, CONV_W - 1, LRU_WIDTH), F32)
    z_ret = jnp.zeros((bp, RET_HEADS, RET_DK, RET_DV), F32)
    z_ssd = jnp.zeros((bp, SSD_HEADS, SSD_STATE, SSD_HEADDIM), F32)
    z_ssd_conv = jnp.zeros((bp, CONV_W - 1, SSD_CONV_DIM), F32)

    xp = x_prompt.reshape(bp * tp, d)
    xs = x_sample.reshape(bs * ts, d)
    mem = mem_prompt.reshape(bp * N_MEM, d)
    hn_p = _rmsnorm_bf16(xp, norm_mix[0], tm=TM_XATT)
    hn_s = _rmsnorm_bf16(xs, norm_mix[0], tm=TM_XATT)
    st_p = [[] for _ in range(5)]
    st_s = [[] for _ in range(5)]
    mk_p, mv_p = [], []
    for l in range(depth):
        p = _layer_params(l, norm_mix, w_in, lru_conv_w, lru_conv_b, lru_wa, lru_ba, lru_wx, lru_bx, lru_lambda,
                          lru_norm, ret_norm, ssd_conv_w, ssd_conv_b, ssd_dt_bias, ssd_a_log, ssd_d, ssd_norm, w_out,
                          norm_xattn, norm_mem, w_q_mem, w_k_mem, w_v_mem, w_o_mem)
        final = l == depth - 1
        next_gain = norm_final if final else norm_mix[l + 1]
        xp, st = _mixer_sublayer(xp, hn_p, rope_p, z_lru_h, z_lru_conv, z_ret, z_ssd, z_ssd_conv, p, batch=bp, seq=tp)
        mkv = _norm_matmul(mem, p["norm_mem"], p["w_kv"], out_dtype=F32, tm=TM_PROJ, tn=TN_PROJ)
        mk = mkv[:, :XATT_DIM].reshape(bp, N_MEM, XATT_DIM)
        mv = mkv[:, XATT_DIM:].reshape(bp, N_MEM, XATT_DIM)
        res = _cross_attn(xp, mk, mv, p["norm_xattn"], p["w_q"], p["w_o"], next_gain, batch=bp, seq=tp, tm=TM_XATT,
                          final=final)
        xp, hn_p = (res, None) if final else res
        for acc, v in zip(st_p, st):
            acc.append(v)
        mk_p.append(mk.reshape(bp, N_MEM, XATT_HEADS, XATT_HD))
        mv_p.append(mv.reshape(bp, N_MEM, XATT_HEADS, XATT_HD))
        xs, st = _mixer_sublayer(xs, hn_s, rope_s, state_lru_h[l], state_lru_conv[l], state_ret[l], state_ssd[l],
                                 state_ssd_conv[l], p, batch=bs, seq=ts)
        res = _cross_attn(xs, cache_mem_k[l].reshape(bs, N_MEM, XATT_DIM), cache_mem_v[l].reshape(bs, N_MEM, XATT_DIM),
                          p["norm_xattn"], p["w_q"], p["w_o"], next_gain, batch=bs, seq=ts, tm=TM_XATT, final=final)
        xs, hn_s = (res, None) if final else res
        for acc, v in zip(st_s, st):
            acc.append(v)

    y_prompt = xp.reshape(bp, tp, d)
    y_sample = xs.reshape(bs, ts, d)
    return (y_prompt, y_sample,
            jnp.stack(st_p[0]), jnp.stack(st_p[1]), jnp.stack(st_p[2]), jnp.stack(st_p[3]), jnp.stack(st_p[4]),
            jnp.stack(mk_p), jnp.stack(mv_p),
            jnp.stack(st_s[0]), jnp.stack(st_s[1]), jnp.stack(st_s[2]), jnp.stack(st_s[3]), jnp.stack(st_s[4]))
```

```python
import functools
import math
from typing import Callable, NamedTuple

import numpy as np
import jax
import jax.numpy as jnp
from jax import lax
from jax.experimental import pallas as pl
from jax.experimental.pallas import tpu as pltpu

F32 = jnp.float32
BF16 = jnp.bfloat16

EPS = 1e-6
CONV_W = 4
PAST_LEN = 4096
N_MEM = 256
LRU_WIDTH = 1024
LRU_BLOCKS = 4
LRU_BLOCK = LRU_WIDTH // LRU_BLOCKS
LRU_C = 8.0
RET_HEADS = 4
RET_DK = 128
RET_DV = 256
RET_QK = RET_HEADS * RET_DK
RET_V = RET_HEADS * RET_DV
ROPE_BASE = 10000.0
SSD_INNER = 2048
SSD_HEADDIM = 64
SSD_HEADS = SSD_INNER // SSD_HEADDIM
SSD_STATE = 128
SSD_GROUPS = 4
SSD_GROUP_HEADS = SSD_HEADS // SSD_GROUPS
SSD_GROUP_WIDTH = SSD_GROUP_HEADS * SSD_HEADDIM
SSD_BC = SSD_GROUPS * SSD_STATE
SSD_CONV_DIM = SSD_INNER + 2 * SSD_BC
XATT_HEADS = 4
XATT_HD = 128
XATT_DIM = XATT_HEADS * XATT_HD

SUBLANES = 8
LANES = 128
BF16_ROWS = 2 * SUBLANES
VMEM_LIMIT_BYTES = 56 * 1024 * 1024

OFF_LX, OFF_LG, OFF_RQ, OFF_RK, OFF_RV, OFF_RG, OFF_SZ, OFF_SX = 0, 1024, 2048, 2560, 3072, 4096, 5120, 7168
OFF_SB = OFF_SX + SSD_INNER
OFF_SC = OFF_SB + SSD_BC
OFF_DT = OFF_SX + SSD_CONV_DIM

RET_LOG_GAMMA = tuple(math.log1p(-(2.0 ** (-5.0 - h))) for h in range(RET_HEADS))
LOG2E = math.log2(math.e)

TM_PROJ, TN_PROJ = 512, 1024
TM_IN_PROJ = 1024
TM_OUT, TN_OUT = 1024, 512
CAST_ROWS = 256
TM_XATT = 256
CHUNK = 128
CHUNK_RET = 256


def _params(semantics):
    return pltpu.CompilerParams(dimension_semantics=semantics, vmem_limit_bytes=VMEM_LIMIT_BYTES)


def _sigmoid(x):
    return 1.0 / (1.0 + jnp.exp(-x))


def _silu(x):
    return x * _sigmoid(x)


def _softplus(x):
    return jnp.maximum(x, 0.0) + jnp.log1p(jnp.exp(-jnp.abs(x)))


def _dot(a, b):
    return jnp.dot(a, b, preferred_element_type=F32)


def _dot_nt(a, b):
    return lax.dot_general(a, b, (((1,), (1,)), ((), ())), preferred_element_type=F32)


def _dot_tn(a, b):
    return lax.dot_general(a, b, (((0,), (0,)), ((), ())), preferred_element_type=F32)


def _rmsnorm_rows_to(dst_ref, src_ref, gain_ref, part_ref, unroll=False):
    tm, d = src_ref.shape
    rc = BF16_ROWS
    lane_tiles = d // LANES

    def sum_squares(r, carry):
        rows = pl.ds(pl.multiple_of(r * rc, rc), rc)
        accs = [None] * min(4, lane_tiles)
        for j in range(lane_tiles):
            v = src_ref[rows, j * LANES:(j + 1) * LANES].astype(F32)
            k = j % len(accs)
            accs[k] = v * v if accs[k] is None else accs[k] + v * v
        while len(accs) > 1:
            accs = [a + b for a, b in zip(accs[0::2], accs[1::2])]
        part_ref[rows, :] = accs[0]
        return carry

    lax.fori_loop(0, tm // rc, sum_squares, 0, unroll=unroll)
    ms = jnp.sum(part_ref[...], axis=-1, keepdims=True) * (1.0 / d)
    part_ref[...] = jnp.broadcast_to(lax.rsqrt(ms + EPS), part_ref.shape)

    def scale(r, carry):
        rows = pl.ds(pl.multiple_of(r * rc, rc), rc)
        rs = part_ref[rows, :]
        for j in range(lane_tiles):
            sl = slice(j * LANES, (j + 1) * LANES)
            dst_ref[rows, sl] = (src_ref[rows, sl].astype(F32) * rs * gain_ref[:, sl]).astype(dst_ref.dtype)
        return carry

    lax.fori_loop(0, tm // rc, scale, 0, unroll=unroll)


def _cast_body(w_ref, o_ref):
    o_ref[...] = w_ref[...].astype(o_ref.dtype)


def _cast_layer_bf16(w_stack, l):
    _, r, c = w_stack.shape
    rows = min(CAST_ROWS, r)
    assert r % rows == 0
    return pl.pallas_call(
        _cast_body,
        grid=(r // rows,),
        in_specs=[pl.BlockSpec((None, rows, c), lambda i: (l, i, 0))],
        out_specs=pl.BlockSpec((rows, c), lambda i: (i, 0)),
        out_shape=jax.ShapeDtypeStruct((r, c), BF16),
        compiler_params=_params(("arbitrary",)),
        name="cast_bf16",
    )(w_stack)


def _norm_matmul_body(x_ref, g_ref, w_ref, o_ref, hn_ref, part_ref):
    @pl.when(pl.program_id(1) == 0)
    def _():
        _rmsnorm_rows_to(hn_ref, x_ref, g_ref, part_ref)

    o_ref[...] = _dot(hn_ref[...], w_ref[...]).astype(o_ref.dtype)


def _norm_matmul(x, g, w, *, out_dtype, tm, tn):
    m, k = x.shape
    n = w.shape[1]
    tm = min(tm, m)
    tn = min(tn, n)
    assert m % tm == 0 and n % tn == 0
    return pl.pallas_call(
        _norm_matmul_body,
        grid=(m // tm, n // tn),
        in_specs=[
            pl.BlockSpec((tm, k), lambda i, j: (i, 0)),
            pl.BlockSpec((1, k), lambda i, j: (0, 0)),
            pl.BlockSpec((k, tn), lambda i, j: (0, j)),
        ],
        out_specs=pl.BlockSpec((tm, tn), lambda i, j: (i, j)),
        out_shape=jax.ShapeDtypeStruct((m, n), out_dtype),
        scratch_shapes=[pltpu.VMEM((tm, k), BF16), pltpu.VMEM((tm, LANES), F32)],
        compiler_params=_params(("arbitrary", "arbitrary")),
        name="norm_matmul",
    )(x, g.reshape(1, k), w)


def _rmsnorm_body(x_ref, g_ref, o_ref, part_ref):
    _rmsnorm_rows_to(o_ref, x_ref, g_ref, part_ref)


def _rmsnorm_bf16(x, g, *, tm):
    m, k = x.shape
    tm = min(tm, m)
    assert m % tm == 0
    return pl.pallas_call(
        _rmsnorm_body,
        grid=(m // tm,),
        in_specs=[pl.BlockSpec((tm, k), lambda i: (i, 0)), pl.BlockSpec((1, k), lambda i: (0, 0))],
        out_specs=pl.BlockSpec((tm, k), lambda i: (i, 0)),
        out_shape=jax.ShapeDtypeStruct((m, k), BF16),
        scratch_shapes=[pltpu.VMEM((tm, LANES), F32)],
        compiler_params=_params(("arbitrary",)),
        name="rmsnorm_bf16",
    )(x, g.reshape(1, k))


def _in_proj_body(hn_ref, w_ref, w2_ref, o_ref, o2_ref):
    @pl.when(pl.program_id(1) == 0)
    def _():
        o2_ref[...] = _dot(hn_ref[...], w2_ref[...])

    o_ref[...] = _dot(hn_ref[...], w_ref[...]).astype(o_ref.dtype)


def _in_proj(hn, w, l, w2, *, n, tm, tn):
    m, k = hn.shape
    n2 = w2.shape[1]
    tm = min(tm, m)
    assert m % tm == 0 and n % tn == 0 and n <= w.shape[2]
    return pl.pallas_call(
        _in_proj_body,
        grid=(m // tm, n // tn),
        in_specs=[
            pl.BlockSpec((tm, k), lambda i, j: (i, 0)),
            pl.BlockSpec((None, k, tn), lambda i, j: (l, 0, j)),
            pl.BlockSpec((k, n2), lambda i, j: (0, 0)),
        ],
        out_specs=[pl.BlockSpec((tm, tn), lambda i, j: (i, j)), pl.BlockSpec((tm, n2), lambda i, j: (i, 0))],
        out_shape=[jax.ShapeDtypeStruct((m, n), BF16), jax.ShapeDtypeStruct((m, n2), F32)],
        compiler_params=_params(("arbitrary", "arbitrary")),
        name="in_proj",
    )(hn, w, w2)


def _out_proj_body(x_ref, a1_ref, a2_ref, a3_ref, w1_ref, w2_ref, w3_ref, o_ref):
    acc = _dot(a1_ref[...], w1_ref[...])
    acc = acc + _dot(a2_ref[...], w2_ref[...])
    acc = acc + _dot(a3_ref[...], w3_ref[...])
    o_ref[...] = x_ref[...] + acc


def _out_proj(x, lru_out, ret_out, ssd_out, w_out_bf16, *, tm, tn):
    m, n = x.shape
    tm = min(tm, m)
    tn = min(tn, n)
    assert m % tm == 0 and n % tn == 0
    return pl.pallas_call(
        _out_proj_body,
        grid=(m // tm, n // tn),
        in_specs=[
            pl.BlockSpec((tm, tn), lambda i, j: (i, j)),
            pl.BlockSpec((tm, LRU_WIDTH), lambda i, j: (i, 0)),
            pl.BlockSpec((tm, RET_V), lambda i, j: (i, 0)),
            pl.BlockSpec((tm, SSD_INNER), lambda i, j: (i, 0)),
            pl.BlockSpec((LRU_WIDTH, tn), lambda i, j: (0, j)),
            pl.BlockSpec((RET_V, tn), lambda i, j: (1, j)),
            pl.BlockSpec((SSD_INNER, tn), lambda i, j: (1, j)),
        ],
        out_specs=pl.BlockSpec((tm, tn), lambda i, j: (i, j)),
        out_shape=jax.ShapeDtypeStruct((m, n), F32),
        compiler_params=_params(("arbitrary", "arbitrary")),
        name="out_proj",
    )(x, lru_out, ret_out, ssd_out, w_out_bf16, w_out_bf16, w_out_bf16)


HIST_ROWS = BF16_ROWS


def _stage_rows(c):
    return -(-(HIST_ROWS + c) // LANES) * LANES


def _shift_matrix(c):
    s = np.zeros((CONV_W - 1, c, _stage_rows(c)), np.float32)
    for k in range(CONV_W - 1):
        back = k + 1
        for t in range(c):
            if t >= back:
                s[k, t, HIST_ROWS + t - back] = 1.0
            else:
                j = (CONV_W - 1) + t - back
                s[k, t, j] = s[k, t, 3 + j] = s[k, t, SUBLANES + j] = 1.0
    return jnp.asarray(s.reshape((CONV_W - 1) * c, -1), dtype=BF16)


def _history_tile(tail8):
    r = pltpu.roll(tail8, CONV_W - 1, 0)
    hi = r.astype(BF16).astype(F32)
    rest = r - hi
    mid = rest.astype(BF16).astype(F32)
    lo = rest - mid
    rows = lax.broadcasted_iota(jnp.int32, r.shape, 0)
    top = jnp.where(rows < 3, hi, jnp.where(rows < 6, pltpu.roll(mid, CONV_W - 1, 0), 0.0))
    bot = jnp.where(rows < 3, lo, 0.0)
    return jnp.concatenate([top, bot], axis=0).astype(BF16)


def _conv_from_stage(stage, shift_ref, x_f32, c, cw_ref, cb_ref):
    sh = _dot(shift_ref[...], stage[...])
    acc = sh[2 * c:3 * c] * cw_ref[0:1, :]
    acc = acc + sh[c:2 * c] * cw_ref[1:2, :]
    acc = acc + sh[0:c] * cw_ref[2:3, :]
    acc = acc + x_f32 * cw_ref[3:4, :]
    return cb_ref[...] + acc


class MixerParts(NamedTuple):
    body: Callable
    args: tuple
    in_specs: list
    out_specs: list
    out_shape: list
    scratch: list


def _mixers_body(*refs, parts):
    n_in = sum(len(p.args) for p in parts)
    n_out = sum(len(p.out_specs) for p in parts)
    ins, outs, scr = refs[:n_in], refs[n_in:n_in + n_out], refs[n_in + n_out:]
    i = o = s = 0
    for p in parts:
        ni, no, ns = len(p.args), len(p.out_specs), len(p.scratch)
        p.body(*ins[i:i + ni], *outs[o:o + no], *scr[s:s + ns])
        i, o, s = i + ni, o + no, s + ns


def _run_mixers(parts, *, batch, nt):
    flat = lambda field: [x for p in parts for x in getattr(p, field)]
    bodies = tuple(p._replace(args=tuple(None for _ in p.args), out_shape=[]) for p in parts)
    outs = pl.pallas_call(
        functools.partial(_mixers_body, parts=bodies),
        grid=(batch, nt),
        in_specs=flat("in_specs"),
        out_specs=flat("out_specs"),
        out_shape=flat("out_shape"),
        scratch_shapes=flat("scratch"),
        compiler_params=_params(("arbitrary", "arbitrary")),
        name="mixers",
    )(*flat("args"))
    res, o = [], 0
    for p in parts:
        res.append(outs[o:o + len(p.out_specs)])
        o += len(p.out_specs)
    return res


def _lru_body(lx_ref, lg_ref, conv0_ref, h0_ref, shift_ref, cw_ref, cb_ref, wa_ref, ba_ref, wx_ref, bx_ref, lam_ref,
              nrm_ref, out_ref, hlast_ref, ctail_ref, stage, hcar, a_scr, b_scr):
    tb = lx_ref.shape[0]

    @pl.when(pl.program_id(1) == 0)
    def _():
        stage[...] = jnp.zeros(stage.shape, stage.dtype)
        stage[0:HIST_ROWS, :] = _history_tile(conv0_ref[...])
        hcar[...] = h0_ref[...]

    stage[pl.ds(HIST_ROWS, tb), :] = lx_ref[...]
    x_in = lx_ref[...].astype(F32)
    xc = _conv_from_stage(stage, shift_ref, x_in, tb, cw_ref, cb_ref)
    tail = x_in[tb - SUBLANES:tb]
    stage[0:HIST_ROWS, :] = _history_tile(tail)
    ctail_ref[...] = tail

    xb = xc.astype(BF16)
    ra, ri = [], []
    for n in range(LRU_BLOCKS):
        xn = xb[:, n * LRU_BLOCK:(n + 1) * LRU_BLOCK]
        ra.append(_dot(xn, wa_ref[n]))
        ri.append(_dot(xn, wx_ref[n]))
    r = _sigmoid(jnp.concatenate(ra, axis=-1) + ba_ref[...])
    i = _sigmoid(jnp.concatenate(ri, axis=-1) + bx_ref[...])
    log_a = r * ((-LRU_C) * _softplus(-lam_ref[...]))
    a = jnp.exp(log_a)
    th = jnp.tanh(log_a)
    b = jnp.sqrt(-2.0 * th / (1.0 - th)) * (i * xc)

    ng = tb // SUBLANES
    for j in range(LRU_WIDTH // LANES):
        a_scr[j] = a[:, j * LANES:(j + 1) * LANES]
        b_scr[j] = b[:, j * LANES:(j + 1) * LANES]
    for j in range(LRU_WIDTH // LANES):
        pa = a_scr[j, pl.ds(0, ng, stride=SUBLANES), :]
        pb = b_scr[j, pl.ds(0, ng, stride=SUBLANES), :]
        for r in range(1, SUBLANES):
            view = pl.ds(r, ng, stride=SUBLANES)
            ar = a_scr[j, view, :]
            pb = ar * pb + b_scr[j, view, :]
            pa = ar * pa
            a_scr[j, view, :] = pa
            b_scr[j, view, :] = pb
    a3 = jnp.concatenate([a_scr[j] for j in range(LRU_WIDTH // LANES)], axis=-1).reshape(ng, SUBLANES, LRU_WIDTH)
    b3 = jnp.concatenate([b_scr[j] for j in range(LRU_WIDTH // LANES)], axis=-1).reshape(ng, SUBLANES, LRU_WIDTH)
    hprev = hcar[...]
    hs = []
    for g in range(ng):
        hg = b3[g] + a3[g] * hprev
        hprev = hg[SUBLANES - 1:SUBLANES]
        hs.append(hg)
    h = jnp.concatenate(hs, axis=0)
    hcar[...] = hprev
    hlast_ref[...] = hprev

    ms = jnp.mean(h * h, axis=-1, keepdims=True)
    out = h * lax.rsqrt(ms + EPS) * nrm_ref[...]
    out_ref[...] = (out * _silu(lg_ref[...].astype(F32))).astype(out_ref.dtype)


def _lru_parts(proj, conv0, h0, p, *, batch, seq, tb):
    assert seq % tb == 0
    nt = seq // tb
    w = LRU_WIDTH
    row = lambda b, t: b * nt + t
    const2 = lambda b, t: (0, 0)
    shift = _shift_matrix(tb)
    return MixerParts(
        body=_lru_body,
        in_specs=[
            pl.BlockSpec((tb, w), lambda b, t: (row(b, t), OFF_LX // w)),
            pl.BlockSpec((tb, w), lambda b, t: (row(b, t), OFF_LG // w)),
            pl.BlockSpec((None, SUBLANES, w), lambda b, t: (b, 0, 0)),
            pl.BlockSpec((None, 1, w), lambda b, t: (b, 0, 0)),
            pl.BlockSpec(shift.shape, const2),
            pl.BlockSpec((CONV_W, w), const2),
            pl.BlockSpec((1, w), const2),
            pl.BlockSpec((LRU_BLOCKS, LRU_BLOCK, LRU_BLOCK), lambda b, t: (0, 0, 0)),
            pl.BlockSpec((1, w), const2),
            pl.BlockSpec((LRU_BLOCKS, LRU_BLOCK, LRU_BLOCK), lambda b, t: (0, 0, 0)),
            pl.BlockSpec((1, w), const2),
            pl.BlockSpec((1, w), const2),
            pl.BlockSpec((1, w), const2),
        ],
        out_specs=[
            pl.BlockSpec((tb, w), lambda b, t: (row(b, t), 0)),
            pl.BlockSpec((None, 1, w), lambda b, t: (b, 0, 0)),
            pl.BlockSpec((None, SUBLANES, w), lambda b, t: (b, 0, 0)),
        ],
        out_shape=[
            jax.ShapeDtypeStruct((batch * seq, w), BF16),
            jax.ShapeDtypeStruct((batch, 1, w), F32),
            jax.ShapeDtypeStruct((batch, SUBLANES, w), F32),
        ],
        scratch=[pltpu.VMEM((_stage_rows(tb), w), BF16), pltpu.VMEM((1, w), F32),
                 pltpu.VMEM((w // LANES, tb, LANES), F32), pltpu.VMEM((w // LANES, tb, LANES), F32)],
        args=(proj, proj, conv0, h0, shift, p["lru_conv_w"], p["lru_conv_b"], p["lru_wa"], p["lru_ba"], p["lru_wx"],
              p["lru_bx"], p["lru_lambda"], p["lru_norm"]))


def _ret_body(q_ref, k_ref, v_ref, g_ref, cq_ref, sq_ref, ck_ref, sk_ref, s0_ref, nrm_ref,
              out_ref, snew_ref, state, decay_tab, eg_tab, wt_tab):
    c = q_ref.shape[0]

    @pl.when(jnp.logical_and(pl.program_id(0) == 0, pl.program_id(1) == 0))
    def _():
        ii = lax.broadcasted_iota(jnp.int32, (c, c), 0)
        jj = lax.broadcasted_iota(jnp.int32, (c, c), 1)
        dpos = jnp.maximum(ii - jj, 0).astype(F32)
        rpos = lax.broadcasted_iota(jnp.int32, (c, RET_DK), 0).astype(F32)
        for h in range(RET_HEADS):
            lg = RET_LOG_GAMMA[h]
            decay_tab[h] = jnp.where(ii >= jj, jnp.exp(dpos * lg), 0.0)
            eg_tab[h] = jnp.exp((rpos + 1.0) * lg)
            wt_tab[h] = jnp.exp((c - 1.0 - rpos) * lg)

    @pl.when(pl.program_id(1) == 0)
    def _():
        state[...] = s0_ref[...]

    cq, sq, ck, sk = cq_ref[...], sq_ref[...], ck_ref[...], sk_ref[...]
    outs = []
    for h in range(RET_HEADS):
        decay, e_g, w_tail = decay_tab[h], eg_tab[h], wt_tab[h]
        a_chunk = math.exp(c * RET_LOG_GAMMA[h])
        qh = q_ref[:, h * RET_DK:(h + 1) * RET_DK].astype(F32)
        kh = k_ref[:, h * RET_DK:(h + 1) * RET_DK].astype(F32)
        qr = qh * cq + pltpu.roll(qh, RET_DK // 2, 1) * sq
        kr = kh * ck + pltpu.roll(kh, RET_DK // 2, 1) * sk
        vh = v_ref[:, h * RET_DV:(h + 1) * RET_DV]
        sh = state[h]
        scores = _dot_nt(qr.astype(BF16), kr.astype(BF16)) * decay
        y = _dot(scores.astype(BF16), vh) + _dot((qr * e_g).astype(BF16), sh.astype(BF16))
        state[h] = a_chunk * sh + _dot_tn((kr * w_tail).astype(BF16), vh)
        mu = jnp.mean(y, axis=-1, keepdims=True)
        d = y - mu
        var = jnp.mean(d * d, axis=-1, keepdims=True)
        outs.append(d * lax.rsqrt(var + EPS))
    ro = jnp.concatenate(outs, axis=-1) * nrm_ref[...]
    out_ref[...] = (ro * _silu(g_ref[...].astype(F32))).astype(out_ref.dtype)
    snew_ref[...] = state[...]


def _ret_parts(proj, s0, rope, p, *, batch, seq, c):
    assert seq % c == 0
    nt = seq // c
    row = lambda b, t: b * nt + t
    tab = pl.BlockSpec((c, RET_DK), lambda b, t: (t, 0))
    return MixerParts(
        body=_ret_body,
        in_specs=[
            pl.BlockSpec((c, RET_QK), lambda b, t: (row(b, t), OFF_RQ // RET_QK)),
            pl.BlockSpec((c, RET_QK), lambda b, t: (row(b, t), OFF_RK // RET_QK)),
            pl.BlockSpec((c, RET_V), lambda b, t: (row(b, t), OFF_RV // RET_V)),
            pl.BlockSpec((c, RET_V), lambda b, t: (row(b, t), OFF_RG // RET_V)),
            tab, tab, tab, tab,
            pl.BlockSpec((None, RET_HEADS, RET_DK, RET_DV), lambda b, t: (b, 0, 0, 0)),
            pl.BlockSpec((1, RET_V), lambda b, t: (0, 0)),
        ],
        out_specs=[
            pl.BlockSpec((c, RET_V), lambda b, t: (row(b, t), 0)),
            pl.BlockSpec((None, RET_HEADS, RET_DK, RET_DV), lambda b, t: (b, 0, 0, 0)),
        ],
        out_shape=[
            jax.ShapeDtypeStruct((batch * seq, RET_V), BF16),
            jax.ShapeDtypeStruct((batch, RET_HEADS, RET_DK, RET_DV), F32),
        ],
        scratch=[
            pltpu.VMEM((RET_HEADS, RET_DK, RET_DV), F32),
            pltpu.VMEM((RET_HEADS, c, c), F32),
            pltpu.VMEM((RET_HEADS, c, RET_DK), F32),
            pltpu.VMEM((RET_HEADS, c, RET_DK), F32),
        ],
        args=(proj, proj, proj, proj, rope[0], rope[1], rope[2], rope[3], s0, p["ret_norm"]))


def _cumsum_rows(x):
    n = x.shape[0]
    rows = lax.broadcasted_iota(jnp.int32, x.shape, 0)
    s = 1
    while s < n:
        x = x + jnp.where(rows >= s, pltpu.roll(x, s, 0), 0.0)
        s *= 2
    return x


def _ssd_body(z0_ref, z1_ref, x0_ref, x1_ref, b_ref, c_ref, dt_ref, conv0_ref, s0_ref, shift_ref, cw_ref, cb_ref,
              dtb_ref, alog_ref, dexp_ref, nrm_ref, out_ref, snew_ref, ctail_ref, stage, state, xw_scr, u_scr, *,
              n_steps):
    c = x0_ref.shape[0]
    gh, hd, gw = SSD_GROUP_HEADS, SSD_HEADDIM, SSD_GROUP_WIDTH
    half = SSD_INNER // 2

    @pl.when(pl.program_id(1) == 0)
    def _():
        stage[...] = jnp.zeros(stage.shape, stage.dtype)
        stage[0:HIST_ROWS, :] = _history_tile(conv0_ref[...])
        for g in range(SSD_GROUPS):
            state[g] = jnp.concatenate([s0_ref[g * gh + hh] for hh in range(gh)], axis=-1)

    blk = pl.ds(HIST_ROWS, c)
    stage[blk, 0:half] = x0_ref[...]
    stage[blk, half:SSD_INNER] = x1_ref[...]
    stage[blk, SSD_INNER:SSD_INNER + SSD_BC] = b_ref[...]
    stage[blk, SSD_INNER + SSD_BC:SSD_CONV_DIM] = c_ref[...]
    sh = _dot(shift_ref[...], stage[...])
    tail = stage[pl.ds(HIST_ROWS + c - BF16_ROWS, BF16_ROWS), :].astype(F32)[SUBLANES:]
    stage[0:HIST_ROWS, :] = _history_tile(tail)
    ctail_ref[...] = tail

    def conv_tile(x_bf16, j):
        sl = slice(j * LANES, (j + 1) * LANES)
        acc = sh[2 * c:3 * c, sl] * cw_ref[0:1, sl]
        acc = acc + sh[c:2 * c, sl] * cw_ref[1:2, sl]
        acc = acc + sh[0:c, sl] * cw_ref[2:3, sl]
        acc = acc + x_bf16.astype(F32) * cw_ref[3:4, sl]
        return _silu(cb_ref[:, sl] + acc)

    x_tiles = SSD_INNER // LANES
    bc_tiles = SSD_BC // LANES

    dt = _softplus(dt_ref[...] + dtb_ref[...])
    a_neg = -jnp.exp(alog_ref[...])
    g2 = _cumsum_rows(dt * a_neg) * LOG2E
    g2_t = g2.T
    lane_lo = lax.broadcasted_iota(jnp.int32, (c, LANES), 1) < hd
    ii = lax.broadcasted_iota(jnp.int32, (c, c), 0)
    jj = lax.broadcasted_iota(jnp.int32, (c, c), 1)
    tri = ii >= jj

    bgs, cbs, y_inter = [], [], []
    for g in range(SSD_GROUPS):
        sl = slice(g * SSD_STATE, (g + 1) * SSD_STATE)
        bg = conv_tile(b_ref[:, sl], x_tiles + g).astype(BF16)
        cg = conv_tile(c_ref[:, sl], x_tiles + bc_tiles + g).astype(BF16)
        bgs.append(bg)
        cbs.append(jnp.where(tri, _dot_nt(cg, bg), 0.0))
        y_inter.append(_dot(cg, state[g].astype(BF16)))

    ssq = None
    a_tiles = []
    for p in range(SSD_HEADS // 2):
        g, pp = divmod(p, gh // 2)
        sl = slice(p * LANES, (p + 1) * LANES)
        x_ref, off = (x0_ref, p * LANES) if p < x_tiles // 2 else (x1_ref, (p - x_tiles // 2) * LANES)
        z_ref = z0_ref if p < x_tiles // 2 else z1_ref
        xs = conv_tile(x_ref[:, off:off + LANES], p)
        gcol = [jnp.broadcast_to(g2[:, h:h + 1], (c, LANES)) for h in (2 * p, 2 * p + 1)]
        dcol = [jnp.broadcast_to(dt[:, h:h + 1], (c, LANES)) for h in (2 * p, 2 * p + 1)]
        g_x = jnp.where(lane_lo, gcol[0], gcol[1])
        xdt = xs * jnp.where(lane_lo, dcol[0], dcol[1])
        g_last = g_x[c - 1:c, :]
        xw_scr[:, sl] = (xdt * jnp.exp2(g_last - g_x)).astype(BF16)
        a_tiles.append(jnp.exp2(g_last))
        ms = []
        for k, h in enumerate((2 * p, 2 * p + 1)):
            decay = jnp.exp2(jnp.minimum(gcol[k][:, 0:c] - g2_t[h:h + 1, :], 0.0))
            ms.append((cbs[g] * decay).astype(BF16))
        rhs = jnp.concatenate([jnp.where(lane_lo, xdt, 0.0), jnp.where(lane_lo, 0.0, xdt)], axis=0).astype(BF16)
        y = _dot(jnp.concatenate(ms, axis=1), rhs) + jnp.exp2(g_x) * y_inter[g][:, pp * LANES:(pp + 1) * LANES]
        so = y + dexp_ref[:, sl] * xs
        u = so * _silu(z_ref[:, off:off + LANES].astype(F32))
        ssq = u * u if ssq is None else ssq + u * u
        u_scr[:, sl] = u

    for g in range(SSD_GROUPS):
        gsl = slice(g * gw, (g + 1) * gw)
        a_row = jnp.concatenate(a_tiles[g * (gh // 2):(g + 1) * (gh // 2)], axis=-1)
        state[g] = a_row * state[g] + _dot_tn(bgs[g], xw_scr[:, gsl])

    rs = lax.rsqrt(jnp.sum(ssq, axis=-1, keepdims=True) * (1.0 / SSD_INNER) + EPS)
    out_ref[...] = (u_scr[...] * rs * nrm_ref[...]).astype(out_ref.dtype)

    @pl.when(pl.program_id(1) == n_steps - 1)
    def _():
        for g in range(SSD_GROUPS):
            sg = state[g]
            for hh in range(gh):
                snew_ref[g * gh + hh] = sg[:, hh * hd:(hh + 1) * hd]


def _ssd_parts(proj, dt_raw, conv0, s0, p, *, batch, seq, c):
    assert seq % c == 0
    nt = seq // c
    row = lambda b, t: b * nt + t
    const2 = lambda b, t: (0, 0)
    half = SSD_INNER // 2
    shift = _shift_matrix(c)
    return MixerParts(
        body=functools.partial(_ssd_body, n_steps=nt),
        in_specs=[
            pl.BlockSpec((c, half), lambda b, t: (row(b, t), OFF_SZ // half)),
            pl.BlockSpec((c, half), lambda b, t: (row(b, t), OFF_SZ // half + 1)),
            pl.BlockSpec((c, half), lambda b, t: (row(b, t), OFF_SX // half)),
            pl.BlockSpec((c, half), lambda b, t: (row(b, t), OFF_SX // half + 1)),
            pl.BlockSpec((c, SSD_BC), lambda b, t: (row(b, t), OFF_SB // SSD_BC)),
            pl.BlockSpec((c, SSD_BC), lambda b, t: (row(b, t), OFF_SC // SSD_BC)),
            pl.BlockSpec((c, LANES), lambda b, t: (row(b, t), 0)),
            pl.BlockSpec((None, SUBLANES, SSD_CONV_DIM), lambda b, t: (b, 0, 0)),
            pl.BlockSpec((None, SSD_HEADS, SSD_STATE, SSD_HEADDIM), lambda b, t: (b, 0, 0, 0)),
            pl.BlockSpec(shift.shape, const2),
            pl.BlockSpec((CONV_W, SSD_CONV_DIM), const2),
            pl.BlockSpec((1, SSD_CONV_DIM), const2),
            pl.BlockSpec((1, LANES), const2),
            pl.BlockSpec((1, LANES), const2),
            pl.BlockSpec((1, SSD_INNER), const2),
            pl.BlockSpec((1, SSD_INNER), const2),
        ],
        out_specs=[
            pl.BlockSpec((c, SSD_INNER), lambda b, t: (row(b, t), 0)),
            pl.BlockSpec((None, SSD_HEADS, SSD_STATE, SSD_HEADDIM), lambda b, t: (b, 0, 0, 0)),
            pl.BlockSpec((None, SUBLANES, SSD_CONV_DIM), lambda b, t: (b, 0, 0)),
        ],
        out_shape=[
            jax.ShapeDtypeStruct((batch * seq, SSD_INNER), BF16),
            jax.ShapeDtypeStruct((batch, SSD_HEADS, SSD_STATE, SSD_HEADDIM), F32),
            jax.ShapeDtypeStruct((batch, SUBLANES, SSD_CONV_DIM), F32),
        ],
        scratch=[
            pltpu.VMEM((_stage_rows(c), SSD_CONV_DIM), BF16),
            pltpu.VMEM((SSD_GROUPS, SSD_STATE, SSD_GROUP_WIDTH), F32),
            pltpu.VMEM((c, SSD_INNER), BF16),
            pltpu.VMEM((c, SSD_INNER), F32),
        ],
        args=(proj, proj, proj, proj, proj, proj, dt_raw, conv0, s0, shift, p["ssd_conv_w"], p["ssd_conv_b"],
              p["ssd_dt_bias"], p["ssd_a_log"], p["ssd_d_exp"], p["ssd_norm"]))


def _xattn_body(x_ref, nx_ref, wq_ref, mk_ref, mv_ref, wo_ref, nf_ref, o_ref, *rest, final):
    if final:
        hn_ref, part_ref = rest
    else:
        hn_next_ref, hn_ref, part_ref = rest
    _rmsnorm_rows_to(hn_ref, x_ref, nx_ref, part_ref)
    q = _dot(hn_ref[...], wq_ref[...])
    scale = XATT_HD ** -0.5
    os = []
    for h in range(XATT_HEADS):
        sl = slice(h * XATT_HD, (h + 1) * XATT_HD)
        s = _dot_nt(q[:, sl].astype(BF16), mk_ref[:, sl].astype(BF16)) * scale
        mx = jnp.max(s, axis=-1, keepdims=True)
        pr = jnp.exp(s - mx)
        den = jnp.sum(pr, axis=-1, keepdims=True)
        os.append(_dot(pr.astype(BF16), mv_ref[:, sl].astype(BF16)) / den)
    o = jnp.concatenate(os, axis=-1).astype(BF16)
    o_ref[...] = x_ref[...] + _dot(o, wo_ref[...])
    if final:
        _rmsnorm_rows_to(o_ref, o_ref, nf_ref, part_ref)
    else:
        _rmsnorm_rows_to(hn_next_ref, o_ref, nf_ref, part_ref)


def _cross_attn(x, mem_k, mem_v, nx, wq, wo, nf, *, batch, seq, tm, final):
    d = x.shape[1]
    tm = min(tm, seq)
    nt = seq // tm
    const2 = lambda b, t: (0, 0)
    x_spec = pl.BlockSpec((tm, d), lambda b, t: (b * nt + t, 0))
    out_specs = x_spec if final else [x_spec, x_spec]
    out_shape = jax.ShapeDtypeStruct(x.shape, F32)
    if not final:
        out_shape = [out_shape, jax.ShapeDtypeStruct(x.shape, BF16)]
    return pl.pallas_call(
        functools.partial(_xattn_body, final=final),
        grid=(batch, nt),
        in_specs=[
            x_spec,
            pl.BlockSpec((1, d), const2),
            pl.BlockSpec((d, XATT_DIM), const2),
            pl.BlockSpec((None, N_MEM, XATT_DIM), lambda b, t: (b, 0, 0)),
            pl.BlockSpec((None, N_MEM, XATT_DIM), lambda b, t: (b, 0, 0)),
            pl.BlockSpec((XATT_DIM, d), const2),
            pl.BlockSpec((1, d), const2),
        ],
        out_specs=out_specs,
        out_shape=out_shape,
        scratch_shapes=[pltpu.VMEM((tm, d), BF16), pltpu.VMEM((tm, LANES), F32)],
        compiler_params=_params(("arbitrary", "arbitrary")),
        name="cross_attn",
    )(x, nx.reshape(1, d), wq, mem_k, mem_v, wo, nf.reshape(1, d))


def _pad_state_rows(s):
    return jnp.pad(s.astype(F32), ((0, 0), (SUBLANES - (CONV_W - 1), 0), (0, 0)))


def _rope_tables(pos):
    half = RET_DK // 2
    inv = ROPE_BASE ** (-jnp.arange(half, dtype=F32) / half)
    ang = pos[:, None] * inv[None, :]
    cos, sin = jnp.cos(ang), jnp.sin(ang)
    cq = jnp.concatenate([cos, cos], axis=-1)
    sq = jnp.concatenate([-sin, sin], axis=-1)
    ks = RET_DK ** -0.5
    return cq, sq, cq * ks, sq * ks


def _layer_params(l, norm_mix, w_in, lru_conv_w, lru_conv_b, lru_wa, lru_ba, lru_wx, lru_bx, lru_lambda, lru_norm,
                  ret_norm, ssd_conv_w, ssd_conv_b, ssd_dt_bias, ssd_a_log, ssd_d, ssd_norm, w_out, norm_xattn,
                  norm_mem, w_q_mem, w_k_mem, w_v_mem, w_o_mem):
    w_dt = jnp.pad(w_in[l, :, OFF_DT:], ((0, 0), (0, LANES - SSD_HEADS)))
    row = lambda v: v.reshape(1, -1).astype(F32)
    pad_lanes = lambda v: jnp.pad(v.astype(F32), (0, LANES - v.shape[0])).reshape(1, LANES)
    return dict(
        norm_mix=norm_mix[l], w_in=w_in, layer=l, w_dt=w_dt,
        lru_conv_w=lru_conv_w[l], lru_conv_b=row(lru_conv_b[l]), lru_wa=lru_wa[l].astype(BF16), lru_ba=row(lru_ba[l]),
        lru_wx=lru_wx[l].astype(BF16), lru_bx=row(lru_bx[l]), lru_lambda=row(lru_lambda[l]), lru_norm=row(lru_norm[l]),
        ret_norm=row(ret_norm[l]),
        ssd_conv_w=ssd_conv_w[l], ssd_conv_b=row(ssd_conv_b[l]), ssd_dt_bias=pad_lanes(ssd_dt_bias[l]),
        ssd_a_log=pad_lanes(ssd_a_log[l]), ssd_d_exp=row(jnp.repeat(ssd_d[l], SSD_HEADDIM)), ssd_norm=row(ssd_norm[l]),
        w_out=_cast_layer_bf16(w_out, l), norm_xattn=norm_xattn[l], norm_mem=norm_mem[l],
        w_q=_cast_layer_bf16(w_q_mem, l),
        w_kv=jnp.concatenate([_cast_layer_bf16(w_k_mem, l), _cast_layer_bf16(w_v_mem, l)], axis=1),
        w_o=_cast_layer_bf16(w_o_mem, l))


def _mixer_sublayer(x, hn, rope, lru_h0, lru_conv0, ret_s0, ssd_s0, ssd_conv0, p, *, batch, seq):
    proj, dt_raw = _in_proj(hn, p["w_in"], p["layer"], p["w_dt"], n=OFF_DT, tm=TM_IN_PROJ, tn=TN_PROJ)
    c = min(CHUNK, seq)
    (lru_out, lru_h1, lru_ct), (ssd_out, ssd_s1, ssd_ct) = _run_mixers(
        [_lru_parts(proj, _pad_state_rows(lru_conv0), lru_h0.reshape(batch, 1, LRU_WIDTH), p,
                    batch=batch, seq=seq, tb=c),
         _ssd_parts(proj, dt_raw, _pad_state_rows(ssd_conv0), ssd_s0, p, batch=batch, seq=seq, c=c)],
        batch=batch, nt=seq // c)
    c_ret = min(CHUNK_RET, seq)
    ((ret_out, ret_s1),) = _run_mixers([_ret_parts(proj, ret_s0, rope, p, batch=batch, seq=seq, c=c_ret)],
                                       batch=batch, nt=seq // c_ret)
    x1 = _out_proj(x, lru_out, ret_out, ssd_out, p["w_out"], tm=TM_OUT, tn=TN_OUT)
    tail = SUBLANES - (CONV_W - 1)
    states = (lru_h1.reshape(batch, LRU_WIDTH), lru_ct[:, tail:, :], ret_s1, ssd_s1, ssd_ct[:, tail:, :])
    return x1, states


def kernel(x_prompt, x_sample, mem_prompt, state_lru_h, state_lru_conv, state_ret, state_ssd, state_ssd_conv, cache_mem_k, cache_mem_v, norm_mix, w_in, lru_conv_w, lru_conv_b, lru_wa, lru_ba, lru_wx, lru_bx, lru_lambda, lru_norm, ret_norm, ssd_conv_w, ssd_conv_b, ssd_dt_bias, ssd_a_log, ssd_d, ssd_norm, w_out, norm_xattn, norm_mem, w_q_mem, w_k_mem, w_v_mem, w_o_mem, norm_final):
    bp, tp, d = x_prompt.shape
    bs, ts, _ = x_sample.shape
    depth = w_in.shape[0]
    assert tp % BF16_ROWS == 0 and ts % BF16_ROWS == 0
    rope_p = _rope_tables(jnp.arange(tp, dtype=F32))
    rope_s = _rope_tables(PAST_LEN + jnp.arange(ts, dtype=F32))
    z_lru_h = jnp.zeros((bp, LRU_WIDTH), F32)
    z_lru_conv = jnp.zeros((bp, CONV_W - 1, LRU_WIDTH), F32)
    z_ret = jnp.zeros((bp, RET_HEADS, RET_DK, RET_DV), F32)
    z_ssd = jnp.zeros((bp, SSD_HEADS, SSD_STATE, SSD_HEADDIM), F32)
    z_ssd_conv = jnp.zeros((bp, CONV_W - 1, SSD_CONV_DIM), F32)

    xp = x_prompt.reshape(bp * tp, d)
    xs = x_sample.reshape(bs * ts, d)
    mem = mem_prompt.reshape(bp * N_MEM, d)
    w_in = w_in.astype(BF16)
    hn_p = _rmsnorm_bf16(xp, norm_mix[0], tm=TM_XATT)
    hn_s = _rmsnorm_bf16(xs, norm_mix[0], tm=TM_XATT)
    st_p = [[] for _ in range(5)]
    st_s = [[] for _ in range(5)]
    mk_p, mv_p = [], []
    for l in range(depth):
        p = _layer_params(l, norm_mix, w_in, lru_conv_w, lru_conv_b, lru_wa, lru_ba, lru_wx, lru_bx, lru_lambda,
                          lru_norm, ret_norm, ssd_conv_w, ssd_conv_b, ssd_dt_bias, ssd_a_log, ssd_d, ssd_norm, w_out,
                          norm_xattn, norm_mem, w_q_mem, w_k_mem, w_v_mem, w_o_mem)
        final = l == depth - 1
        next_gain = norm_final if final else norm_mix[l + 1]
        xp, st = _mixer_sublayer(xp, hn_p, rope_p, z_lru_h, z_lru_conv, z_ret, z_ssd, z_ssd_conv, p, batch=bp, seq=tp)
        mkv = _norm_matmul(mem, p["norm_mem"], p["w_kv"], out_dtype=F32, tm=TM_PROJ, tn=TN_PROJ)
        mk = mkv[:, :XATT_DIM].reshape(bp, N_MEM, XATT_DIM)
        mv = mkv[:, XATT_DIM:].reshape(bp, N_MEM, XATT_DIM)
        res = _cross_attn(xp, mk, mv, p["norm_xattn"], p["w_q"], p["w_o"], next_gain, batch=bp, seq=tp, tm=TM_XATT,
                          final=final)
        xp, hn_p = (res, None) if final else res
        for acc, v in zip(st_p, st):
            acc.append(v)
        mk_p.append(mk.reshape(bp, N_MEM, XATT_HEADS, XATT_HD))
        mv_p.append(mv.reshape(bp, N_MEM, XATT_HEADS, XATT_HD))
        xs, st = _mixer_sublayer(xs, hn_s, rope_s, state_lru_h[l], state_lru_conv[l], state_ret[l], state_ssd[l],
                                 state_ssd_conv[l], p, batch=bs, seq=ts)
        res = _cross_attn(xs, cache_mem_k[l].reshape(bs, N_MEM, XATT_DIM), cache_mem_v[l].reshape(bs, N_MEM, XATT_DIM),
                          p["norm_xattn"], p["w_q"], p["w_o"], next_gain, batch=bs, seq=ts, tm=TM_XATT, final=final)
        xs, hn_s = (res, None) if final else res
        for acc, v in zip(st_s, st):
            acc.append(v)

    y_prompt = xp.reshape(bp, tp, d)
    y_sample = xs.reshape(bs, ts, d)
    return (y_prompt, y_sample,
            jnp.stack(st_p[0]), jnp.stack(st_p[1]), jnp.stack(st_p[2]), jnp.stack(st_p[3]), jnp.stack(st_p[4]),
            jnp.stack(mk_p), jnp.stack(mv_p),
            jnp.stack(st_s[0]), jnp.stack(st_s[1]), jnp.stack(st_s[2]), jnp.stack(st_s[3]), jnp.stack(st_s[4]))
```

```python
import functools
import math
from typing import Callable, NamedTuple

import numpy as np
import jax
import jax.numpy as jnp
from jax import lax
from jax.experimental import pallas as pl
from jax.experimental.pallas import tpu as pltpu

F32 = jnp.float32
BF16 = jnp.bfloat16

EPS = 1e-6
CONV_W = 4
PAST_LEN = 4096
N_MEM = 256
LRU_WIDTH = 1024
LRU_BLOCKS = 4
LRU_BLOCK = LRU_WIDTH // LRU_BLOCKS
LRU_C = 8.0
RET_HEADS = 4
RET_DK = 128
RET_DV = 256
RET_QK = RET_HEADS * RET_DK
RET_V = RET_HEADS * RET_DV
ROPE_BASE = 10000.0
SSD_INNER = 2048
SSD_HEADDIM = 64
SSD_HEADS = SSD_INNER // SSD_HEADDIM
SSD_STATE = 128
SSD_GROUPS = 4
SSD_GROUP_HEADS = SSD_HEADS // SSD_GROUPS
SSD_GROUP_WIDTH = SSD_GROUP_HEADS * SSD_HEADDIM
SSD_BC = SSD_GROUPS * SSD_STATE
SSD_CONV_DIM = SSD_INNER + 2 * SSD_BC
XATT_HEADS = 4
XATT_HD = 128
XATT_DIM = XATT_HEADS * XATT_HD

SUBLANES = 8
LANES = 128
BF16_ROWS = 2 * SUBLANES
VMEM_LIMIT_BYTES = 56 * 1024 * 1024

OFF_LX, OFF_LG, OFF_RQ, OFF_RK, OFF_RV, OFF_RG, OFF_SZ, OFF_SX = 0, 1024, 2048, 2560, 3072, 4096, 5120, 7168
OFF_SB = OFF_SX + SSD_INNER
OFF_SC = OFF_SB + SSD_BC
OFF_DT = OFF_SX + SSD_CONV_DIM

RET_LOG_GAMMA = tuple(math.log1p(-(2.0 ** (-5.0 - h))) for h in range(RET_HEADS))
LOG2E = math.log2(math.e)

TM_PROJ, TN_PROJ = 512, 1024
TM_IN_PROJ = 1024
TM_OUT, TN_OUT = 1024, 1024
CAST_ROWS = 256
TM_XATT = 256
CHUNK = 128
CHUNK_RET = 256


def _params(semantics):
    return pltpu.CompilerParams(dimension_semantics=semantics, vmem_limit_bytes=VMEM_LIMIT_BYTES)


def _sigmoid(x):
    return 1.0 / (1.0 + jnp.exp(-x))


def _silu(x):
    return x * _sigmoid(x)


def _softplus(x):
    return jnp.maximum(x, 0.0) + jnp.log1p(jnp.exp(-jnp.abs(x)))


def _dot(a, b):
    return jnp.dot(a, b, preferred_element_type=F32)


def _dot_nt(a, b):
    return lax.dot_general(a, b, (((1,), (1,)), ((), ())), preferred_element_type=F32)


def _dot_tn(a, b):
    return lax.dot_general(a, b, (((0,), (0,)), ((), ())), preferred_element_type=F32)


def _rmsnorm_rows_to(dst_ref, src_ref, gain_ref, part_ref, unroll=False):
    tm, d = src_ref.shape
    rc = BF16_ROWS
    lane_tiles = d // LANES

    def sum_squares(r, carry):
        rows = pl.ds(pl.multiple_of(r * rc, rc), rc)
        accs = [None] * min(4, lane_tiles)
        for j in range(lane_tiles):
            v = src_ref[rows, j * LANES:(j + 1) * LANES].astype(F32)
            k = j % len(accs)
            accs[k] = v * v if accs[k] is None else accs[k] + v * v
        while len(accs) > 1:
            accs = [a + b for a, b in zip(accs[0::2], accs[1::2])]
        part_ref[rows, :] = accs[0]
        return carry

    lax.fori_loop(0, tm // rc, sum_squares, 0, unroll=unroll)
    ms = jnp.sum(part_ref[...], axis=-1, keepdims=True) * (1.0 / d)
    part_ref[...] = jnp.broadcast_to(lax.rsqrt(ms + EPS), part_ref.shape)

    def scale(r, carry):
        rows = pl.ds(pl.multiple_of(r * rc, rc), rc)
        rs = part_ref[rows, :]
        for j in range(lane_tiles):
            sl = slice(j * LANES, (j + 1) * LANES)
            dst_ref[rows, sl] = (src_ref[rows, sl].astype(F32) * rs * gain_ref[:, sl]).astype(dst_ref.dtype)
        return carry

    lax.fori_loop(0, tm // rc, scale, 0, unroll=unroll)


def _cast_body(w_ref, o_ref):
    o_ref[...] = w_ref[...].astype(o_ref.dtype)


def _cast_layer_bf16(w_stack, l):
    _, r, c = w_stack.shape
    rows = min(CAST_ROWS, r)
    assert r % rows == 0
    return pl.pallas_call(
        _cast_body,
        grid=(r // rows,),
        in_specs=[pl.BlockSpec((None, rows, c), lambda i: (l, i, 0))],
        out_specs=pl.BlockSpec((rows, c), lambda i: (i, 0)),
        out_shape=jax.ShapeDtypeStruct((r, c), BF16),
        compiler_params=_params(("arbitrary",)),
        name="cast_bf16",
    )(w_stack)


def _cast_xattn_body(wq_ref, wk_ref, wv_ref, wo_ref, oq_ref, okv_ref, oo_ref):
    oq_ref[...] = wq_ref[...].astype(oq_ref.dtype)
    okv_ref[:, 0:XATT_DIM] = wk_ref[...].astype(okv_ref.dtype)
    okv_ref[:, XATT_DIM:2 * XATT_DIM] = wv_ref[...].astype(okv_ref.dtype)
    oo_ref[...] = wo_ref[...].astype(oo_ref.dtype)


def _cast_xattn_weights(w_q, w_k, w_v, w_o, l):
    _, d, n = w_q.shape
    steps = d // CAST_ROWS
    assert d % CAST_ROWS == 0 and n % steps == 0 and (n // steps) % BF16_ROWS == 0
    in_rows = lambda i: (l, i, 0)
    return pl.pallas_call(
        _cast_xattn_body,
        grid=(steps,),
        in_specs=[pl.BlockSpec((None, CAST_ROWS, n), in_rows), pl.BlockSpec((None, CAST_ROWS, n), in_rows),
                  pl.BlockSpec((None, CAST_ROWS, n), in_rows), pl.BlockSpec((None, n // steps, d), in_rows)],
        out_specs=[pl.BlockSpec((CAST_ROWS, n), lambda i: (i, 0)), pl.BlockSpec((CAST_ROWS, 2 * n), lambda i: (i, 0)),
                   pl.BlockSpec((n // steps, d), lambda i: (i, 0))],
        out_shape=[jax.ShapeDtypeStruct((d, n), BF16), jax.ShapeDtypeStruct((d, 2 * n), BF16),
                   jax.ShapeDtypeStruct((n, d), BF16)],
        compiler_params=_params(("arbitrary",)),
        name="cast_xattn",
    )(w_q, w_k, w_v, w_o)


def _norm_matmul_body(x_ref, g_ref, w_ref, o_ref, hn_ref, part_ref):
    @pl.when(pl.program_id(1) == 0)
    def _():
        _rmsnorm_rows_to(hn_ref, x_ref, g_ref, part_ref)

    o_ref[...] = _dot(hn_ref[...], w_ref[...]).astype(o_ref.dtype)


def _norm_matmul(x, g, w, *, out_dtype, tm, tn):
    m, k = x.shape
    n = w.shape[1]
    tm = min(tm, m)
    tn = min(tn, n)
    assert m % tm == 0 and n % tn == 0
    return pl.pallas_call(
        _norm_matmul_body,
        grid=(m // tm, n // tn),
        in_specs=[
            pl.BlockSpec((tm, k), lambda i, j: (i, 0)),
            pl.BlockSpec((1, k), lambda i, j: (0, 0)),
            pl.BlockSpec((k, tn), lambda i, j: (0, j)),
        ],
        out_specs=pl.BlockSpec((tm, tn), lambda i, j: (i, j)),
        out_shape=jax.ShapeDtypeStruct((m, n), out_dtype),
        scratch_shapes=[pltpu.VMEM((tm, k), BF16), pltpu.VMEM((tm, LANES), F32)],
        compiler_params=_params(("arbitrary", "arbitrary")),
        name="norm_matmul",
    )(x, g.reshape(1, k), w)


def _rmsnorm_body(x_ref, g_ref, o_ref, part_ref):
    _rmsnorm_rows_to(o_ref, x_ref, g_ref, part_ref)


def _rmsnorm_bf16(x, g, *, tm):
    m, k = x.shape
    tm = min(tm, m)
    assert m % tm == 0
    return pl.pallas_call(
        _rmsnorm_body,
        grid=(m // tm,),
        in_specs=[pl.BlockSpec((tm, k), lambda i: (i, 0)), pl.BlockSpec((1, k), lambda i: (0, 0))],
        out_specs=pl.BlockSpec((tm, k), lambda i: (i, 0)),
        out_shape=jax.ShapeDtypeStruct((m, k), BF16),
        scratch_shapes=[pltpu.VMEM((tm, LANES), F32)],
        compiler_params=_params(("arbitrary",)),
        name="rmsnorm_bf16",
    )(x, g.reshape(1, k))


def _in_proj_body(hn_ref, w_ref, w2_ref, o_ref, o2_ref):
    @pl.when(pl.program_id(1) == 0)
    def _():
        o2_ref[...] = _dot(hn_ref[...], w2_ref[...])

    o_ref[...] = _dot(hn_ref[...], w_ref[...]).astype(o_ref.dtype)


def _in_proj(hn, w, l, w2, *, n, tm, tn):
    m, k = hn.shape
    n2 = w2.shape[1]
    tm = min(tm, m)
    assert m % tm == 0 and n % tn == 0 and n <= w.shape[2]
    return pl.pallas_call(
        _in_proj_body,
        grid=(m // tm, n // tn),
        in_specs=[
            pl.BlockSpec((tm, k), lambda i, j: (i, 0)),
            pl.BlockSpec((None, k, tn), lambda i, j: (l, 0, j)),
            pl.BlockSpec((k, n2), lambda i, j: (0, 0)),
        ],
        out_specs=[pl.BlockSpec((tm, tn), lambda i, j: (i, j)), pl.BlockSpec((tm, n2), lambda i, j: (i, 0))],
        out_shape=[jax.ShapeDtypeStruct((m, n), BF16), jax.ShapeDtypeStruct((m, n2), F32)],
        compiler_params=_params(("arbitrary", "arbitrary")),
        name="in_proj",
    )(hn, w, w2)


def _out_proj_body(x_ref, a1_ref, a2_ref, a3_ref, w1_ref, w2_ref, w3_ref, o_ref):
    acc = _dot(a1_ref[...], w1_ref[...])
    acc = acc + _dot(a2_ref[...], w2_ref[...])
    acc = acc + _dot(a3_ref[...], w3_ref[...])
    o_ref[...] = x_ref[...] + acc


def _out_proj(x, lru_out, ret_out, ssd_out, w_out_bf16, *, tm, tn):
    m, n = x.shape
    tm = min(tm, m)
    tn = min(tn, n)
    assert m % tm == 0 and n % tn == 0
    return pl.pallas_call(
        _out_proj_body,
        grid=(m // tm, n // tn),
        in_specs=[
            pl.BlockSpec((tm, tn), lambda i, j: (i, j)),
            pl.BlockSpec((tm, LRU_WIDTH), lambda i, j: (i, 0)),
            pl.BlockSpec((tm, RET_V), lambda i, j: (i, 0)),
            pl.BlockSpec((tm, SSD_INNER), lambda i, j: (i, 0)),
            pl.BlockSpec((LRU_WIDTH, tn), lambda i, j: (0, j)),
            pl.BlockSpec((RET_V, tn), lambda i, j: (1, j)),
            pl.BlockSpec((SSD_INNER, tn), lambda i, j: (1, j)),
        ],
        out_specs=pl.BlockSpec((tm, tn), lambda i, j: (i, j)),
        out_shape=jax.ShapeDtypeStruct((m, n), F32),
        compiler_params=_params(("arbitrary", "arbitrary")),
        name="out_proj",
    )(x, lru_out, ret_out, ssd_out, w_out_bf16, w_out_bf16, w_out_bf16)


HIST_ROWS = BF16_ROWS


def _stage_rows(c):
    return -(-(HIST_ROWS + c) // LANES) * LANES


def _shift_matrix(c):
    s = np.zeros((CONV_W - 1, c, _stage_rows(c)), np.float32)
    for k in range(CONV_W - 1):
        back = k + 1
        for t in range(c):
            if t >= back:
                s[k, t, HIST_ROWS + t - back] = 1.0
            else:
                j = (CONV_W - 1) + t - back
                s[k, t, j] = s[k, t, 3 + j] = s[k, t, SUBLANES + j] = 1.0
    return jnp.asarray(s.reshape((CONV_W - 1) * c, -1), dtype=BF16)


def _history_tile(tail8):
    r = pltpu.roll(tail8, CONV_W - 1, 0)
    hi = r.astype(BF16).astype(F32)
    rest = r - hi
    mid = rest.astype(BF16).astype(F32)
    lo = rest - mid
    rows = lax.broadcasted_iota(jnp.int32, r.shape, 0)
    top = jnp.where(rows < 3, hi, jnp.where(rows < 6, pltpu.roll(mid, CONV_W - 1, 0), 0.0))
    bot = jnp.where(rows < 3, lo, 0.0)
    return jnp.concatenate([top, bot], axis=0).astype(BF16)


def _conv_from_stage(stage, shift_ref, x_f32, c, cw_ref, cb_ref):
    sh = _dot(shift_ref[...], stage[...])
    acc = sh[2 * c:3 * c] * cw_ref[0:1, :]
    acc = acc + sh[c:2 * c] * cw_ref[1:2, :]
    acc = acc + sh[0:c] * cw_ref[2:3, :]
    acc = acc + x_f32 * cw_ref[3:4, :]
    return cb_ref[...] + acc


class MixerParts(NamedTuple):
    body: Callable
    args: tuple
    in_specs: list
    out_specs: list
    out_shape: list
    scratch: list
    carried: dict = {}


def _mixers_body(*refs, parts, n_carried):
    n_in = sum(len(p.args) for p in parts)
    n_out = sum(len(p.out_specs) for p in parts)
    ins = refs[:n_in]
    outs = refs[n_in + n_carried:n_in + n_carried + n_out]
    scr = refs[n_in + n_carried + n_out:]
    i = o = s = 0
    for p in parts:
        ni, no, ns = len(p.args), len(p.out_specs), len(p.scratch)
        p.body(*ins[i:i + ni], *outs[o:o + no], *scr[s:s + ns])
        i, o, s = i + ni, o + no, s + ns


def _run_mixers(parts, *, batch, nt):
    flat = lambda field: [x for p in parts for x in getattr(p, field)]
    bodies = tuple(p._replace(args=tuple(None for _ in p.args), out_shape=[], carried={}) for p in parts)
    args, in_specs = flat("args"), flat("in_specs")
    aliases, o = {}, 0
    for p in parts:
        for k, buf in p.carried.items():
            aliases[len(args)] = o + k
            args.append(buf)
            in_specs.append(pl.BlockSpec(memory_space=pl.ANY))
        o += len(p.out_specs)
    outs = pl.pallas_call(
        functools.partial(_mixers_body, parts=bodies, n_carried=len(aliases)),
        grid=(batch, nt),
        in_specs=in_specs,
        out_specs=flat("out_specs"),
        out_shape=flat("out_shape"),
        scratch_shapes=flat("scratch"),
        input_output_aliases=aliases,
        compiler_params=_params(("arbitrary", "arbitrary")),
        name="mixers",
    )(*args)
    res, o = [], 0
    for p in parts:
        res.append(outs[o:o + len(p.out_specs)])
        o += len(p.out_specs)
    return res


def _lru_body(lxg_ref, conv0_ref, h0_ref, shift_ref, cw_ref, cb_ref, wa_ref, ba_ref, wx_ref, bx_ref, lam_ref,
              nrm_ref, out_ref, hlast_ref, ctail_ref, stage, hcar, a_scr, b_scr):
    tb = lxg_ref.shape[0]
    lx_ref = lxg_ref.at[:, 0:LRU_WIDTH]
    lg_ref = lxg_ref.at[:, LRU_WIDTH:2 * LRU_WIDTH]

    @pl.when(pl.program_id(1) == 0)
    def _():
        stage[...] = jnp.zeros(stage.shape, stage.dtype)
        stage[0:HIST_ROWS, :] = _history_tile(conv0_ref[...])
        hcar[...] = h0_ref[...]

    stage[pl.ds(HIST_ROWS, tb), :] = lx_ref[...]
    x_in = lx_ref[...].astype(F32)
    xc = _conv_from_stage(stage, shift_ref, x_in, tb, cw_ref, cb_ref)
    tail = x_in[tb - SUBLANES:tb]
    stage[0:HIST_ROWS, :] = _history_tile(tail)
    ctail_ref[...] = tail

    xb = xc.astype(BF16)
    ra, ri = [], []
    for n in range(LRU_BLOCKS):
        xn = xb[:, n * LRU_BLOCK:(n + 1) * LRU_BLOCK]
        ra.append(_dot(xn, wa_ref[n]))
        ri.append(_dot(xn, wx_ref[n]))
    r = _sigmoid(jnp.concatenate(ra, axis=-1) + ba_ref[...])
    i = _sigmoid(jnp.concatenate(ri, axis=-1) + bx_ref[...])
    log_a = r * ((-LRU_C) * _softplus(-lam_ref[...]))
    a = jnp.exp(log_a)
    th = jnp.tanh(log_a)
    b = jnp.sqrt(-2.0 * th / (1.0 - th)) * (i * xc)

    ng = tb // SUBLANES
    for j in range(LRU_WIDTH // LANES):
        a_scr[j] = a[:, j * LANES:(j + 1) * LANES]
        b_scr[j] = b[:, j * LANES:(j + 1) * LANES]
    for j in range(LRU_WIDTH // LANES):
        pa = a_scr[j, pl.ds(0, ng, stride=SUBLANES), :]
        pb = b_scr[j, pl.ds(0, ng, stride=SUBLANES), :]
        for r in range(1, SUBLANES):
            view = pl.ds(r, ng, stride=SUBLANES)
            ar = a_scr[j, view, :]
            pb = ar * pb + b_scr[j, view, :]
            pa = ar * pa
            a_scr[j, view, :] = pa
            b_scr[j, view, :] = pb
    a3 = jnp.concatenate([a_scr[j] for j in range(LRU_WIDTH // LANES)], axis=-1).reshape(ng, SUBLANES, LRU_WIDTH)
    b3 = jnp.concatenate([b_scr[j] for j in range(LRU_WIDTH // LANES)], axis=-1).reshape(ng, SUBLANES, LRU_WIDTH)
    hprev = hcar[...]
    hs = []
    for g in range(ng):
        hg = b3[g] + a3[g] * hprev
        hprev = hg[SUBLANES - 1:SUBLANES]
        hs.append(hg)
    h = jnp.concatenate(hs, axis=0)
    hcar[...] = hprev
    hlast_ref[...] = hprev

    ms = jnp.mean(h * h, axis=-1, keepdims=True)
    out = h * lax.rsqrt(ms + EPS) * nrm_ref[...]
    out_ref[...] = (out * _silu(lg_ref[...].astype(F32))).astype(out_ref.dtype)


def _lru_parts(proj, conv0, h0, p, *, batch, seq, tb):
    assert seq % tb == 0 and OFF_LG == OFF_LX + LRU_WIDTH and OFF_LX % (2 * LRU_WIDTH) == 0
    nt = seq // tb
    w = LRU_WIDTH
    row = lambda b, t: b * nt + t
    const2 = lambda b, t: (0, 0)
    shift = _shift_matrix(tb)
    return MixerParts(
        body=_lru_body,
        in_specs=[
            pl.BlockSpec((tb, 2 * w), lambda b, t: (row(b, t), OFF_LX // (2 * w))),
            pl.BlockSpec((None, SUBLANES, w), lambda b, t: (b, 0, 0)),
            pl.BlockSpec((None, 1, w), lambda b, t: (b, 0, 0)),
            pl.BlockSpec(shift.shape, const2),
            pl.BlockSpec((CONV_W, w), const2),
            pl.BlockSpec((1, w), const2),
            pl.BlockSpec((LRU_BLOCKS, LRU_BLOCK, LRU_BLOCK), lambda b, t: (0, 0, 0)),
            pl.BlockSpec((1, w), const2),
            pl.BlockSpec((LRU_BLOCKS, LRU_BLOCK, LRU_BLOCK), lambda b, t: (0, 0, 0)),
            pl.BlockSpec((1, w), const2),
            pl.BlockSpec((1, w), const2),
            pl.BlockSpec((1, w), const2),
        ],
        out_specs=[
            pl.BlockSpec((tb, w), lambda b, t: (row(b, t), 0)),
            pl.BlockSpec((None, 1, w), lambda b, t: (b, 0, 0)),
            pl.BlockSpec((None, SUBLANES, w), lambda b, t: (b, 0, 0)),
        ],
        out_shape=[
            jax.ShapeDtypeStruct((batch * seq, w), BF16),
            jax.ShapeDtypeStruct((batch, 1, w), F32),
            jax.ShapeDtypeStruct((batch, SUBLANES, w), F32),
        ],
        scratch=[pltpu.VMEM((_stage_rows(tb), w), BF16), pltpu.VMEM((1, w), F32),
                 pltpu.VMEM((w // LANES, tb, LANES), F32), pltpu.VMEM((w // LANES, tb, LANES), F32)],
        args=(proj, conv0, h0, shift, p["lru_conv_w"], p["lru_conv_b"], p["lru_wa"], p["lru_ba"], p["lru_wx"],
              p["lru_bx"], p["lru_lambda"], p["lru_norm"]))


def _ret_body(qkvg_ref, cq_ref, sq_ref, ck_ref, sk_ref, s0_ref, nrm_ref,
              out_ref, snew_ref, state, decay_tab, eg_tab, wt_tab):
    c = qkvg_ref.shape[0]
    q_ref = qkvg_ref.at[:, OFF_RQ:OFF_RQ + RET_QK]
    k_ref = qkvg_ref.at[:, OFF_RK:OFF_RK + RET_QK]
    v_ref = qkvg_ref.at[:, OFF_RV:OFF_RV + RET_V]
    g_ref = qkvg_ref.at[:, OFF_RG:OFF_RG + RET_V]

    @pl.when(jnp.logical_and(pl.program_id(0) == 0, pl.program_id(1) == 0))
    def _():
        ii = lax.broadcasted_iota(jnp.int32, (c, c), 0)
        jj = lax.broadcasted_iota(jnp.int32, (c, c), 1)
        dpos = jnp.maximum(ii - jj, 0).astype(F32)
        rpos = lax.broadcasted_iota(jnp.int32, (c, RET_DK), 0).astype(F32)
        for h in range(RET_HEADS):
            lg = RET_LOG_GAMMA[h]
            decay_tab[h] = jnp.where(ii >= jj, jnp.exp(dpos * lg), 0.0)
            eg_tab[h] = jnp.exp((rpos + 1.0) * lg)
            wt_tab[h] = jnp.exp((c - 1.0 - rpos) * lg)

    @pl.when(pl.program_id(1) == 0)
    def _():
        state[...] = s0_ref[...]

    cq, sq, ck, sk = cq_ref[...], sq_ref[...], ck_ref[...], sk_ref[...]
    outs = []
    for h in range(RET_HEADS):
        decay, e_g, w_tail = decay_tab[h], eg_tab[h], wt_tab[h]
        a_chunk = math.exp(c * RET_LOG_GAMMA[h])
        qh = q_ref[:, h * RET_DK:(h + 1) * RET_DK].astype(F32)
        kh = k_ref[:, h * RET_DK:(h + 1) * RET_DK].astype(F32)
        qr = qh * cq + pltpu.roll(qh, RET_DK // 2, 1) * sq
        kr = kh * ck + pltpu.roll(kh, RET_DK // 2, 1) * sk
        vh = v_ref[:, h * RET_DV:(h + 1) * RET_DV]
        sh = state[h]
        scores = _dot_nt(qr.astype(BF16), kr.astype(BF16)) * decay
        y = _dot(scores.astype(BF16), vh) + _dot((qr * e_g).astype(BF16), sh.astype(BF16))
        state[h] = a_chunk * sh + _dot_tn((kr * w_tail).astype(BF16), vh)
        mu = jnp.mean(y, axis=-1, keepdims=True)
        d = y - mu
        var = jnp.mean(d * d, axis=-1, keepdims=True)
        outs.append(d * lax.rsqrt(var + EPS))
    ro = jnp.concatenate(outs, axis=-1) * nrm_ref[...]
    out_ref[...] = (ro * _silu(g_ref[...].astype(F32))).astype(out_ref.dtype)
    snew_ref[...] = state[...]


def _ret_parts(proj, s0, rope, p, states, layer, *, batch, seq, c):
    assert seq % c == 0 and states.shape[1:] == (batch, RET_HEADS, RET_DK, RET_DV)
    nt = seq // c
    row = lambda b, t: b * nt + t
    tab = pl.BlockSpec((c, RET_DK), lambda b, t: (t, 0))
    return MixerParts(
        body=_ret_body,
        in_specs=[
            pl.BlockSpec((c, OFF_SZ), lambda b, t: (row(b, t), 0)),
            tab, tab, tab, tab,
            pl.BlockSpec((None, RET_HEADS, RET_DK, RET_DV), lambda b, t: (b, 0, 0, 0)),
            pl.BlockSpec((1, RET_V), lambda b, t: (0, 0)),
        ],
        out_specs=[
            pl.BlockSpec((c, RET_V), lambda b, t: (row(b, t), 0)),
            pl.BlockSpec((None, None, RET_HEADS, RET_DK, RET_DV), lambda b, t: (layer, b, 0, 0, 0)),
        ],
        out_shape=[
            jax.ShapeDtypeStruct((batch * seq, RET_V), BF16),
            jax.ShapeDtypeStruct(states.shape, F32),
        ],
        carried={1: states},
        scratch=[
            pltpu.VMEM((RET_HEADS, RET_DK, RET_DV), F32),
            pltpu.VMEM((RET_HEADS, c, c), F32),
            pltpu.VMEM((RET_HEADS, c, RET_DK), F32),
            pltpu.VMEM((RET_HEADS, c, RET_DK), F32),
        ],
        args=(proj, rope[0], rope[1], rope[2], rope[3], s0, p["ret_norm"]))


def _cumsum_rows(x):
    n = x.shape[0]
    rows = lax.broadcasted_iota(jnp.int32, x.shape, 0)
    s = 1
    while s < n:
        x = x + jnp.where(rows >= s, pltpu.roll(x, s, 0), 0.0)
        s *= 2
    return x


def _ssd_body(zxbc_ref, dt_ref, conv0_ref, s0_ref, shift_ref, cw_ref, cb_ref,
              dtb_ref, alog_ref, dexp_ref, nrm_ref, out_ref, snew_ref, ctail_ref, stage, state, xw_scr, u_scr, *,
              n_steps):
    c = zxbc_ref.shape[0]
    gh, hd, gw = SSD_GROUP_HEADS, SSD_HEADDIM, SSD_GROUP_WIDTH
    conv_in = lambda j: zxbc_ref[:, SSD_INNER + j * LANES:SSD_INNER + (j + 1) * LANES]

    @pl.when(pl.program_id(1) == 0)
    def _():
        stage[...] = jnp.zeros(stage.shape, stage.dtype)
        stage[0:HIST_ROWS, :] = _history_tile(conv0_ref[...])
        for g in range(SSD_GROUPS):
            state[g] = jnp.concatenate([s0_ref[g * gh + hh] for hh in range(gh)], axis=-1)

    stage[pl.ds(HIST_ROWS, c), :] = zxbc_ref[:, SSD_INNER:SSD_INNER + SSD_CONV_DIM]
    sh = _dot(shift_ref[...], stage[...])
    tail = stage[pl.ds(HIST_ROWS + c - BF16_ROWS, BF16_ROWS), :].astype(F32)[SUBLANES:]
    stage[0:HIST_ROWS, :] = _history_tile(tail)
    ctail_ref[...] = tail

    def conv_tile(x_bf16, j):
        sl = slice(j * LANES, (j + 1) * LANES)
        acc = sh[2 * c:3 * c, sl] * cw_ref[0:1, sl]
        acc = acc + sh[c:2 * c, sl] * cw_ref[1:2, sl]
        acc = acc + sh[0:c, sl] * cw_ref[2:3, sl]
        acc = acc + x_bf16.astype(F32) * cw_ref[3:4, sl]
        return _silu(cb_ref[:, sl] + acc)

    x_tiles = SSD_INNER // LANES
    bc_tiles = SSD_BC // LANES

    dt = _softplus(dt_ref[...] + dtb_ref[...])
    a_neg = -jnp.exp(alog_ref[...])
    g2 = _cumsum_rows(dt * a_neg) * LOG2E
    g2_t = g2.T
    lane_lo = lax.broadcasted_iota(jnp.int32, (c, LANES), 1) < hd
    ii = lax.broadcasted_iota(jnp.int32, (c, c), 0)
    jj = lax.broadcasted_iota(jnp.int32, (c, c), 1)
    tri = ii >= jj

    bgs, cbs, y_inter = [], [], []
    for g in range(SSD_GROUPS):
        bg = conv_tile(conv_in(x_tiles + g), x_tiles + g).astype(BF16)
        cg = conv_tile(conv_in(x_tiles + bc_tiles + g), x_tiles + bc_tiles + g).astype(BF16)
        bgs.append(bg)
        cbs.append(jnp.where(tri, _dot_nt(cg, bg), 0.0))
        y_inter.append(_dot(cg, state[g].astype(BF16)))

    ssq = None
    a_tiles = []
    for p in range(SSD_HEADS // 2):
        g, pp = divmod(p, gh // 2)
        sl = slice(p * LANES, (p + 1) * LANES)
        xs = conv_tile(conv_in(p), p)
        gcol = [jnp.broadcast_to(g2[:, h:h + 1], (c, LANES)) for h in (2 * p, 2 * p + 1)]
        dcol = [jnp.broadcast_to(dt[:, h:h + 1], (c, LANES)) for h in (2 * p, 2 * p + 1)]
        g_x = jnp.where(lane_lo, gcol[0], gcol[1])
        xdt = xs * jnp.where(lane_lo, dcol[0], dcol[1])
        g_last = g_x[c - 1:c, :]
        xw_scr[:, sl] = (xdt * jnp.exp2(g_last - g_x)).astype(BF16)
        a_tiles.append(jnp.exp2(g_last))
        ms = []
        for k, h in enumerate((2 * p, 2 * p + 1)):
            decay = jnp.exp2(jnp.minimum(gcol[k][:, 0:c] - g2_t[h:h + 1, :], 0.0))
            ms.append((cbs[g] * decay).astype(BF16))
        rhs = jnp.concatenate([jnp.where(lane_lo, xdt, 0.0), jnp.where(lane_lo, 0.0, xdt)], axis=0).astype(BF16)
        y = _dot(jnp.concatenate(ms, axis=1), rhs) + jnp.exp2(g_x) * y_inter[g][:, pp * LANES:(pp + 1) * LANES]
        so = y + dexp_ref[:, sl] * xs
        u = so * _silu(zxbc_ref[:, sl].astype(F32))
        ssq = u * u if ssq is None else ssq + u * u
        u_scr[:, sl] = u

    for g in range(SSD_GROUPS):
        gsl = slice(g * gw, (g + 1) * gw)
        a_row = jnp.concatenate(a_tiles[g * (gh // 2):(g + 1) * (gh // 2)], axis=-1)
        state[g] = a_row * state[g] + _dot_tn(bgs[g], xw_scr[:, gsl])

    rs = lax.rsqrt(jnp.sum(ssq, axis=-1, keepdims=True) * (1.0 / SSD_INNER) + EPS)
    out_ref[...] = (u_scr[...] * rs * nrm_ref[...]).astype(out_ref.dtype)

    @pl.when(pl.program_id(1) == n_steps - 1)
    def _():
        for g in range(SSD_GROUPS):
            sg = state[g]
            for hh in range(gh):
                snew_ref[g * gh + hh] = sg[:, hh * hd:(hh + 1) * hd]


def _ssd_parts(proj, dt_raw, conv0, s0, p, states, layer, *, batch, seq, c):
    assert seq % c == 0 and states.shape[1:] == (batch, SSD_HEADS, SSD_STATE, SSD_HEADDIM)
    nt = seq // c
    row = lambda b, t: b * nt + t
    const2 = lambda b, t: (0, 0)
    width = SSD_INNER + SSD_CONV_DIM
    assert OFF_SZ % width == 0 and OFF_SX == OFF_SZ + SSD_INNER
    shift = _shift_matrix(c)
    return MixerParts(
        body=functools.partial(_ssd_body, n_steps=nt),
        in_specs=[
            pl.BlockSpec((c, width), lambda b, t: (row(b, t), OFF_SZ // width)),
            pl.BlockSpec((c, LANES), lambda b, t: (row(b, t), 0)),
            pl.BlockSpec((None, SUBLANES, SSD_CONV_DIM), lambda b, t: (b, 0, 0)),
            pl.BlockSpec((None, SSD_HEADS, SSD_STATE, SSD_HEADDIM), lambda b, t: (b, 0, 0, 0)),
            pl.BlockSpec(shift.shape, const2),
            pl.BlockSpec((CONV_W, SSD_CONV_DIM), const2),
            pl.BlockSpec((1, SSD_CONV_DIM), const2),
            pl.BlockSpec((1, LANES), const2),
            pl.BlockSpec((1, LANES), const2),
            pl.BlockSpec((1, SSD_INNER), const2),
            pl.BlockSpec((1, SSD_INNER), const2),
        ],
        out_specs=[
            pl.BlockSpec((c, SSD_INNER), lambda b, t: (row(b, t), 0)),
            pl.BlockSpec((None, None, SSD_HEADS, SSD_STATE, SSD_HEADDIM), lambda b, t: (layer, b, 0, 0, 0)),
            pl.BlockSpec((None, SUBLANES, SSD_CONV_DIM), lambda b, t: (b, 0, 0)),
        ],
        out_shape=[
            jax.ShapeDtypeStruct((batch * seq, SSD_INNER), BF16),
            jax.ShapeDtypeStruct(states.shape, F32),
            jax.ShapeDtypeStruct((batch, SUBLANES, SSD_CONV_DIM), F32),
        ],
        carried={1: states},
        scratch=[
            pltpu.VMEM((_stage_rows(c), SSD_CONV_DIM), BF16),
            pltpu.VMEM((SSD_GROUPS, SSD_STATE, SSD_GROUP_WIDTH), F32),
            pltpu.VMEM((c, SSD_INNER), BF16),
            pltpu.VMEM((c, SSD_INNER), F32),
        ],
        args=(proj, dt_raw, conv0, s0, shift, p["ssd_conv_w"], p["ssd_conv_b"],
              p["ssd_dt_bias"], p["ssd_a_log"], p["ssd_d_exp"], p["ssd_norm"]))


def _xattn_body(x_ref, nx_ref, wq_ref, mk_ref, mv_ref, wo_ref, nf_ref, o_ref, *rest, final):
    if final:
        hn_ref, part_ref = rest
    else:
        hn_next_ref, hn_ref, part_ref = rest
    _rmsnorm_rows_to(hn_ref, x_ref, nx_ref, part_ref)
    q = _dot(hn_ref[...], wq_ref[...])
    scale = XATT_HD ** -0.5
    os = []
    for h in range(XATT_HEADS):
        sl = slice(h * XATT_HD, (h + 1) * XATT_HD)
        s = _dot_nt(q[:, sl].astype(BF16), mk_ref[:, sl].astype(BF16)) * scale
        mx = jnp.max(s, axis=-1, keepdims=True)
        pr = jnp.exp(s - mx)
        den = jnp.sum(pr, axis=-1, keepdims=True)
        os.append(_dot(pr.astype(BF16), mv_ref[:, sl].astype(BF16)) / den)
    o = jnp.concatenate(os, axis=-1).astype(BF16)
    o_ref[...] = x_ref[...] + _dot(o, wo_ref[...])
    if final:
        _rmsnorm_rows_to(o_ref, o_ref, nf_ref, part_ref)
    else:
        _rmsnorm_rows_to(hn_next_ref, o_ref, nf_ref, part_ref)


def _cross_attn(x, mem_k, mem_v, nx, wq, wo, nf, *, batch, seq, tm, final):
    d = x.shape[1]
    tm = min(tm, seq)
    nt = seq // tm
    const2 = lambda b, t: (0, 0)
    x_spec = pl.BlockSpec((tm, d), lambda b, t: (b * nt + t, 0))
    out_specs = x_spec if final else [x_spec, x_spec]
    out_shape = jax.ShapeDtypeStruct(x.shape, F32)
    if not final:
        out_shape = [out_shape, jax.ShapeDtypeStruct(x.shape, BF16)]
    return pl.pallas_call(
        functools.partial(_xattn_body, final=final),
        grid=(batch, nt),
        in_specs=[
            x_spec,
            pl.BlockSpec((1, d), const2),
            pl.BlockSpec((d, XATT_DIM), const2),
            pl.BlockSpec((None, N_MEM, XATT_DIM), lambda b, t: (b, 0, 0)),
            pl.BlockSpec((None, N_MEM, XATT_DIM), lambda b, t: (b, 0, 0)),
            pl.BlockSpec((XATT_DIM, d), const2),
            pl.BlockSpec((1, d), const2),
        ],
        out_specs=out_specs,
        out_shape=out_shape,
        scratch_shapes=[pltpu.VMEM((tm, d), BF16), pltpu.VMEM((tm, LANES), F32)],
        compiler_params=_params(("arbitrary", "arbitrary")),
        name="cross_attn",
    )(x, nx.reshape(1, d), wq, mem_k, mem_v, wo, nf.reshape(1, d))


def _pad_state_rows(s):
    return jnp.pad(s.astype(F32), ((0, 0), (SUBLANES - (CONV_W - 1), 0), (0, 0)))


def _rope_tables(pos):
    half = RET_DK // 2
    inv = ROPE_BASE ** (-jnp.arange(half, dtype=F32) / half)
    ang = pos[:, None] * inv[None, :]
    cos, sin = jnp.cos(ang), jnp.sin(ang)
    cq = jnp.concatenate([cos, cos], axis=-1)
    sq = jnp.concatenate([-sin, sin], axis=-1)
    ks = RET_DK ** -0.5
    return cq, sq, cq * ks, sq * ks


def _layer_params(l, norm_mix, w_in, lru_conv_w, lru_conv_b, lru_wa, lru_ba, lru_wx, lru_bx, lru_lambda, lru_norm,
                  ret_norm, ssd_conv_w, ssd_conv_b, ssd_dt_bias, ssd_a_log, ssd_d, ssd_norm, w_out, norm_xattn,
                  norm_mem, w_q_mem, w_k_mem, w_v_mem, w_o_mem):
    w_dt = jnp.pad(w_in[l, :, OFF_DT:], ((0, 0), (0, LANES - SSD_HEADS)))
    w_q, w_kv, w_o = _cast_xattn_weights(w_q_mem, w_k_mem, w_v_mem, w_o_mem, l)
    row = lambda v: v.reshape(1, -1).astype(F32)
    pad_lanes = lambda v: jnp.pad(v.astype(F32), (0, LANES - v.shape[0])).reshape(1, LANES)
    return dict(
        norm_mix=norm_mix[l], w_in=w_in, layer=l, w_dt=w_dt,
        lru_conv_w=lru_conv_w[l], lru_conv_b=row(lru_conv_b[l]), lru_wa=lru_wa[l].astype(BF16), lru_ba=row(lru_ba[l]),
        lru_wx=lru_wx[l].astype(BF16), lru_bx=row(lru_bx[l]), lru_lambda=row(lru_lambda[l]), lru_norm=row(lru_norm[l]),
        ret_norm=row(ret_norm[l]),
        ssd_conv_w=ssd_conv_w[l], ssd_conv_b=row(ssd_conv_b[l]), ssd_dt_bias=pad_lanes(ssd_dt_bias[l]),
        ssd_a_log=pad_lanes(ssd_a_log[l]), ssd_d_exp=row(jnp.repeat(ssd_d[l], SSD_HEADDIM)), ssd_norm=row(ssd_norm[l]),
        w_out=_cast_layer_bf16(w_out, l), norm_xattn=norm_xattn[l], norm_mem=norm_mem[l],
        w_q=w_q, w_kv=w_kv, w_o=w_o)


def _mixer_sublayer(x, hn, rope, lru_h0, lru_conv0, ret_s0, ssd_s0, ssd_conv0, ret_states, ssd_states, p, *,
                    batch, seq):
    layer = p["layer"]
    proj, dt_raw = _in_proj(hn, p["w_in"], layer, p["w_dt"], n=OFF_DT, tm=TM_IN_PROJ, tn=TN_PROJ)
    c = min(CHUNK, seq)
    (lru_out, lru_h1, lru_ct), (ssd_out, ssd_states, ssd_ct) = _run_mixers(
        [_lru_parts(proj, _pad_state_rows(lru_conv0), lru_h0.reshape(batch, 1, LRU_WIDTH), p,
                    batch=batch, seq=seq, tb=c),
         _ssd_parts(proj, dt_raw, _pad_state_rows(ssd_conv0), ssd_s0, p, ssd_states, layer,
                    batch=batch, seq=seq, c=c)],
        batch=batch, nt=seq // c)
    c_ret = min(CHUNK_RET, seq)
    ((ret_out, ret_states),) = _run_mixers(
        [_ret_parts(proj, ret_s0, rope, p, ret_states, layer, batch=batch, seq=seq, c=c_ret)],
        batch=batch, nt=seq // c_ret)
    x1 = _out_proj(x, lru_out, ret_out, ssd_out, p["w_out"], tm=TM_OUT, tn=TN_OUT)
    tail = SUBLANES - (CONV_W - 1)
    small_states = (lru_h1.reshape(batch, LRU_WIDTH), lru_ct[:, tail:, :], ssd_ct[:, tail:, :])
    return x1, small_states, ret_states, ssd_states


def kernel(x_prompt, x_sample, mem_prompt, state_lru_h, state_lru_conv, state_ret, state_ssd, state_ssd_conv, cache_mem_k, cache_mem_v, norm_mix, w_in, lru_conv_w, lru_conv_b, lru_wa, lru_ba, lru_wx, lru_bx, lru_lambda, lru_norm, ret_norm, ssd_conv_w, ssd_conv_b, ssd_dt_bias, ssd_a_log, ssd_d, ssd_norm, w_out, norm_xattn, norm_mem, w_q_mem, w_k_mem, w_v_mem, w_o_mem, norm_final):
    bp, tp, d = x_prompt.shape
    bs, ts, _ = x_sample.shape
    depth = w_in.shape[0]
    assert tp % BF16_ROWS == 0 and ts % BF16_ROWS == 0
    rope_p = _rope_tables(jnp.arange(tp, dtype=F32))
    rope_s = _rope_tables(PAST_LEN + jnp.arange(ts, dtype=F32))
    z_lru_h = jnp.zeros((bp, LRU_WIDTH), F32)
    z_lru_conv = jnp.zeros((bp, CONV_W - 1, LRU_WIDTH), F32)
    z_ret = jnp.zeros((bp, RET_HEADS, RET_DK, RET_DV), F32)
    z_ssd = jnp.zeros((bp, SSD_HEADS, SSD_STATE, SSD_HEADDIM), F32)
    z_ssd_conv = jnp.zeros((bp, CONV_W - 1, SSD_CONV_DIM), F32)

    xp = x_prompt.reshape(bp * tp, d)
    xs = x_sample.reshape(bs * ts, d)
    mem = mem_prompt.reshape(bp * N_MEM, d)
    w_in = w_in.astype(BF16)
    hn_p = _rmsnorm_bf16(xp, norm_mix[0], tm=TM_XATT)
    hn_s = _rmsnorm_bf16(xs, norm_mix[0], tm=TM_XATT)
    st_p = [[] for _ in range(3)]
    st_s = [[] for _ in range(3)]
    ret_p = jnp.zeros((depth,) + z_ret.shape, F32)
    ssd_p = jnp.zeros((depth,) + z_ssd.shape, F32)
    ret_s = jnp.zeros(state_ret.shape, F32)
    ssd_s = jnp.zeros(state_ssd.shape, F32)
    mk_p, mv_p = [], []
    for l in range(depth):
        p = _layer_params(l, norm_mix, w_in, lru_conv_w, lru_conv_b, lru_wa, lru_ba, lru_wx, lru_bx, lru_lambda,
                          lru_norm, ret_norm, ssd_conv_w, ssd_conv_b, ssd_dt_bias, ssd_a_log, ssd_d, ssd_norm, w_out,
                          norm_xattn, norm_mem, w_q_mem, w_k_mem, w_v_mem, w_o_mem)
        final = l == depth - 1
        next_gain = norm_final if final else norm_mix[l + 1]
        xp, st, ret_p, ssd_p = _mixer_sublayer(xp, hn_p, rope_p, z_lru_h, z_lru_conv, z_ret, z_ssd, z_ssd_conv,
                                               ret_p, ssd_p, p, batch=bp, seq=tp)
        mkv = _norm_matmul(mem, p["norm_mem"], p["w_kv"], out_dtype=F32, tm=TM_PROJ, tn=TN_PROJ)
        mk = mkv[:, :XATT_DIM].reshape(bp, N_MEM, XATT_DIM)
        mv = mkv[:, XATT_DIM:].reshape(bp, N_MEM, XATT_DIM)
        res = _cross_attn(xp, mk, mv, p["norm_xattn"], p["w_q"], p["w_o"], next_gain, batch=bp, seq=tp, tm=TM_XATT,
                          final=final)
        xp, hn_p = (res, None) if final else res
        for acc, v in zip(st_p, st):
            acc.append(v)
        mk_p.append(mk.reshape(bp, N_MEM, XATT_HEADS, XATT_HD))
        mv_p.append(mv.reshape(bp, N_MEM, XATT_HEADS, XATT_HD))
        xs, st, ret_s, ssd_s = _mixer_sublayer(xs, hn_s, rope_s, state_lru_h[l], state_lru_conv[l], state_ret[l],
                                               state_ssd[l], state_ssd_conv[l], ret_s, ssd_s, p, batch=bs, seq=ts)
        res = _cross_attn(xs, cache_mem_k[l].reshape(bs, N_MEM, XATT_DIM), cache_mem_v[l].reshape(bs, N_MEM, XATT_DIM),
                          p["norm_xattn"], p["w_q"], p["w_o"], next_gain, batch=bs, seq=ts, tm=TM_XATT, final=final)
        xs, hn_s = (res, None) if final else res
        for acc, v in zip(st_s, st):
            acc.append(v)

    y_prompt = xp.reshape(bp, tp, d)
    y_sample = xs.reshape(bs, ts, d)
    return (y_prompt, y_sample,
            jnp.stack(st_p[0]), jnp.stack(st_p[1]), ret_p, ssd_p, jnp.stack(st_p[2]),
            jnp.stack(mk_p), jnp.stack(mv_p),
            jnp.stack(st_s[0]), jnp.stack(st_s[1]), ret_s, ssd_s, jnp.stack(st_s[2]))
```

```python
import functools
import math
from typing import Callable, NamedTuple

import numpy as np
import jax
import jax.numpy as jnp
from jax import lax
from jax.experimental import pallas as pl
from jax.experimental.pallas import tpu as pltpu

F32 = jnp.float32
BF16 = jnp.bfloat16

EPS = 1e-6
CONV_W = 4
PAST_LEN = 4096
N_MEM = 256
LRU_WIDTH = 1024
LRU_BLOCKS = 4
LRU_BLOCK = LRU_WIDTH // LRU_BLOCKS
LRU_C = 8.0
RET_HEADS = 4
RET_DK = 128
RET_DV = 256
RET_QK = RET_HEADS * RET_DK
RET_V = RET_HEADS * RET_DV
ROPE_BASE = 10000.0
SSD_INNER = 2048
SSD_HEADDIM = 64
SSD_HEADS = SSD_INNER // SSD_HEADDIM
SSD_STATE = 128
SSD_GROUPS = 4
SSD_GROUP_HEADS = SSD_HEADS // SSD_GROUPS
SSD_GROUP_WIDTH = SSD_GROUP_HEADS * SSD_HEADDIM
SSD_BC = SSD_GROUPS * SSD_STATE
SSD_CONV_DIM = SSD_INNER + 2 * SSD_BC
XATT_HEADS = 4
XATT_HD = 128
XATT_DIM = XATT_HEADS * XATT_HD

SUBLANES = 8
LANES = 128
BF16_ROWS = 2 * SUBLANES
VMEM_LIMIT_BYTES = 56 * 1024 * 1024

OFF_LX, OFF_LG, OFF_RQ, OFF_RK, OFF_RV, OFF_RG, OFF_SZ, OFF_SX = 0, 1024, 2048, 2560, 3072, 4096, 5120, 7168
OFF_SB = OFF_SX + SSD_INNER
OFF_SC = OFF_SB + SSD_BC
OFF_DT = OFF_SX + SSD_CONV_DIM

RET_LOG_GAMMA = tuple(math.log1p(-(2.0 ** (-5.0 - h))) for h in range(RET_HEADS))
LOG2E = math.log2(math.e)

TM_PROJ, TN_PROJ = 512, 1024
TM_IN_PROJ, TN_IN_PROJ = 1024, 1280
TM_OUT, TN_OUT = 1024, 1024
CAST_ROWS = 256
CAST_ROWS_W_IN = 512
TM_XATT = 256
CHUNK = 128
CHUNK_RET = 256


def _params(semantics):
    return pltpu.CompilerParams(dimension_semantics=semantics, vmem_limit_bytes=VMEM_LIMIT_BYTES)


def _sigmoid(x):
    return 1.0 / (1.0 + jnp.exp(-x))


def _silu(x):
    return x * _sigmoid(x)


def _softplus(x):
    return jnp.maximum(x, 0.0) + jnp.log1p(jnp.exp(-jnp.abs(x)))


def _dot(a, b):
    return jnp.dot(a, b, preferred_element_type=F32)


def _dot_nt(a, b):
    return lax.dot_general(a, b, (((1,), (1,)), ((), ())), preferred_element_type=F32)


def _dot_tn(a, b):
    return lax.dot_general(a, b, (((0,), (0,)), ((), ())), preferred_element_type=F32)


def _rmsnorm_rows_to(dst_ref, src_ref, gain_ref, part_ref, unroll=False):
    tm, d = src_ref.shape
    rc = BF16_ROWS
    lane_tiles = d // LANES

    def sum_squares(r, carry):
        rows = pl.ds(pl.multiple_of(r * rc, rc), rc)
        accs = [None] * min(4, lane_tiles)
        for j in range(lane_tiles):
            v = src_ref[rows, j * LANES:(j + 1) * LANES].astype(F32)
            k = j % len(accs)
            accs[k] = v * v if accs[k] is None else accs[k] + v * v
        while len(accs) > 1:
            accs = [a + b for a, b in zip(accs[0::2], accs[1::2])]
        part_ref[rows, :] = accs[0]
        return carry

    lax.fori_loop(0, tm // rc, sum_squares, 0, unroll=unroll)
    ms = jnp.sum(part_ref[...], axis=-1, keepdims=True) * (1.0 / d)
    part_ref[...] = jnp.broadcast_to(lax.rsqrt(ms + EPS), part_ref.shape)

    def scale(r, carry):
        rows = pl.ds(pl.multiple_of(r * rc, rc), rc)
        rs = part_ref[rows, :]
        for j in range(lane_tiles):
            sl = slice(j * LANES, (j + 1) * LANES)
            dst_ref[rows, sl] = (src_ref[rows, sl].astype(F32) * rs * gain_ref[:, sl]).astype(dst_ref.dtype)
        return carry

    lax.fori_loop(0, tm // rc, scale, 0, unroll=unroll)


def _cast_body(w_ref, o_ref):
    o_ref[...] = w_ref[...].astype(o_ref.dtype)


def _cast_layer_bf16(w_stack, l):
    _, r, c = w_stack.shape
    rows = min(CAST_ROWS, r)
    assert r % rows == 0
    return pl.pallas_call(
        _cast_body,
        grid=(r // rows,),
        in_specs=[pl.BlockSpec((None, rows, c), lambda i: (l, i, 0))],
        out_specs=pl.BlockSpec((rows, c), lambda i: (i, 0)),
        out_shape=jax.ShapeDtypeStruct((r, c), BF16),
        compiler_params=_params(("arbitrary",)),
        name="cast_bf16",
    )(w_stack)


def _cast_stack_bf16(w_stack, *, n_rows, rows):
    layers, r, c = w_stack.shape
    assert n_rows <= r and n_rows % rows == 0
    spec = pl.BlockSpec((None, rows, c), lambda l, i: (l, i, 0))
    return pl.pallas_call(
        _cast_body,
        grid=(layers, n_rows // rows),
        in_specs=[spec],
        out_specs=spec,
        out_shape=jax.ShapeDtypeStruct((layers, n_rows, c), BF16),
        compiler_params=_params(("arbitrary", "arbitrary")),
        name="cast_stack",
    )(w_stack)


def _cast_xattn_body(wq_ref, wk_ref, wv_ref, wo_ref, oq_ref, okv_ref, oo_ref):
    oq_ref[...] = wq_ref[...].astype(oq_ref.dtype)
    okv_ref[:, 0:XATT_DIM] = wk_ref[...].astype(okv_ref.dtype)
    okv_ref[:, XATT_DIM:2 * XATT_DIM] = wv_ref[...].astype(okv_ref.dtype)
    oo_ref[...] = wo_ref[...].astype(oo_ref.dtype)


def _cast_xattn_weights(w_q, w_k, w_v, w_o, l):
    _, d, n = w_q.shape
    steps = d // CAST_ROWS
    assert d % CAST_ROWS == 0 and n % steps == 0 and (n // steps) % BF16_ROWS == 0
    in_rows = lambda i: (l, i, 0)
    return pl.pallas_call(
        _cast_xattn_body,
        grid=(steps,),
        in_specs=[pl.BlockSpec((None, CAST_ROWS, n), in_rows), pl.BlockSpec((None, CAST_ROWS, n), in_rows),
                  pl.BlockSpec((None, CAST_ROWS, n), in_rows), pl.BlockSpec((None, n // steps, d), in_rows)],
        out_specs=[pl.BlockSpec((CAST_ROWS, n), lambda i: (i, 0)), pl.BlockSpec((CAST_ROWS, 2 * n), lambda i: (i, 0)),
                   pl.BlockSpec((n // steps, d), lambda i: (i, 0))],
        out_shape=[jax.ShapeDtypeStruct((d, n), BF16), jax.ShapeDtypeStruct((d, 2 * n), BF16),
                   jax.ShapeDtypeStruct((n, d), BF16)],
        compiler_params=_params(("arbitrary",)),
        name="cast_xattn",
    )(w_q, w_k, w_v, w_o)


def _norm_matmul_body(x_ref, g_ref, w_ref, o_ref, hn_ref, part_ref):
    @pl.when(pl.program_id(1) == 0)
    def _():
        _rmsnorm_rows_to(hn_ref, x_ref, g_ref, part_ref)

    o_ref[...] = _dot(hn_ref[...], w_ref[...]).astype(o_ref.dtype)


def _norm_matmul(x, g, w, *, out_dtype, tm, tn):
    m, k = x.shape
    n = w.shape[1]
    tm = min(tm, m)
    tn = min(tn, n)
    assert m % tm == 0 and n % tn == 0
    return pl.pallas_call(
        _norm_matmul_body,
        grid=(m // tm, n // tn),
        in_specs=[
            pl.BlockSpec((tm, k), lambda i, j: (i, 0)),
            pl.BlockSpec((1, k), lambda i, j: (0, 0)),
            pl.BlockSpec((k, tn), lambda i, j: (0, j)),
        ],
        out_specs=pl.BlockSpec((tm, tn), lambda i, j: (i, j)),
        out_shape=jax.ShapeDtypeStruct((m, n), out_dtype),
        scratch_shapes=[pltpu.VMEM((tm, k), BF16), pltpu.VMEM((tm, LANES), F32)],
        compiler_params=_params(("arbitrary", "arbitrary")),
        name="norm_matmul",
    )(x, g.reshape(1, k), w)


def _rmsnorm_body(x_ref, g_ref, o_ref, part_ref):
    _rmsnorm_rows_to(o_ref, x_ref, g_ref, part_ref)


def _rmsnorm_bf16(x, g, *, tm):
    m, k = x.shape
    tm = min(tm, m)
    assert m % tm == 0
    return pl.pallas_call(
        _rmsnorm_body,
        grid=(m // tm,),
        in_specs=[pl.BlockSpec((tm, k), lambda i: (i, 0)), pl.BlockSpec((1, k), lambda i: (0, 0))],
        out_specs=pl.BlockSpec((tm, k), lambda i: (i, 0)),
        out_shape=jax.ShapeDtypeStruct((m, k), BF16),
        scratch_shapes=[pltpu.VMEM((tm, LANES), F32)],
        compiler_params=_params(("arbitrary",)),
        name="rmsnorm_bf16",
    )(x, g.reshape(1, k))


def _in_proj_body(hn_ref, wt_ref, w2t_ref, o_ref, o2_ref):
    @pl.when(pl.program_id(1) == 0)
    def _():
        o2_ref[...] = _dot_nt(hn_ref[...], w2t_ref[...])

    o_ref[...] = _dot_nt(hn_ref[...], wt_ref[...]).astype(o_ref.dtype)


def _in_proj(hn, wt, l, w2t, *, n, tm, tn):
    m, k = hn.shape
    n2 = w2t.shape[0]
    tm = min(tm, m)
    assert m % tm == 0 and n % tn == 0 and n <= wt.shape[1]
    w, w2 = wt, w2t
    return pl.pallas_call(
        _in_proj_body,
        grid=(m // tm, n // tn),
        in_specs=[
            pl.BlockSpec((tm, k), lambda i, j: (i, 0)),
            pl.BlockSpec((None, tn, k), lambda i, j: (l, j, 0)),
            pl.BlockSpec((n2, k), lambda i, j: (0, 0)),
        ],
        out_specs=[pl.BlockSpec((tm, tn), lambda i, j: (i, j)), pl.BlockSpec((tm, n2), lambda i, j: (i, 0))],
        out_shape=[jax.ShapeDtypeStruct((m, n), BF16), jax.ShapeDtypeStruct((m, n2), F32)],
        compiler_params=_params(("arbitrary", "arbitrary")),
        name="in_proj",
    )(hn, w, w2)


def _out_proj_body(x_ref, a1_ref, a2_ref, a3_ref, w1_ref, w2_ref, w3_ref, o_ref):
    acc = _dot(a1_ref[...], w1_ref[...])
    acc = acc + _dot(a2_ref[...], w2_ref[...])
    acc = acc + _dot(a3_ref[...], w3_ref[...])
    o_ref[...] = x_ref[...] + acc


def _out_proj(x, lru_out, ret_out, ssd_out, w_out_bf16, *, tm, tn):
    m, n = x.shape
    tm = min(tm, m)
    tn = min(tn, n)
    assert m % tm == 0 and n % tn == 0
    return pl.pallas_call(
        _out_proj_body,
        grid=(m // tm, n // tn),
        in_specs=[
            pl.BlockSpec((tm, tn), lambda i, j: (i, j)),
            pl.BlockSpec((tm, LRU_WIDTH), lambda i, j: (i, 0)),
            pl.BlockSpec((tm, RET_V), lambda i, j: (i, 0)),
            pl.BlockSpec((tm, SSD_INNER), lambda i, j: (i, 0)),
            pl.BlockSpec((LRU_WIDTH, tn), lambda i, j: (0, j)),
            pl.BlockSpec((RET_V, tn), lambda i, j: (1, j)),
            pl.BlockSpec((SSD_INNER, tn), lambda i, j: (1, j)),
        ],
        out_specs=pl.BlockSpec((tm, tn), lambda i, j: (i, j)),
        out_shape=jax.ShapeDtypeStruct((m, n), F32),
        compiler_params=_params(("arbitrary", "arbitrary")),
        name="out_proj",
    )(x, lru_out, ret_out, ssd_out, w_out_bf16, w_out_bf16, w_out_bf16)


HIST_ROWS = BF16_ROWS


def _stage_rows(c):
    return -(-(HIST_ROWS + c) // LANES) * LANES


def _shift_matrix(c):
    s = np.zeros((CONV_W - 1, c, _stage_rows(c)), np.float32)
    for k in range(CONV_W - 1):
        back = k + 1
        for t in range(c):
            if t >= back:
                s[k, t, HIST_ROWS + t - back] = 1.0
            else:
                j = (CONV_W - 1) + t - back
                s[k, t, j] = s[k, t, 3 + j] = s[k, t, SUBLANES + j] = 1.0
    return jnp.asarray(s.reshape((CONV_W - 1) * c, -1), dtype=BF16)


def _history_tile(tail8):
    r = pltpu.roll(tail8, CONV_W - 1, 0)
    hi = r.astype(BF16).astype(F32)
    rest = r - hi
    mid = rest.astype(BF16).astype(F32)
    lo = rest - mid
    rows = lax.broadcasted_iota(jnp.int32, r.shape, 0)
    top = jnp.where(rows < 3, hi, jnp.where(rows < 6, pltpu.roll(mid, CONV_W - 1, 0), 0.0))
    bot = jnp.where(rows < 3, lo, 0.0)
    return jnp.concatenate([top, bot], axis=0).astype(BF16)


def _conv_from_stage(stage, shift_ref, x_f32, c, cw_ref, cb_ref):
    sh = _dot(shift_ref[...], stage[...])
    acc = sh[2 * c:3 * c] * cw_ref[0:1, :]
    acc = acc + sh[c:2 * c] * cw_ref[1:2, :]
    acc = acc + sh[0:c] * cw_ref[2:3, :]
    acc = acc + x_f32 * cw_ref[3:4, :]
    return cb_ref[...] + acc


class MixerParts(NamedTuple):
    body: Callable
    args: tuple
    in_specs: list
    out_specs: list
    out_shape: list
    scratch: list
    carried: dict = {}


def _mixers_body(*refs, parts, n_carried):
    n_in = sum(len(p.args) for p in parts)
    n_out = sum(len(p.out_specs) for p in parts)
    ins = refs[:n_in]
    outs = refs[n_in + n_carried:n_in + n_carried + n_out]
    scr = refs[n_in + n_carried + n_out:]
    i = o = s = 0
    for p in parts:
        ni, no, ns = len(p.args), len(p.out_specs), len(p.scratch)
        p.body(*ins[i:i + ni], *outs[o:o + no], *scr[s:s + ns])
        i, o, s = i + ni, o + no, s + ns


def _run_mixers(parts, *, batch, nt):
    flat = lambda field: [x for p in parts for x in getattr(p, field)]
    bodies = tuple(p._replace(args=tuple(None for _ in p.args), out_shape=[], carried={}) for p in parts)
    args, in_specs = flat("args"), flat("in_specs")
    aliases, o = {}, 0
    for p in parts:
        for k, buf in p.carried.items():
            aliases[len(args)] = o + k
            args.append(buf)
            in_specs.append(pl.BlockSpec(memory_space=pl.ANY))
        o += len(p.out_specs)
    outs = pl.pallas_call(
        functools.partial(_mixers_body, parts=bodies, n_carried=len(aliases)),
        grid=(batch, nt),
        in_specs=in_specs,
        out_specs=flat("out_specs"),
        out_shape=flat("out_shape"),
        scratch_shapes=flat("scratch"),
        input_output_aliases=aliases,
        compiler_params=_params(("arbitrary", "arbitrary")),
        name="mixers",
    )(*args)
    res, o = [], 0
    for p in parts:
        res.append(outs[o:o + len(p.out_specs)])
        o += len(p.out_specs)
    return res


def _lru_body(lxg_ref, conv0_ref, h0_ref, shift_ref, cw_ref, cb_ref, wa_ref, ba_ref, wx_ref, bx_ref, lam_ref,
              nrm_ref, out_ref, hlast_ref, ctail_ref, stage, hcar, a_scr, b_scr):
    tb = lxg_ref.shape[0]
    lx_ref = lxg_ref.at[:, 0:LRU_WIDTH]
    lg_ref = lxg_ref.at[:, LRU_WIDTH:2 * LRU_WIDTH]

    @pl.when(pl.program_id(1) == 0)
    def _():
        stage[...] = jnp.zeros(stage.shape, stage.dtype)
        stage[0:HIST_ROWS, :] = _history_tile(conv0_ref[...])
        hcar[...] = h0_ref[...]

    stage[pl.ds(HIST_ROWS, tb), :] = lx_ref[...]
    x_in = lx_ref[...].astype(F32)
    xc = _conv_from_stage(stage, shift_ref, x_in, tb, cw_ref, cb_ref)
    tail = x_in[tb - SUBLANES:tb]
    stage[0:HIST_ROWS, :] = _history_tile(tail)
    ctail_ref[...] = tail

    xb = xc.astype(BF16)
    ra, ri = [], []
    for n in range(LRU_BLOCKS):
        xn = xb[:, n * LRU_BLOCK:(n + 1) * LRU_BLOCK]
        ra.append(_dot(xn, wa_ref[n]))
        ri.append(_dot(xn, wx_ref[n]))
    r = _sigmoid(jnp.concatenate(ra, axis=-1) + ba_ref[...])
    i = _sigmoid(jnp.concatenate(ri, axis=-1) + bx_ref[...])
    log_a = r * ((-LRU_C) * _softplus(-lam_ref[...]))
    a = jnp.exp(log_a)
    th = jnp.tanh(log_a)
    b = jnp.sqrt(-2.0 * th / (1.0 - th)) * (i * xc)

    ng = tb // SUBLANES
    for j in range(LRU_WIDTH // LANES):
        a_scr[j] = a[:, j * LANES:(j + 1) * LANES]
        b_scr[j] = b[:, j * LANES:(j + 1) * LANES]
    for j in range(LRU_WIDTH // LANES):
        pa = a_scr[j, pl.ds(0, ng, stride=SUBLANES), :]
        pb = b_scr[j, pl.ds(0, ng, stride=SUBLANES), :]
        for r in range(1, SUBLANES):
            view = pl.ds(r, ng, stride=SUBLANES)
            ar = a_scr[j, view, :]
            pb = ar * pb + b_scr[j, view, :]
            pa = ar * pa
            a_scr[j, view, :] = pa
            b_scr[j, view, :] = pb
    a3 = jnp.concatenate([a_scr[j] for j in range(LRU_WIDTH // LANES)], axis=-1).reshape(ng, SUBLANES, LRU_WIDTH)
    b3 = jnp.concatenate([b_scr[j] for j in range(LRU_WIDTH // LANES)], axis=-1).reshape(ng, SUBLANES, LRU_WIDTH)
    hprev = hcar[...]
    hs = []
    for g in range(ng):
        hg = b3[g] + a3[g] * hprev
        hprev = hg[SUBLANES - 1:SUBLANES]
        hs.append(hg)
    h = jnp.concatenate(hs, axis=0)
    hcar[...] = hprev
    hlast_ref[...] = hprev

    ms = jnp.mean(h * h, axis=-1, keepdims=True)
    out = h * lax.rsqrt(ms + EPS) * nrm_ref[...]
    out_ref[...] = (out * _silu(lg_ref[...].astype(F32))).astype(out_ref.dtype)


def _lru_parts(proj, conv0, h0, p, *, batch, seq, tb):
    assert seq % tb == 0 and OFF_LG == OFF_LX + LRU_WIDTH and OFF_LX % (2 * LRU_WIDTH) == 0
    nt = seq // tb
    w = LRU_WIDTH
    row = lambda b, t: b * nt + t
    const2 = lambda b, t: (0, 0)
    shift = _shift_matrix(tb)
    return MixerParts(
        body=_lru_body,
        in_specs=[
            pl.BlockSpec((tb, 2 * w), lambda b, t: (row(b, t), OFF_LX // (2 * w))),
            pl.BlockSpec((None, SUBLANES, w), lambda b, t: (b, 0, 0)),
            pl.BlockSpec((None, 1, w), lambda b, t: (b, 0, 0)),
            pl.BlockSpec(shift.shape, const2),
            pl.BlockSpec((CONV_W, w), const2),
            pl.BlockSpec((1, w), const2),
            pl.BlockSpec((LRU_BLOCKS, LRU_BLOCK, LRU_BLOCK), lambda b, t: (0, 0, 0)),
            pl.BlockSpec((1, w), const2),
            pl.BlockSpec((LRU_BLOCKS, LRU_BLOCK, LRU_BLOCK), lambda b, t: (0, 0, 0)),
            pl.BlockSpec((1, w), const2),
            pl.BlockSpec((1, w), const2),
            pl.BlockSpec((1, w), const2),
        ],
        out_specs=[
            pl.BlockSpec((tb, w), lambda b, t: (row(b, t), 0)),
            pl.BlockSpec((None, 1, w), lambda b, t: (b, 0, 0)),
            pl.BlockSpec((None, SUBLANES, w), lambda b, t: (b, 0, 0)),
        ],
        out_shape=[
            jax.ShapeDtypeStruct((batch * seq, w), BF16),
            jax.ShapeDtypeStruct((batch, 1, w), F32),
            jax.ShapeDtypeStruct((batch, SUBLANES, w), F32),
        ],
        scratch=[pltpu.VMEM((_stage_rows(tb), w), BF16), pltpu.VMEM((1, w), F32),
                 pltpu.VMEM((w // LANES, tb, LANES), F32), pltpu.VMEM((w // LANES, tb, LANES), F32)],
        args=(proj, conv0, h0, shift, p["lru_conv_w"], p["lru_conv_b"], p["lru_wa"], p["lru_ba"], p["lru_wx"],
              p["lru_bx"], p["lru_lambda"], p["lru_norm"]))


def _ret_body(qk_ref, v_ref, g_ref, cq_ref, sq_ref, ck_ref, sk_ref, s0_ref, nrm_ref,
              out_ref, snew_ref, state, decay_tab, eg_tab, wt_tab):
    c = qk_ref.shape[0]
    q_ref = qk_ref.at[:, 0:RET_QK]
    k_ref = qk_ref.at[:, RET_QK:2 * RET_QK]

    @pl.when(jnp.logical_and(pl.program_id(0) == 0, pl.program_id(1) == 0))
    def _():
        ii = lax.broadcasted_iota(jnp.int32, (c, c), 0)
        jj = lax.broadcasted_iota(jnp.int32, (c, c), 1)
        dpos = jnp.maximum(ii - jj, 0).astype(F32)
        rpos = lax.broadcasted_iota(jnp.int32, (c, RET_DK), 0).astype(F32)
        for h in range(RET_HEADS):
            lg = RET_LOG_GAMMA[h]
            decay_tab[h] = jnp.where(ii >= jj, jnp.exp(dpos * lg), 0.0)
            eg_tab[h] = jnp.exp((rpos + 1.0) * lg)
            wt_tab[h] = jnp.exp((c - 1.0 - rpos) * lg)

    @pl.when(pl.program_id(1) == 0)
    def _():
        state[...] = s0_ref[...]

    cq, sq, ck, sk = cq_ref[...], sq_ref[...], ck_ref[...], sk_ref[...]
    outs = []
    for h in range(RET_HEADS):
        decay, e_g, w_tail = decay_tab[h], eg_tab[h], wt_tab[h]
        a_chunk = math.exp(c * RET_LOG_GAMMA[h])
        qh = q_ref[:, h * RET_DK:(h + 1) * RET_DK].astype(F32)
        kh = k_ref[:, h * RET_DK:(h + 1) * RET_DK].astype(F32)
        qr = qh * cq + pltpu.roll(qh, RET_DK // 2, 1) * sq
        kr = kh * ck + pltpu.roll(kh, RET_DK // 2, 1) * sk
        vh = v_ref[:, h * RET_DV:(h + 1) * RET_DV]
        sh = state[h]
        scores = _dot_nt(qr.astype(BF16), kr.astype(BF16)) * decay
        y = _dot(scores.astype(BF16), vh) + _dot((qr * e_g).astype(BF16), sh.astype(BF16))
        state[h] = a_chunk * sh + _dot_tn((kr * w_tail).astype(BF16), vh)
        mu = jnp.mean(y, axis=-1, keepdims=True)
        d = y - mu
        var = jnp.mean(d * d, axis=-1, keepdims=True)
        outs.append(d * lax.rsqrt(var + EPS))
    ro = jnp.concatenate(outs, axis=-1) * nrm_ref[...]
    out_ref[...] = (ro * _silu(g_ref[...].astype(F32))).astype(out_ref.dtype)
    snew_ref[...] = state[...]


def _ret_parts(proj, s0, rope, p, states, layer, *, batch, seq, c):
    assert seq % c == 0 and states.shape[1:] == (batch, RET_HEADS, RET_DK, RET_DV)
    nt = seq // c
    row = lambda b, t: b * nt + t
    tab = pl.BlockSpec((c, RET_DK), lambda b, t: (t, 0))
    return MixerParts(
        body=_ret_body,
        in_specs=[
            pl.BlockSpec((c, 2 * RET_QK), lambda b, t: (row(b, t), OFF_RQ // (2 * RET_QK))),
            pl.BlockSpec((c, RET_V), lambda b, t: (row(b, t), OFF_RV // RET_V)),
            pl.BlockSpec((c, RET_V), lambda b, t: (row(b, t), OFF_RG // RET_V)),
            tab, tab, tab, tab,
            pl.BlockSpec((None, RET_HEADS, RET_DK, RET_DV), lambda b, t: (b, 0, 0, 0)),
            pl.BlockSpec((1, RET_V), lambda b, t: (0, 0)),
        ],
        out_specs=[
            pl.BlockSpec((c, RET_V), lambda b, t: (row(b, t), 0)),
            pl.BlockSpec((None, None, RET_HEADS, RET_DK, RET_DV), lambda b, t: (layer, b, 0, 0, 0)),
        ],
        out_shape=[
            jax.ShapeDtypeStruct((batch * seq, RET_V), BF16),
            jax.ShapeDtypeStruct(states.shape, F32),
        ],
        carried={1: states},
        scratch=[
            pltpu.VMEM((RET_HEADS, RET_DK, RET_DV), F32),
            pltpu.VMEM((RET_HEADS, c, c), F32),
            pltpu.VMEM((RET_HEADS, c, RET_DK), F32),
            pltpu.VMEM((RET_HEADS, c, RET_DK), F32),
        ],
        args=(proj, proj, proj, rope[0], rope[1], rope[2], rope[3], s0, p["ret_norm"]))


def _cumsum_rows(x):
    n = x.shape[0]
    rows = lax.broadcasted_iota(jnp.int32, x.shape, 0)
    s = 1
    while s < n:
        x = x + jnp.where(rows >= s, pltpu.roll(x, s, 0), 0.0)
        s *= 2
    return x


def _ssd_body(zxbc_ref, dt_ref, conv0_ref, s0_ref, shift_ref, cw_ref, cb_ref,
              dtb_ref, alog_ref, dexp_ref, nrm_ref, out_ref, snew_ref, ctail_ref, stage, state, xw_scr, u_scr, *,
              n_steps):
    c = zxbc_ref.shape[0]
    gh, hd, gw = SSD_GROUP_HEADS, SSD_HEADDIM, SSD_GROUP_WIDTH
    conv_in = lambda j: zxbc_ref[:, SSD_INNER + j * LANES:SSD_INNER + (j + 1) * LANES]

    @pl.when(pl.program_id(1) == 0)
    def _():
        stage[...] = jnp.zeros(stage.shape, stage.dtype)
        stage[0:HIST_ROWS, :] = _history_tile(conv0_ref[...])
        for g in range(SSD_GROUPS):
            state[g] = jnp.concatenate([s0_ref[g * gh + hh] for hh in range(gh)], axis=-1)

    stage[pl.ds(HIST_ROWS, c), :] = zxbc_ref[:, SSD_INNER:SSD_INNER + SSD_CONV_DIM]
    sh = _dot(shift_ref[...], stage[...])
    tail = stage[pl.ds(HIST_ROWS + c - BF16_ROWS, BF16_ROWS), :].astype(F32)[SUBLANES:]
    stage[0:HIST_ROWS, :] = _history_tile(tail)
    ctail_ref[...] = tail

    def conv_tile(x_bf16, j):
        sl = slice(j * LANES, (j + 1) * LANES)
        acc = sh[2 * c:3 * c, sl] * cw_ref[0:1, sl]
        acc = acc + sh[c:2 * c, sl] * cw_ref[1:2, sl]
        acc = acc + sh[0:c, sl] * cw_ref[2:3, sl]
        acc = acc + x_bf16.astype(F32) * cw_ref[3:4, sl]
        return _silu(cb_ref[:, sl] + acc)

    x_tiles = SSD_INNER // LANES
    bc_tiles = SSD_BC // LANES

    dt = _softplus(dt_ref[...] + dtb_ref[...])
    a_neg = -jnp.exp(alog_ref[...])
    g2 = _cumsum_rows(dt * a_neg) * LOG2E
    g2_t = g2.T
    lane_lo = lax.broadcasted_iota(jnp.int32, (c, LANES), 1) < hd
    ii = lax.broadcasted_iota(jnp.int32, (c, c), 0)
    jj = lax.broadcasted_iota(jnp.int32, (c, c), 1)
    tri = ii >= jj

    bgs, cbs, y_inter = [], [], []
    for g in range(SSD_GROUPS):
        bg = conv_tile(conv_in(x_tiles + g), x_tiles + g).astype(BF16)
        cg = conv_tile(conv_in(x_tiles + bc_tiles + g), x_tiles + bc_tiles + g).astype(BF16)
        bgs.append(bg)
        cbs.append(jnp.where(tri, _dot_nt(cg, bg), 0.0))
        y_inter.append(_dot(cg, state[g].astype(BF16)))

    ssq = None
    a_tiles = []
    for p in range(SSD_HEADS // 2):
        g, pp = divmod(p, gh // 2)
        sl = slice(p * LANES, (p + 1) * LANES)
        xs = conv_tile(conv_in(p), p)
        gcol = [jnp.broadcast_to(g2[:, h:h + 1], (c, LANES)) for h in (2 * p, 2 * p + 1)]
        dcol = [jnp.broadcast_to(dt[:, h:h + 1], (c, LANES)) for h in (2 * p, 2 * p + 1)]
        g_x = jnp.where(lane_lo, gcol[0], gcol[1])
        xdt = xs * jnp.where(lane_lo, dcol[0], dcol[1])
        g_last = g_x[c - 1:c, :]
        xw_scr[:, sl] = (xdt * jnp.exp2(g_last - g_x)).astype(BF16)
        a_tiles.append(jnp.exp2(g_last))
        ms = []
        for k, h in enumerate((2 * p, 2 * p + 1)):
            decay = jnp.exp2(jnp.minimum(gcol[k][:, 0:c] - g2_t[h:h + 1, :], 0.0))
            ms.append((cbs[g] * decay).astype(BF16))
        rhs = jnp.concatenate([jnp.where(lane_lo, xdt, 0.0), jnp.where(lane_lo, 0.0, xdt)], axis=0).astype(BF16)
        y = _dot(jnp.concatenate(ms, axis=1), rhs) + jnp.exp2(g_x) * y_inter[g][:, pp * LANES:(pp + 1) * LANES]
        so = y + dexp_ref[:, sl] * xs
        u = so * _silu(zxbc_ref[:, sl].astype(F32))
        ssq = u * u if ssq is None else ssq + u * u
        u_scr[:, sl] = u

    for g in range(SSD_GROUPS):
        gsl = slice(g * gw, (g + 1) * gw)
        a_row = jnp.concatenate(a_tiles[g * (gh // 2):(g + 1) * (gh // 2)], axis=-1)
        state[g] = a_row * state[g] + _dot_tn(bgs[g], xw_scr[:, gsl])

    rs = lax.rsqrt(jnp.sum(ssq, axis=-1, keepdims=True) * (1.0 / SSD_INNER) + EPS)
    out_ref[...] = (u_scr[...] * rs * nrm_ref[...]).astype(out_ref.dtype)

    @pl.when(pl.program_id(1) == n_steps - 1)
    def _():
        for g in range(SSD_GROUPS):
            sg = state[g]
            for hh in range(gh):
                snew_ref[g * gh + hh] = sg[:, hh * hd:(hh + 1) * hd]


def _ssd_parts(proj, dt_raw, conv0, s0, p, states, layer, *, batch, seq, c):
    assert seq % c == 0 and states.shape[1:] == (batch, SSD_HEADS, SSD_STATE, SSD_HEADDIM)
    nt = seq // c
    row = lambda b, t: b * nt + t
    const2 = lambda b, t: (0, 0)
    width = SSD_INNER + SSD_CONV_DIM
    assert OFF_SZ % width == 0 and OFF_SX == OFF_SZ + SSD_INNER
    shift = _shift_matrix(c)
    return MixerParts(
        body=functools.partial(_ssd_body, n_steps=nt),
        in_specs=[
            pl.BlockSpec((c, width), lambda b, t: (row(b, t), OFF_SZ // width)),
            pl.BlockSpec((c, LANES), lambda b, t: (row(b, t), 0)),
            pl.BlockSpec((None, SUBLANES, SSD_CONV_DIM), lambda b, t: (b, 0, 0)),
            pl.BlockSpec((None, SSD_HEADS, SSD_STATE, SSD_HEADDIM), lambda b, t: (b, 0, 0, 0)),
            pl.BlockSpec(shift.shape, const2),
            pl.BlockSpec((CONV_W, SSD_CONV_DIM), const2),
            pl.BlockSpec((1, SSD_CONV_DIM), const2),
            pl.BlockSpec((1, LANES), const2),
            pl.BlockSpec((1, LANES), const2),
            pl.BlockSpec((1, SSD_INNER), const2),
            pl.BlockSpec((1, SSD_INNER), const2),
        ],
        out_specs=[
            pl.BlockSpec((c, SSD_INNER), lambda b, t: (row(b, t), 0)),
            pl.BlockSpec((None, None, SSD_HEADS, SSD_STATE, SSD_HEADDIM), lambda b, t: (layer, b, 0, 0, 0)),
            pl.BlockSpec((None, SUBLANES, SSD_CONV_DIM), lambda b, t: (b, 0, 0)),
        ],
        out_shape=[
            jax.ShapeDtypeStruct((batch * seq, SSD_INNER), BF16),
            jax.ShapeDtypeStruct(states.shape, F32),
            jax.ShapeDtypeStruct((batch, SUBLANES, SSD_CONV_DIM), F32),
        ],
        carried={1: states},
        scratch=[
            pltpu.VMEM((_stage_rows(c), SSD_CONV_DIM), BF16),
            pltpu.VMEM((SSD_GROUPS, SSD_STATE, SSD_GROUP_WIDTH), F32),
            pltpu.VMEM((c, SSD_INNER), BF16),
            pltpu.VMEM((c, SSD_INNER), F32),
        ],
        args=(proj, dt_raw, conv0, s0, shift, p["ssd_conv_w"], p["ssd_conv_b"],
              p["ssd_dt_bias"], p["ssd_a_log"], p["ssd_d_exp"], p["ssd_norm"]))


def _xattn_body(x_ref, nx_ref, wq_ref, mk_ref, mv_ref, wo_ref, nf_ref, o_ref, *rest, final):
    if final:
        hn_ref, part_ref = rest
    else:
        hn_next_ref, hn_ref, part_ref = rest
    _rmsnorm_rows_to(hn_ref, x_ref, nx_ref, part_ref)
    q = _dot(hn_ref[...], wq_ref[...])
    scale = XATT_HD ** -0.5
    os = []
    for h in range(XATT_HEADS):
        sl = slice(h * XATT_HD, (h + 1) * XATT_HD)
        s = _dot_nt(q[:, sl].astype(BF16), mk_ref[:, sl].astype(BF16)) * scale
        mx = jnp.max(s, axis=-1, keepdims=True)
        pr = jnp.exp(s - mx)
        den = jnp.sum(pr, axis=-1, keepdims=True)
        os.append(_dot(pr.astype(BF16), mv_ref[:, sl].astype(BF16)) / den)
    o = jnp.concatenate(os, axis=-1).astype(BF16)
    o_ref[...] = x_ref[...] + _dot(o, wo_ref[...])
    if final:
        _rmsnorm_rows_to(o_ref, o_ref, nf_ref, part_ref)
    else:
        _rmsnorm_rows_to(hn_next_ref, o_ref, nf_ref, part_ref)


def _cross_attn(x, mem_k, mem_v, nx, wq, wo, nf, *, batch, seq, tm, final):
    d = x.shape[1]
    tm = min(tm, seq)
    nt = seq // tm
    const2 = lambda b, t: (0, 0)
    x_spec = pl.BlockSpec((tm, d), lambda b, t: (b * nt + t, 0))
    out_specs = x_spec if final else [x_spec, x_spec]
    out_shape = jax.ShapeDtypeStruct(x.shape, F32)
    if not final:
        out_shape = [out_shape, jax.ShapeDtypeStruct(x.shape, BF16)]
    return pl.pallas_call(
        functools.partial(_xattn_body, final=final),
        grid=(batch, nt),
        in_specs=[
            x_spec,
            pl.BlockSpec((1, d), const2),
            pl.BlockSpec((d, XATT_DIM), const2),
            pl.BlockSpec((None, N_MEM, XATT_DIM), lambda b, t: (b, 0, 0)),
            pl.BlockSpec((None, N_MEM, XATT_DIM), lambda b, t: (b, 0, 0)),
            pl.BlockSpec((XATT_DIM, d), const2),
            pl.BlockSpec((1, d), const2),
        ],
        out_specs=out_specs,
        out_shape=out_shape,
        scratch_shapes=[pltpu.VMEM((tm, d), BF16), pltpu.VMEM((tm, LANES), F32)],
        compiler_params=_params(("arbitrary", "arbitrary")),
        name="cross_attn",
    )(x, nx.reshape(1, d), wq, mem_k, mem_v, wo, nf.reshape(1, d))


def _pad_state_rows(s):
    return jnp.pad(s.astype(F32), ((0, 0), (SUBLANES - (CONV_W - 1), 0), (0, 0)))


def _rope_tables(pos):
    half = RET_DK // 2
    inv = ROPE_BASE ** (-jnp.arange(half, dtype=F32) / half)
    ang = pos[:, None] * inv[None, :]
    cos, sin = jnp.cos(ang), jnp.sin(ang)
    cq = jnp.concatenate([cos, cos], axis=-1)
    sq = jnp.concatenate([-sin, sin], axis=-1)
    ks = RET_DK ** -0.5
    return cq, sq, cq * ks, sq * ks


def _layer_params(l, norm_mix, w_in, lru_conv_w, lru_conv_b, lru_wa, lru_ba, lru_wx, lru_bx, lru_lambda, lru_norm,
                  ret_norm, ssd_conv_w, ssd_conv_b, ssd_dt_bias, ssd_a_log, ssd_d, ssd_norm, w_out, norm_xattn,
                  norm_mem, w_q_mem, w_k_mem, w_v_mem, w_o_mem):
    w_in, w_dt_t = w_in
    w_dt = w_dt_t[l]
    w_q, w_kv, w_o = _cast_xattn_weights(w_q_mem, w_k_mem, w_v_mem, w_o_mem, l)
    row = lambda v: v.reshape(1, -1).astype(F32)
    pad_lanes = lambda v: jnp.pad(v.astype(F32), (0, LANES - v.shape[0])).reshape(1, LANES)
    return dict(
        norm_mix=norm_mix[l], w_in=w_in, layer=l, w_dt=w_dt,
        lru_conv_w=lru_conv_w[l], lru_conv_b=row(lru_conv_b[l]), lru_wa=lru_wa[l].astype(BF16), lru_ba=row(lru_ba[l]),
        lru_wx=lru_wx[l].astype(BF16), lru_bx=row(lru_bx[l]), lru_lambda=row(lru_lambda[l]), lru_norm=row(lru_norm[l]),
        ret_norm=row(ret_norm[l]),
        ssd_conv_w=ssd_conv_w[l], ssd_conv_b=row(ssd_conv_b[l]), ssd_dt_bias=pad_lanes(ssd_dt_bias[l]),
        ssd_a_log=pad_lanes(ssd_a_log[l]), ssd_d_exp=row(jnp.repeat(ssd_d[l], SSD_HEADDIM)), ssd_norm=row(ssd_norm[l]),
        w_out=_cast_layer_bf16(w_out, l), norm_xattn=norm_xattn[l], norm_mem=norm_mem[l],
        w_q=w_q, w_kv=w_kv, w_o=w_o)


def _mixer_sublayer(x, hn, rope, lru_h0, lru_conv0, ret_s0, ssd_s0, ssd_conv0, ret_states, ssd_states, p, *,
                    batch, seq):
    layer = p["layer"]
    proj, dt_raw = _in_proj(hn, p["w_in"], layer, p["w_dt"], n=OFF_DT, tm=TM_IN_PROJ, tn=TN_IN_PROJ)
    c = min(CHUNK, seq)
    (lru_out, lru_h1, lru_ct), (ssd_out, ssd_states, ssd_ct) = _run_mixers(
        [_lru_parts(proj, _pad_state_rows(lru_conv0), lru_h0.reshape(batch, 1, LRU_WIDTH), p,
                    batch=batch, seq=seq, tb=c),
         _ssd_parts(proj, dt_raw, _pad_state_rows(ssd_conv0), ssd_s0, p, ssd_states, layer,
                    batch=batch, seq=seq, c=c)],
        batch=batch, nt=seq // c)
    c_ret = min(CHUNK_RET, seq)
    ((ret_out, ret_states),) = _run_mixers(
        [_ret_parts(proj, ret_s0, rope, p, ret_states, layer, batch=batch, seq=seq, c=c_ret)],
        batch=batch, nt=seq // c_ret)
    x1 = _out_proj(x, lru_out, ret_out, ssd_out, p["w_out"], tm=TM_OUT, tn=TN_OUT)
    tail = SUBLANES - (CONV_W - 1)
    small_states = (lru_h1.reshape(batch, LRU_WIDTH), lru_ct[:, tail:, :], ssd_ct[:, tail:, :])
    return x1, small_states, ret_states, ssd_states


def kernel(x_prompt, x_sample, mem_prompt, state_lru_h, state_lru_conv, state_ret, state_ssd, state_ssd_conv, cache_mem_k, cache_mem_v, norm_mix, w_in, lru_conv_w, lru_conv_b, lru_wa, lru_ba, lru_wx, lru_bx, lru_lambda, lru_norm, ret_norm, ssd_conv_w, ssd_conv_b, ssd_dt_bias, ssd_a_log, ssd_d, ssd_norm, w_out, norm_xattn, norm_mem, w_q_mem, w_k_mem, w_v_mem, w_o_mem, norm_final):
    bp, tp, d = x_prompt.shape
    bs, ts, _ = x_sample.shape
    depth = w_in.shape[0]
    assert tp % BF16_ROWS == 0 and ts % BF16_ROWS == 0
    rope_p = _rope_tables(jnp.arange(tp, dtype=F32))
    rope_s = _rope_tables(PAST_LEN + jnp.arange(ts, dtype=F32))
    z_lru_h = jnp.zeros((bp, LRU_WIDTH), F32)
    z_lru_conv = jnp.zeros((bp, CONV_W - 1, LRU_WIDTH), F32)
    z_ret = jnp.zeros((bp, RET_HEADS, RET_DK, RET_DV), F32)
    z_ssd = jnp.zeros((bp, SSD_HEADS, SSD_STATE, SSD_HEADDIM), F32)
    z_ssd_conv = jnp.zeros((bp, CONV_W - 1, SSD_CONV_DIM), F32)

    xp = x_prompt.reshape(bp * tp, d)
    xs = x_sample.reshape(bs * ts, d)
    mem = mem_prompt.reshape(bp * N_MEM, d)
    w_in_t = jnp.swapaxes(w_in, 1, 2)
    w_dt_t = jnp.pad(w_in_t[:, OFF_DT:, :], ((0, 0), (0, LANES - SSD_HEADS), (0, 0))).astype(BF16)
    w_in = (_cast_stack_bf16(w_in_t, n_rows=OFF_DT, rows=CAST_ROWS_W_IN), w_dt_t)
    hn_p = _rmsnorm_bf16(xp, norm_mix[0], tm=TM_XATT)
    hn_s = _rmsnorm_bf16(xs, norm_mix[0], tm=TM_XATT)
    st_p = [[] for _ in range(3)]
    st_s = [[] for _ in range(3)]
    ret_p = jnp.zeros((depth,) + z_ret.shape, F32)
    ssd_p = jnp.zeros((depth,) + z_ssd.shape, F32)
    ret_s = jnp.zeros(state_ret.shape, F32)
    ssd_s = jnp.zeros(state_ssd.shape, F32)
    mk_p, mv_p = [], []
    for l in range(depth):
        p = _layer_params(l, norm_mix, w_in, lru_conv_w, lru_conv_b, lru_wa, lru_ba, lru_wx, lru_bx, lru_lambda,
                          lru_norm, ret_norm, ssd_conv_w, ssd_conv_b, ssd_dt_bias, ssd_a_log, ssd_d, ssd_norm, w_out,
                          norm_xattn, norm_mem, w_q_mem, w_k_mem, w_v_mem, w_o_mem)
        final = l == depth - 1
        next_gain = norm_final if final else norm_mix[l + 1]
        xp, st, ret_p, ssd_p = _mixer_sublayer(xp, hn_p, rope_p, z_lru_h, z_lru_conv, z_ret, z_ssd, z_ssd_conv,
                                               ret_p, ssd_p, p, batch=bp, seq=tp)
        mkv = _norm_matmul(mem, p["norm_mem"], p["w_kv"], out_dtype=F32, tm=TM_PROJ, tn=TN_PROJ)
        mk = mkv[:, :XATT_DIM].reshape(bp, N_MEM, XATT_DIM)
        mv = mkv[:, XATT_DIM:].reshape(bp, N_MEM, XATT_DIM)
        res = _cross_attn(xp, mk, mv, p["norm_xattn"], p["w_q"], p["w_o"], next_gain, batch=bp, seq=tp, tm=TM_XATT,
                          final=final)
        xp, hn_p = (res, None) if final else res
        for acc, v in zip(st_p, st):
            acc.append(v)
        mk_p.append(mk.reshape(bp, N_MEM, XATT_HEADS, XATT_HD))
        mv_p.append(mv.reshape(bp, N_MEM, XATT_HEADS, XATT_HD))
        xs, st, ret_s, ssd_s = _mixer_sublayer(xs, hn_s, rope_s, state_lru_h[l], state_lru_conv[l], state_ret[l],
                                               state_ssd[l], state_ssd_conv[l], ret_s, ssd_s, p, batch=bs, seq=ts)
        res = _cross_attn(xs, cache_mem_k[l].reshape(bs, N_MEM, XATT_DIM), cache_mem_v[l].reshape(bs, N_MEM, XATT_DIM),
                          p["norm_xattn"], p["w_q"], p["w_o"], next_gain, batch=bs, seq=ts, tm=TM_XATT, final=final)
        xs, hn_s = (res, None) if final else res
        for acc, v in zip(st_s, st):
            acc.append(v)

    y_prompt = xp.reshape(bp, tp, d)
    y_sample = xs.reshape(bs, ts, d)
    return (y_prompt, y_sample,
            jnp.stack(st_p[0]), jnp.stack(st_p[1]), ret_p, ssd_p, jnp.stack(st_p[2]),
            jnp.stack(mk_p), jnp.stack(mv_p),
            jnp.stack(st_s[0]), jnp.stack(st_s[1]), ret_s, ssd_s, jnp.stack(st_s[2]))
```

```python
import functools
import math
from typing import Callable, NamedTuple

import numpy as np
import jax
import jax.numpy as jnp
from jax import lax
from jax.experimental import pallas as pl
from jax.experimental.pallas import tpu as pltpu

F32 = jnp.float32
BF16 = jnp.bfloat16

EPS = 1e-6
CONV_W = 4
PAST_LEN = 4096
N_MEM = 256
LRU_WIDTH = 1024
LRU_BLOCKS = 4
LRU_BLOCK = LRU_WIDTH // LRU_BLOCKS
LRU_C = 8.0
RET_HEADS = 4
RET_DK = 128
RET_DV = 256
RET_QK = RET_HEADS * RET_DK
RET_V = RET_HEADS * RET_DV
ROPE_BASE = 10000.0
SSD_INNER = 2048
SSD_HEADDIM = 64
SSD_HEADS = SSD_INNER // SSD_HEADDIM
SSD_STATE = 128
SSD_GROUPS = 4
SSD_GROUP_HEADS = SSD_HEADS // SSD_GROUPS
SSD_GROUP_WIDTH = SSD_GROUP_HEADS * SSD_HEADDIM
SSD_BC = SSD_GROUPS * SSD_STATE
SSD_CONV_DIM = SSD_INNER + 2 * SSD_BC
XATT_HEADS = 4
XATT_HD = 128
XATT_DIM = XATT_HEADS * XATT_HD

SUBLANES = 8
LANES = 128
BF16_ROWS = 2 * SUBLANES
VMEM_LIMIT_BYTES = 56 * 1024 * 1024

OFF_LX, OFF_LG, OFF_RQ, OFF_RK, OFF_RV, OFF_RG, OFF_SZ, OFF_SX = 0, 1024, 2048, 2560, 3072, 4096, 5120, 7168
OFF_SB = OFF_SX + SSD_INNER
OFF_SC = OFF_SB + SSD_BC
OFF_DT = OFF_SX + SSD_CONV_DIM

RET_LOG_GAMMA = tuple(math.log1p(-(2.0 ** (-5.0 - h))) for h in range(RET_HEADS))
LOG2E = math.log2(math.e)

TM_PROJ, TN_PROJ = 512, 1024
TM_IN_PROJ, TN_IN_PROJ = 1024, 1280
TM_OUT, TN_OUT = 1024, 1024
CAST_ROWS = 256
CAST_ROWS_W_IN = 512
TM_XATT = 256
CHUNK = 128
CHUNK_RET = 256


def _params(semantics):
    return pltpu.CompilerParams(dimension_semantics=semantics, vmem_limit_bytes=VMEM_LIMIT_BYTES)


def _sigmoid(x):
    return 1.0 / (1.0 + jnp.exp(-x))


def _silu(x):
    return x * _sigmoid(x)


def _softplus(x):
    return jnp.maximum(x, 0.0) + jnp.log1p(jnp.exp(-jnp.abs(x)))


def _dot(a, b):
    return jnp.dot(a, b, preferred_element_type=F32)


def _dot_nt(a, b):
    return lax.dot_general(a, b, (((1,), (1,)), ((), ())), preferred_element_type=F32)


def _dot_tn(a, b):
    return lax.dot_general(a, b, (((0,), (0,)), ((), ())), preferred_element_type=F32)


def _rmsnorm_rows_to(dst_ref, src_ref, gain_ref, part_ref, unroll=False):
    tm, d = src_ref.shape
    rc = BF16_ROWS
    lane_tiles = d // LANES

    def sum_squares(r, carry):
        rows = pl.ds(pl.multiple_of(r * rc, rc), rc)
        accs = [None] * min(4, lane_tiles)
        for j in range(lane_tiles):
            v = src_ref[rows, j * LANES:(j + 1) * LANES].astype(F32)
            k = j % len(accs)
            accs[k] = v * v if accs[k] is None else accs[k] + v * v
        while len(accs) > 1:
            accs = [a + b for a, b in zip(accs[0::2], accs[1::2])]
        part_ref[rows, :] = accs[0]
        return carry

    lax.fori_loop(0, tm // rc, sum_squares, 0, unroll=unroll)
    ms = jnp.sum(part_ref[...], axis=-1, keepdims=True) * (1.0 / d)
    part_ref[...] = jnp.broadcast_to(lax.rsqrt(ms + EPS), part_ref.shape)

    def scale(r, carry):
        rows = pl.ds(pl.multiple_of(r * rc, rc), rc)
        rs = part_ref[rows, :]
        for j in range(lane_tiles):
            sl = slice(j * LANES, (j + 1) * LANES)
            dst_ref[rows, sl] = (src_ref[rows, sl].astype(F32) * rs * gain_ref[:, sl]).astype(dst_ref.dtype)
        return carry

    lax.fori_loop(0, tm // rc, scale, 0, unroll=unroll)


def _cast_body(w_ref, o_ref):
    o_ref[...] = w_ref[...].astype(o_ref.dtype)


def _cast_layer_bf16(w_stack, l):
    _, r, c = w_stack.shape
    rows = min(CAST_ROWS, r)
    assert r % rows == 0
    return pl.pallas_call(
        _cast_body,
        grid=(r // rows,),
        in_specs=[pl.BlockSpec((None, rows, c), lambda i: (l, i, 0))],
        out_specs=pl.BlockSpec((rows, c), lambda i: (i, 0)),
        out_shape=jax.ShapeDtypeStruct((r, c), BF16),
        compiler_params=_params(("arbitrary",)),
        name="cast_bf16",
    )(w_stack)


def _cast_stack_bf16(w_stack, *, n_rows, rows):
    layers, r, c = w_stack.shape
    assert n_rows <= r and n_rows % rows == 0
    spec = pl.BlockSpec((None, rows, c), lambda l, i: (l, i, 0))
    return pl.pallas_call(
        _cast_body,
        grid=(layers, n_rows // rows),
        in_specs=[spec],
        out_specs=spec,
        out_shape=jax.ShapeDtypeStruct((layers, n_rows, c), BF16),
        compiler_params=_params(("arbitrary", "arbitrary")),
        name="cast_stack",
    )(w_stack)


def _cast_xattn_body(wq_ref, wk_ref, wv_ref, wo_ref, oq_ref, okv_ref, oo_ref):
    oq_ref[...] = wq_ref[...].astype(oq_ref.dtype)
    okv_ref[:, 0:XATT_DIM] = wk_ref[...].astype(okv_ref.dtype)
    okv_ref[:, XATT_DIM:2 * XATT_DIM] = wv_ref[...].astype(okv_ref.dtype)
    oo_ref[...] = wo_ref[...].astype(oo_ref.dtype)


def _cast_xattn_weights(w_q, w_k, w_v, w_o, l):
    _, d, n = w_q.shape
    steps = d // CAST_ROWS
    assert d % CAST_ROWS == 0 and n % steps == 0 and (n // steps) % BF16_ROWS == 0
    in_rows = lambda i: (l, i, 0)
    return pl.pallas_call(
        _cast_xattn_body,
        grid=(steps,),
        in_specs=[pl.BlockSpec((None, CAST_ROWS, n), in_rows), pl.BlockSpec((None, CAST_ROWS, n), in_rows),
                  pl.BlockSpec((None, CAST_ROWS, n), in_rows), pl.BlockSpec((None, n // steps, d), in_rows)],
        out_specs=[pl.BlockSpec((CAST_ROWS, n), lambda i: (i, 0)), pl.BlockSpec((CAST_ROWS, 2 * n), lambda i: (i, 0)),
                   pl.BlockSpec((n // steps, d), lambda i: (i, 0))],
        out_shape=[jax.ShapeDtypeStruct((d, n), BF16), jax.ShapeDtypeStruct((d, 2 * n), BF16),
                   jax.ShapeDtypeStruct((n, d), BF16)],
        compiler_params=_params(("arbitrary",)),
        name="cast_xattn",
    )(w_q, w_k, w_v, w_o)


def _norm_matmul_body(x_ref, g_ref, w_ref, o_ref, hn_ref, part_ref):
    @pl.when(pl.program_id(1) == 0)
    def _():
        _rmsnorm_rows_to(hn_ref, x_ref, g_ref, part_ref)

    o_ref[...] = _dot(hn_ref[...], w_ref[...]).astype(o_ref.dtype)


def _norm_matmul(x, g, w, *, out_dtype, tm, tn):
    m, k = x.shape
    n = w.shape[1]
    tm = min(tm, m)
    tn = min(tn, n)
    assert m % tm == 0 and n % tn == 0
    return pl.pallas_call(
        _norm_matmul_body,
        grid=(m // tm, n // tn),
        in_specs=[
            pl.BlockSpec((tm, k), lambda i, j: (i, 0)),
            pl.BlockSpec((1, k), lambda i, j: (0, 0)),
            pl.BlockSpec((k, tn), lambda i, j: (0, j)),
        ],
        out_specs=pl.BlockSpec((tm, tn), lambda i, j: (i, j)),
        out_shape=jax.ShapeDtypeStruct((m, n), out_dtype),
        scratch_shapes=[pltpu.VMEM((tm, k), BF16), pltpu.VMEM((tm, LANES), F32)],
        compiler_params=_params(("arbitrary", "arbitrary")),
        name="norm_matmul",
    )(x, g.reshape(1, k), w)


def _rmsnorm_body(x_ref, g_ref, o_ref, part_ref):
    _rmsnorm_rows_to(o_ref, x_ref, g_ref, part_ref)


def _rmsnorm_bf16(x, g, *, tm):
    m, k = x.shape
    tm = min(tm, m)
    assert m % tm == 0
    return pl.pallas_call(
        _rmsnorm_body,
        grid=(m // tm,),
        in_specs=[pl.BlockSpec((tm, k), lambda i: (i, 0)), pl.BlockSpec((1, k), lambda i: (0, 0))],
        out_specs=pl.BlockSpec((tm, k), lambda i: (i, 0)),
        out_shape=jax.ShapeDtypeStruct((m, k), BF16),
        scratch_shapes=[pltpu.VMEM((tm, LANES), F32)],
        compiler_params=_params(("arbitrary",)),
        name="rmsnorm_bf16",
    )(x, g.reshape(1, k))


def _in_proj_body(hn_ref, wt_ref, w2t_ref, o_ref, o2_ref):
    @pl.when(pl.program_id(1) == 0)
    def _():
        o2_ref[...] = _dot_nt(hn_ref[...], w2t_ref[...])

    o_ref[...] = _dot_nt(hn_ref[...], wt_ref[...]).astype(o_ref.dtype)


def _in_proj(hn, wt, l, w2t, *, n, tm, tn):
    m, k = hn.shape
    n2 = w2t.shape[0]
    tm = min(tm, m)
    assert m % tm == 0 and n % tn == 0 and n <= wt.shape[1]
    w, w2 = wt, w2t
    return pl.pallas_call(
        _in_proj_body,
        grid=(m // tm, n // tn),
        in_specs=[
            pl.BlockSpec((tm, k), lambda i, j: (i, 0)),
            pl.BlockSpec((None, tn, k), lambda i, j: (l, j, 0)),
            pl.BlockSpec((n2, k), lambda i, j: (0, 0)),
        ],
        out_specs=[pl.BlockSpec((tm, tn), lambda i, j: (i, j)), pl.BlockSpec((tm, n2), lambda i, j: (i, 0))],
        out_shape=[jax.ShapeDtypeStruct((m, n), BF16), jax.ShapeDtypeStruct((m, n2), F32)],
        compiler_params=_params(("arbitrary", "arbitrary")),
        name="in_proj",
    )(hn, w, w2)


def _out_proj_body(x_ref, a1_ref, a2_ref, a3_ref, w1_ref, w2_ref, w3_ref, o_ref):
    acc = _dot(a1_ref[...], w1_ref[...])
    acc = acc + _dot(a2_ref[...], w2_ref[...])
    acc = acc + _dot(a3_ref[...], w3_ref[...])
    o_ref[...] = x_ref[...] + acc


def _out_proj(x, lru_out, ret_out, ssd_out, w_out_bf16, *, tm, tn):
    m, n = x.shape
    tm = min(tm, m)
    tn = min(tn, n)
    assert m % tm == 0 and n % tn == 0
    return pl.pallas_call(
        _out_proj_body,
        grid=(m // tm, n // tn),
        in_specs=[
            pl.BlockSpec((tm, tn), lambda i, j: (i, j)),
            pl.BlockSpec((tm, LRU_WIDTH), lambda i, j: (i, 0)),
            pl.BlockSpec((tm, RET_V), lambda i, j: (i, 0)),
            pl.BlockSpec((tm, SSD_INNER), lambda i, j: (i, 0)),
            pl.BlockSpec((LRU_WIDTH, tn), lambda i, j: (0, j)),
            pl.BlockSpec((RET_V, tn), lambda i, j: (1, j)),
            pl.BlockSpec((SSD_INNER, tn), lambda i, j: (1, j)),
        ],
        out_specs=pl.BlockSpec((tm, tn), lambda i, j: (i, j)),
        out_shape=jax.ShapeDtypeStruct((m, n), F32),
        compiler_params=_params(("arbitrary", "arbitrary")),
        name="out_proj",
    )(x, lru_out, ret_out, ssd_out, w_out_bf16, w_out_bf16, w_out_bf16)


HIST_ROWS = BF16_ROWS
N_HIST = CONV_W - 1
HIST_HI, HIST_MID, HIST_LO = 0, N_HIST, SUBLANES
assert HIST_MID + N_HIST <= SUBLANES and HIST_LO + N_HIST <= HIST_ROWS


def _stage_rows(c):
    return -(-(HIST_ROWS + c) // LANES) * LANES


def _shift_matrix(c):
    s = np.zeros((N_HIST, c, _stage_rows(c)), np.float32)
    for k in range(N_HIST):
        back = k + 1
        for t in range(c):
            if t >= back:
                s[k, t, HIST_ROWS + t - back] = 1.0
            else:
                j = N_HIST + t - back
                s[k, t, HIST_HI + j] = s[k, t, HIST_MID + j] = s[k, t, HIST_LO + j] = 1.0
    return jnp.asarray(s.reshape(N_HIST * c, -1), dtype=BF16)


def _history_tile(tail8):
    r = pltpu.roll(tail8, N_HIST, 0)
    hi = r.astype(BF16).astype(F32)
    rest = r - hi
    mid = rest.astype(BF16).astype(F32)
    lo = rest - mid
    rows = lax.broadcasted_iota(jnp.int32, r.shape, 0)
    top = jnp.where(rows < N_HIST, hi, jnp.where(rows < HIST_MID + N_HIST, pltpu.roll(mid, HIST_MID, 0), 0.0))
    bot = jnp.where(rows < N_HIST, lo, 0.0)
    return jnp.concatenate([top, bot], axis=0).astype(BF16)


def _conv_from_stage(stage, shift_ref, x_f32, c, cw_ref, cb_ref):
    sh = _dot(shift_ref[...], stage[...])
    acc = sh[2 * c:3 * c] * cw_ref[0:1, :]
    acc = acc + sh[c:2 * c] * cw_ref[1:2, :]
    acc = acc + sh[0:c] * cw_ref[2:3, :]
    acc = acc + x_f32 * cw_ref[3:4, :]
    return cb_ref[...] + acc


class MixerParts(NamedTuple):
    body: Callable
    args: tuple
    in_specs: list
    out_specs: list
    out_shape: list
    scratch: list
    carried: dict = {}


def _mixers_body(*refs, parts, n_carried):
    n_in = sum(len(p.args) for p in parts)
    n_out = sum(len(p.out_specs) for p in parts)
    ins = refs[:n_in]
    outs = refs[n_in + n_carried:n_in + n_carried + n_out]
    scr = refs[n_in + n_carried + n_out:]
    i = o = s = 0
    for p in parts:
        ni, no, ns = len(p.args), len(p.out_specs), len(p.scratch)
        p.body(*ins[i:i + ni], *outs[o:o + no], *scr[s:s + ns])
        i, o, s = i + ni, o + no, s + ns


def _run_mixers(parts, *, batch, nt):
    flat = lambda field: [x for p in parts for x in getattr(p, field)]
    bodies = tuple(p._replace(args=tuple(None for _ in p.args), out_shape=[], carried={}) for p in parts)
    args, in_specs = flat("args"), flat("in_specs")
    aliases, o = {}, 0
    for p in parts:
        for k, buf in p.carried.items():
            aliases[len(args)] = o + k
            args.append(buf)
            in_specs.append(pl.BlockSpec(memory_space=pl.ANY))
        o += len(p.out_specs)
    outs = pl.pallas_call(
        functools.partial(_mixers_body, parts=bodies, n_carried=len(aliases)),
        grid=(batch, nt),
        in_specs=in_specs,
        out_specs=flat("out_specs"),
        out_shape=flat("out_shape"),
        scratch_shapes=flat("scratch"),
        input_output_aliases=aliases,
        compiler_params=_params(("arbitrary", "arbitrary")),
        name="mixers",
    )(*args)
    res, o = [], 0
    for p in parts:
        res.append(outs[o:o + len(p.out_specs)])
        o += len(p.out_specs)
    return res


def _lru_body(lxg_ref, conv0_ref, h0_ref, shift_ref, cw_ref, cb_ref, wa_ref, ba_ref, wx_ref, bx_ref, lam_ref,
              nrm_ref, out_ref, hlast_ref, ctail_ref, stage, hcar, a_scr, b_scr):
    tb = lxg_ref.shape[0]
    lx_ref = lxg_ref.at[:, 0:LRU_WIDTH]
    lg_ref = lxg_ref.at[:, LRU_WIDTH:2 * LRU_WIDTH]

    @pl.when(pl.program_id(1) == 0)
    def _():
        stage[...] = jnp.zeros(stage.shape, stage.dtype)
        stage[0:HIST_ROWS, :] = _history_tile(conv0_ref[...])
        hcar[...] = h0_ref[...]

    stage[pl.ds(HIST_ROWS, tb), :] = lx_ref[...]
    x_in = lx_ref[...].astype(F32)
    xc = _conv_from_stage(stage, shift_ref, x_in, tb, cw_ref, cb_ref)
    tail = x_in[tb - SUBLANES:tb]
    stage[0:HIST_ROWS, :] = _history_tile(tail)
    ctail_ref[...] = tail

    xb = xc.astype(BF16)
    ra, ri = [], []
    for n in range(LRU_BLOCKS):
        xn = xb[:, n * LRU_BLOCK:(n + 1) * LRU_BLOCK]
        ra.append(_dot(xn, wa_ref[n]))
        ri.append(_dot(xn, wx_ref[n]))
    r = _sigmoid(jnp.concatenate(ra, axis=-1) + ba_ref[...])
    i = _sigmoid(jnp.concatenate(ri, axis=-1) + bx_ref[...])
    log_a = r * ((-LRU_C) * _softplus(-lam_ref[...]))
    a = jnp.exp(log_a)
    th = jnp.tanh(log_a)
    b = jnp.sqrt(-2.0 * th / (1.0 - th)) * (i * xc)

    ng = tb // SUBLANES
    for j in range(LRU_WIDTH // LANES):
        a_scr[j] = a[:, j * LANES:(j + 1) * LANES]
        b_scr[j] = b[:, j * LANES:(j + 1) * LANES]
    for j in range(LRU_WIDTH // LANES):
        pa = a_scr[j, pl.ds(0, ng, stride=SUBLANES), :]
        pb = b_scr[j, pl.ds(0, ng, stride=SUBLANES), :]
        for r in range(1, SUBLANES):
            view = pl.ds(r, ng, stride=SUBLANES)
            ar = a_scr[j, view, :]
            pb = ar * pb + b_scr[j, view, :]
            pa = ar * pa
            a_scr[j, view, :] = pa
            b_scr[j, view, :] = pb
    a3 = jnp.concatenate([a_scr[j] for j in range(LRU_WIDTH // LANES)], axis=-1).reshape(ng, SUBLANES, LRU_WIDTH)
    b3 = jnp.concatenate([b_scr[j] for j in range(LRU_WIDTH // LANES)], axis=-1).reshape(ng, SUBLANES, LRU_WIDTH)
    hprev = hcar[...]
    hs = []
    for g in range(ng):
        hg = b3[g] + a3[g] * hprev
        hprev = hg[SUBLANES - 1:SUBLANES]
        hs.append(hg)
    h = jnp.concatenate(hs, axis=0)
    hcar[...] = hprev
    hlast_ref[...] = hprev

    ms = jnp.mean(h * h, axis=-1, keepdims=True)
    out = h * lax.rsqrt(ms + EPS) * nrm_ref[...]
    out_ref[...] = (out * _silu(lg_ref[...].astype(F32))).astype(out_ref.dtype)


def _lru_parts(proj, conv0, h0, p, *, batch, seq, tb):
    assert seq % tb == 0 and OFF_LG == OFF_LX + LRU_WIDTH and OFF_LX % (2 * LRU_WIDTH) == 0
    nt = seq // tb
    w = LRU_WIDTH
    row = lambda b, t: b * nt + t
    const2 = lambda b, t: (0, 0)
    shift = _shift_matrix(tb)
    return MixerParts(
        body=_lru_body,
        in_specs=[
            pl.BlockSpec((tb, 2 * w), lambda b, t: (row(b, t), OFF_LX // (2 * w))),
            pl.BlockSpec((None, SUBLANES, w), lambda b, t: (b, 0, 0)),
            pl.BlockSpec((None, 1, w), lambda b, t: (b, 0, 0)),
            pl.BlockSpec(shift.shape, const2),
            pl.BlockSpec((CONV_W, w), const2),
            pl.BlockSpec((1, w), const2),
            pl.BlockSpec((LRU_BLOCKS, LRU_BLOCK, LRU_BLOCK), lambda b, t: (0, 0, 0)),
            pl.BlockSpec((1, w), const2),
            pl.BlockSpec((LRU_BLOCKS, LRU_BLOCK, LRU_BLOCK), lambda b, t: (0, 0, 0)),
            pl.BlockSpec((1, w), const2),
            pl.BlockSpec((1, w), const2),
            pl.BlockSpec((1, w), const2),
        ],
        out_specs=[
            pl.BlockSpec((tb, w), lambda b, t: (row(b, t), 0)),
            pl.BlockSpec((None, 1, w), lambda b, t: (b, 0, 0)),
            pl.BlockSpec((None, SUBLANES, w), lambda b, t: (b, 0, 0)),
        ],
        out_shape=[
            jax.ShapeDtypeStruct((batch * seq, w), BF16),
            jax.ShapeDtypeStruct((batch, 1, w), F32),
            jax.ShapeDtypeStruct((batch, SUBLANES, w), F32),
        ],
        scratch=[pltpu.VMEM((_stage_rows(tb), w), BF16), pltpu.VMEM((1, w), F32),
                 pltpu.VMEM((w // LANES, tb, LANES), F32), pltpu.VMEM((w // LANES, tb, LANES), F32)],
        args=(proj, conv0, h0, shift, p["lru_conv_w"], p["lru_conv_b"], p["lru_wa"], p["lru_ba"], p["lru_wx"],
              p["lru_bx"], p["lru_lambda"], p["lru_norm"]))


def _ret_body(qk_ref, v_ref, g_ref, cq_ref, sq_ref, ck_ref, sk_ref, s0_ref, nrm_ref,
              out_ref, snew_ref, state, decay_tab, eg_tab, wt_tab):
    c = qk_ref.shape[0]
    q_ref = qk_ref.at[:, 0:RET_QK]
    k_ref = qk_ref.at[:, RET_QK:2 * RET_QK]

    @pl.when(jnp.logical_and(pl.program_id(0) == 0, pl.program_id(1) == 0))
    def _():
        ii = lax.broadcasted_iota(jnp.int32, (c, c), 0)
        jj = lax.broadcasted_iota(jnp.int32, (c, c), 1)
        dpos = jnp.maximum(ii - jj, 0).astype(F32)
        rpos = lax.broadcasted_iota(jnp.int32, (c, RET_DK), 0).astype(F32)
        for h in range(RET_HEADS):
            lg = RET_LOG_GAMMA[h]
            decay_tab[h] = jnp.where(ii >= jj, jnp.exp(dpos * lg), 0.0)
            eg_tab[h] = jnp.exp((rpos + 1.0) * lg)
            wt_tab[h] = jnp.exp((c - 1.0 - rpos) * lg)

    @pl.when(pl.program_id(1) == 0)
    def _():
        state[...] = s0_ref[...]

    cq, sq, ck, sk = cq_ref[...], sq_ref[...], ck_ref[...], sk_ref[...]
    outs = []
    for h in range(RET_HEADS):
        decay, e_g, w_tail = decay_tab[h], eg_tab[h], wt_tab[h]
        a_chunk = math.exp(c * RET_LOG_GAMMA[h])
        qh = q_ref[:, h * RET_DK:(h + 1) * RET_DK].astype(F32)
        kh = k_ref[:, h * RET_DK:(h + 1) * RET_DK].astype(F32)
        qr = qh * cq + pltpu.roll(qh, RET_DK // 2, 1) * sq
        kr = kh * ck + pltpu.roll(kh, RET_DK // 2, 1) * sk
        vh = v_ref[:, h * RET_DV:(h + 1) * RET_DV]
        sh = state[h]
        scores = _dot_nt(qr.astype(BF16), kr.astype(BF16)) * decay
        y = _dot(scores.astype(BF16), vh) + _dot((qr * e_g).astype(BF16), sh.astype(BF16))
        state[h] = a_chunk * sh + _dot_tn((kr * w_tail).astype(BF16), vh)
        mu = jnp.mean(y, axis=-1, keepdims=True)
        d = y - mu
        var = jnp.mean(d * d, axis=-1, keepdims=True)
        outs.append(d * lax.rsqrt(var + EPS))
    ro = jnp.concatenate(outs, axis=-1) * nrm_ref[...]
    out_ref[...] = (ro * _silu(g_ref[...].astype(F32))).astype(out_ref.dtype)
    snew_ref[...] = state[...]


def _ret_parts(proj, s0, rope, p, states, layer, *, batch, seq, c):
    assert seq % c == 0 and states.shape[1:] == (batch, RET_HEADS, RET_DK, RET_DV)
    nt = seq // c
    row = lambda b, t: b * nt + t
    tab = pl.BlockSpec((c, RET_DK), lambda b, t: (t, 0))
    return MixerParts(
        body=_ret_body,
        in_specs=[
            pl.BlockSpec((c, 2 * RET_QK), lambda b, t: (row(b, t), OFF_RQ // (2 * RET_QK))),
            pl.BlockSpec((c, RET_V), lambda b, t: (row(b, t), OFF_RV // RET_V)),
            pl.BlockSpec((c, RET_V), lambda b, t: (row(b, t), OFF_RG // RET_V)),
            tab, tab, tab, tab,
            pl.BlockSpec((None, RET_HEADS, RET_DK, RET_DV), lambda b, t: (b, 0, 0, 0)),
            pl.BlockSpec((1, RET_V), lambda b, t: (0, 0)),
        ],
        out_specs=[
            pl.BlockSpec((c, RET_V), lambda b, t: (row(b, t), 0)),
            pl.BlockSpec((None, None, RET_HEADS, RET_DK, RET_DV), lambda b, t: (layer, b, 0, 0, 0)),
        ],
        out_shape=[
            jax.ShapeDtypeStruct((batch * seq, RET_V), BF16),
            jax.ShapeDtypeStruct(states.shape, F32),
        ],
        carried={1: states},
        scratch=[
            pltpu.VMEM((RET_HEADS, RET_DK, RET_DV), F32),
            pltpu.VMEM((RET_HEADS, c, c), F32),
            pltpu.VMEM((RET_HEADS, c, RET_DK), F32),
            pltpu.VMEM((RET_HEADS, c, RET_DK), F32),
        ],
        args=(proj, proj, proj, rope[0], rope[1], rope[2], rope[3], s0, p["ret_norm"]))


def _cumsum_rows(x):
    n = x.shape[0]
    rows = lax.broadcasted_iota(jnp.int32, x.shape, 0)
    s = 1
    while s < n:
        x = x + jnp.where(rows >= s, pltpu.roll(x, s, 0), 0.0)
        s *= 2
    return x


def _ssd_body(zxbc_ref, dt_ref, conv0_ref, s0_ref, shift_ref, cw_ref, cb_ref,
              dtb_ref, alog_ref, dexp_ref, nrm_ref, out_ref, snew_ref, ctail_ref, stage, state, xw_scr, u_scr, *,
              n_steps):
    c = zxbc_ref.shape[0]
    gh, hd, gw = SSD_GROUP_HEADS, SSD_HEADDIM, SSD_GROUP_WIDTH
    conv_in = lambda j: zxbc_ref[:, SSD_INNER + j * LANES:SSD_INNER + (j + 1) * LANES]

    @pl.when(pl.program_id(1) == 0)
    def _():
        stage[...] = jnp.zeros(stage.shape, stage.dtype)
        stage[0:HIST_ROWS, :] = _history_tile(conv0_ref[...])
        for g in range(SSD_GROUPS):
            state[g] = jnp.concatenate([s0_ref[g * gh + hh] for hh in range(gh)], axis=-1)

    stage[pl.ds(HIST_ROWS, c), :] = zxbc_ref[:, SSD_INNER:SSD_INNER + SSD_CONV_DIM]
    sh = _dot(shift_ref[...], stage[...])
    tail = stage[pl.ds(HIST_ROWS + c - BF16_ROWS, BF16_ROWS), :].astype(F32)[SUBLANES:]
    stage[0:HIST_ROWS, :] = _history_tile(tail)
    ctail_ref[...] = tail

    cw_half = 0.5 * cw_ref[...]
    cb_half = 0.5 * cb_ref[...]

    def conv_tile(x_bf16, j):
        sl = slice(j * LANES, (j + 1) * LANES)
        acc = sh[2 * c:3 * c, sl] * cw_half[0:1, sl]
        acc = acc + sh[c:2 * c, sl] * cw_half[1:2, sl]
        acc = acc + sh[0:c, sl] * cw_half[2:3, sl]
        acc = acc + x_bf16.astype(F32) * cw_half[3:4, sl]
        h = cb_half[:, sl] + acc
        return h + h * jnp.tanh(h)

    x_tiles = SSD_INNER // LANES
    bc_tiles = SSD_BC // LANES

    dt = _softplus(dt_ref[...] + dtb_ref[...])
    a_neg = -jnp.exp(alog_ref[...])
    g2 = _cumsum_rows(dt * a_neg) * LOG2E
    g2_t = g2.T
    lane_lo = lax.broadcasted_iota(jnp.int32, (c, LANES), 1) < hd
    ii = lax.broadcasted_iota(jnp.int32, (c, c), 0)
    jj = lax.broadcasted_iota(jnp.int32, (c, c), 1)
    tri = ii >= jj

    bgs, cbs, y_inter = [], [], []
    for g in range(SSD_GROUPS):
        bg = conv_tile(conv_in(x_tiles + g), x_tiles + g).astype(BF16)
        cg = conv_tile(conv_in(x_tiles + bc_tiles + g), x_tiles + bc_tiles + g).astype(BF16)
        bgs.append(bg)
        cbs.append(jnp.where(tri, _dot_nt(cg, bg), 0.0))
        y_inter.append(_dot(cg, state[g].astype(BF16)))

    ssq = None
    a_tiles = []
    for p in range(SSD_HEADS // 2):
        g, pp = divmod(p, gh // 2)
        sl = slice(p * LANES, (p + 1) * LANES)
        xs = conv_tile(conv_in(p), p)
        gcol = [jnp.broadcast_to(g2[:, h:h + 1], (c, LANES)) for h in (2 * p, 2 * p + 1)]
        dcol = [jnp.broadcast_to(dt[:, h:h + 1], (c, LANES)) for h in (2 * p, 2 * p + 1)]
        g_x = jnp.where(lane_lo, gcol[0], gcol[1])
        xdt = xs * jnp.where(lane_lo, dcol[0], dcol[1])
        g_last = g_x[c - 1:c, :]
        xw_scr[:, sl] = (xdt * jnp.exp2(g_last - g_x)).astype(BF16)
        a_tiles.append(jnp.exp2(g_last))
        ms = []
        for k, h in enumerate((2 * p, 2 * p + 1)):
            decay = jnp.exp2(jnp.minimum(gcol[k][:, 0:c] - g2_t[h:h + 1, :], 0.0))
            ms.append((cbs[g] * decay).astype(BF16))
        rhs = jnp.concatenate([jnp.where(lane_lo, xdt, 0.0), jnp.where(lane_lo, 0.0, xdt)], axis=0).astype(BF16)
        y = _dot(jnp.concatenate(ms, axis=1), rhs) + jnp.exp2(g_x) * y_inter[g][:, pp * LANES:(pp + 1) * LANES]
        so = y + dexp_ref[:, sl] * xs
        u = so * _silu(zxbc_ref[:, sl].astype(F32))
        ssq = u * u if ssq is None else ssq + u * u
        u_scr[:, sl] = u

    for g in range(SSD_GROUPS):
        gsl = slice(g * gw, (g + 1) * gw)
        a_row = jnp.concatenate(a_tiles[g * (gh // 2):(g + 1) * (gh // 2)], axis=-1)
        state[g] = a_row * state[g] + _dot_tn(bgs[g], xw_scr[:, gsl])

    rs = lax.rsqrt(jnp.sum(ssq, axis=-1, keepdims=True) * (1.0 / SSD_INNER) + EPS)
    out_ref[...] = (u_scr[...] * rs * nrm_ref[...]).astype(out_ref.dtype)

    @pl.when(pl.program_id(1) == n_steps - 1)
    def _():
        for g in range(SSD_GROUPS):
            sg = state[g]
            for hh in range(gh):
                snew_ref[g * gh + hh] = sg[:, hh * hd:(hh + 1) * hd]


def _ssd_parts(proj, dt_raw, conv0, s0, p, states, layer, *, batch, seq, c):
    assert seq % c == 0 and states.shape[1:] == (batch, SSD_HEADS, SSD_STATE, SSD_HEADDIM)
    nt = seq // c
    row = lambda b, t: b * nt + t
    const2 = lambda b, t: (0, 0)
    width = SSD_INNER + SSD_CONV_DIM
    assert OFF_SZ % width == 0 and OFF_SX == OFF_SZ + SSD_INNER
    shift = _shift_matrix(c)
    return MixerParts(
        body=functools.partial(_ssd_body, n_steps=nt),
        in_specs=[
            pl.BlockSpec((c, width), lambda b, t: (row(b, t), OFF_SZ // width)),
            pl.BlockSpec((c, LANES), lambda b, t: (row(b, t), 0)),
            pl.BlockSpec((None, SUBLANES, SSD_CONV_DIM), lambda b, t: (b, 0, 0)),
            pl.BlockSpec((None, SSD_HEADS, SSD_STATE, SSD_HEADDIM), lambda b, t: (b, 0, 0, 0)),
            pl.BlockSpec(shift.shape, const2),
            pl.BlockSpec((CONV_W, SSD_CONV_DIM), const2),
            pl.BlockSpec((1, SSD_CONV_DIM), const2),
            pl.BlockSpec((1, LANES), const2),
            pl.BlockSpec((1, LANES), const2),
            pl.BlockSpec((1, SSD_INNER), const2),
            pl.BlockSpec((1, SSD_INNER), const2),
        ],
        out_specs=[
            pl.BlockSpec((c, SSD_INNER), lambda b, t: (row(b, t), 0)),
            pl.BlockSpec((None, None, SSD_HEADS, SSD_STATE, SSD_HEADDIM), lambda b, t: (layer, b, 0, 0, 0)),
            pl.BlockSpec((None, SUBLANES, SSD_CONV_DIM), lambda b, t: (b, 0, 0)),
        ],
        out_shape=[
            jax.ShapeDtypeStruct((batch * seq, SSD_INNER), BF16),
            jax.ShapeDtypeStruct(states.shape, F32),
            jax.ShapeDtypeStruct((batch, SUBLANES, SSD_CONV_DIM), F32),
        ],
        carried={1: states},
        scratch=[
            pltpu.VMEM((_stage_rows(c), SSD_CONV_DIM), BF16),
            pltpu.VMEM((SSD_GROUPS, SSD_STATE, SSD_GROUP_WIDTH), F32),
            pltpu.VMEM((c, SSD_INNER), BF16),
            pltpu.VMEM((c, SSD_INNER), F32),
        ],
        args=(proj, dt_raw, conv0, s0, shift, p["ssd_conv_w"], p["ssd_conv_b"],
              p["ssd_dt_bias"], p["ssd_a_log"], p["ssd_d_exp"], p["ssd_norm"]))


def _xattn_body(x_ref, nx_ref, wq_ref, mk_ref, mv_ref, wo_ref, nf_ref, o_ref, *rest, final):
    if final:
        hn_ref, part_ref = rest
    else:
        hn_next_ref, hn_ref, part_ref = rest
    _rmsnorm_rows_to(hn_ref, x_ref, nx_ref, part_ref)
    q = _dot(hn_ref[...], wq_ref[...])
    scale = XATT_HD ** -0.5
    os = []
    for h in range(XATT_HEADS):
        sl = slice(h * XATT_HD, (h + 1) * XATT_HD)
        s = _dot_nt(q[:, sl].astype(BF16), mk_ref[:, sl].astype(BF16)) * scale
        mx = jnp.max(s, axis=-1, keepdims=True)
        pr = jnp.exp(s - mx)
        den = jnp.sum(pr, axis=-1, keepdims=True)
        os.append(_dot(pr.astype(BF16), mv_ref[:, sl].astype(BF16)) / den)
    o = jnp.concatenate(os, axis=-1).astype(BF16)
    o_ref[...] = x_ref[...] + _dot(o, wo_ref[...])
    if final:
        _rmsnorm_rows_to(o_ref, o_ref, nf_ref, part_ref)
    else:
        _rmsnorm_rows_to(hn_next_ref, o_ref, nf_ref, part_ref)


def _cross_attn(x, mem_k, mem_v, nx, wq, wo, nf, *, batch, seq, tm, final):
    d = x.shape[1]
    tm = min(tm, seq)
    nt = seq // tm
    const2 = lambda b, t: (0, 0)
    x_spec = pl.BlockSpec((tm, d), lambda b, t: (b * nt + t, 0))
    out_specs = x_spec if final else [x_spec, x_spec]
    out_shape = jax.ShapeDtypeStruct(x.shape, F32)
    if not final:
        out_shape = [out_shape, jax.ShapeDtypeStruct(x.shape, BF16)]
    return pl.pallas_call(
        functools.partial(_xattn_body, final=final),
        grid=(batch, nt),
        in_specs=[
            x_spec,
            pl.BlockSpec((1, d), const2),
            pl.BlockSpec((d, XATT_DIM), const2),
            pl.BlockSpec((None, N_MEM, XATT_DIM), lambda b, t: (b, 0, 0)),
            pl.BlockSpec((None, N_MEM, XATT_DIM), lambda b, t: (b, 0, 0)),
            pl.BlockSpec((XATT_DIM, d), const2),
            pl.BlockSpec((1, d), const2),
        ],
        out_specs=out_specs,
        out_shape=out_shape,
        scratch_shapes=[pltpu.VMEM((tm, d), BF16), pltpu.VMEM((tm, LANES), F32)],
        compiler_params=_params(("arbitrary", "arbitrary")),
        name="cross_attn",
    )(x, nx.reshape(1, d), wq, mem_k, mem_v, wo, nf.reshape(1, d))


def _pad_state_rows(s):
    return jnp.pad(s.astype(F32), ((0, 0), (SUBLANES - (CONV_W - 1), 0), (0, 0)))


def _rope_tables(pos):
    half = RET_DK // 2
    inv = ROPE_BASE ** (-jnp.arange(half, dtype=F32) / half)
    ang = pos[:, None] * inv[None, :]
    cos, sin = jnp.cos(ang), jnp.sin(ang)
    cq = jnp.concatenate([cos, cos], axis=-1)
    sq = jnp.concatenate([-sin, sin], axis=-1)
    ks = RET_DK ** -0.5
    return cq, sq, cq * ks, sq * ks


def _layer_params(l, norm_mix, w_in, lru_conv_w, lru_conv_b, lru_wa, lru_ba, lru_wx, lru_bx, lru_lambda, lru_norm,
                  ret_norm, ssd_conv_w, ssd_conv_b, ssd_dt_bias, ssd_a_log, ssd_d, ssd_norm, w_out, norm_xattn,
                  norm_mem, w_q_mem, w_k_mem, w_v_mem, w_o_mem):
    w_in, w_dt_t = w_in
    w_dt = w_dt_t[l]
    w_q, w_kv, w_o = _cast_xattn_weights(w_q_mem, w_k_mem, w_v_mem, w_o_mem, l)
    row = lambda v: v.reshape(1, -1).astype(F32)
    pad_lanes = lambda v: jnp.pad(v.astype(F32), (0, LANES - v.shape[0])).reshape(1, LANES)
    return dict(
        norm_mix=norm_mix[l], w_in=w_in, layer=l, w_dt=w_dt,
        lru_conv_w=lru_conv_w[l], lru_conv_b=row(lru_conv_b[l]), lru_wa=lru_wa[l].astype(BF16), lru_ba=row(lru_ba[l]),
        lru_wx=lru_wx[l].astype(BF16), lru_bx=row(lru_bx[l]), lru_lambda=row(lru_lambda[l]), lru_norm=row(lru_norm[l]),
        ret_norm=row(ret_norm[l]),
        ssd_conv_w=ssd_conv_w[l], ssd_conv_b=row(ssd_conv_b[l]), ssd_dt_bias=pad_lanes(ssd_dt_bias[l]),
        ssd_a_log=pad_lanes(ssd_a_log[l]), ssd_d_exp=row(jnp.repeat(ssd_d[l], SSD_HEADDIM)), ssd_norm=row(ssd_norm[l]),
        w_out=_cast_layer_bf16(w_out, l), norm_xattn=norm_xattn[l], norm_mem=norm_mem[l],
        w_q=w_q, w_kv=w_kv, w_o=w_o)


def _mixer_sublayer(x, hn, rope, lru_h0, lru_conv0, ret_s0, ssd_s0, ssd_conv0, ret_states, ssd_states, p, *,
                    batch, seq):
    layer = p["layer"]
    proj, dt_raw = _in_proj(hn, p["w_in"], layer, p["w_dt"], n=OFF_DT, tm=TM_IN_PROJ, tn=TN_IN_PROJ)
    c = min(CHUNK, seq)
    (lru_out, lru_h1, lru_ct), (ssd_out, ssd_states, ssd_ct) = _run_mixers(
        [_lru_parts(proj, _pad_state_rows(lru_conv0), lru_h0.reshape(batch, 1, LRU_WIDTH), p,
                    batch=batch, seq=seq, tb=c),
         _ssd_parts(proj, dt_raw, _pad_state_rows(ssd_conv0), ssd_s0, p, ssd_states, layer,
                    batch=batch, seq=seq, c=c)],
        batch=batch, nt=seq // c)
    c_ret = min(CHUNK_RET, seq)
    ((ret_out, ret_states),) = _run_mixers(
        [_ret_parts(proj, ret_s0, rope, p, ret_states, layer, batch=batch, seq=seq, c=c_ret)],
        batch=batch, nt=seq // c_ret)
    x1 = _out_proj(x, lru_out, ret_out, ssd_out, p["w_out"], tm=TM_OUT, tn=TN_OUT)
    tail = SUBLANES - (CONV_W - 1)
    small_states = (lru_h1.reshape(batch, LRU_WIDTH), lru_ct[:, tail:, :], ssd_ct[:, tail:, :])
    return x1, small_states, ret_states, ssd_states


def kernel(x_prompt, x_sample, mem_prompt, state_lru_h, state_lru_conv, state_ret, state_ssd, state_ssd_conv, cache_mem_k, cache_mem_v, norm_mix, w_in, lru_conv_w, lru_conv_b, lru_wa, lru_ba, lru_wx, lru_bx, lru_lambda, lru_norm, ret_norm, ssd_conv_w, ssd_conv_b, ssd_dt_bias, ssd_a_log, ssd_d, ssd_norm, w_out, norm_xattn, norm_mem, w_q_mem, w_k_mem, w_v_mem, w_o_mem, norm_final):
    bp, tp, d = x_prompt.shape
    bs, ts, _ = x_sample.shape
    depth = w_in.shape[0]
    assert tp % BF16_ROWS == 0 and ts % BF16_ROWS == 0
    rope_p = _rope_tables(jnp.arange(tp, dtype=F32))
    rope_s = _rope_tables(PAST_LEN + jnp.arange(ts, dtype=F32))
    z_lru_h = jnp.zeros((bp, LRU_WIDTH), F32)
    z_lru_conv = jnp.zeros((bp, CONV_W - 1, LRU_WIDTH), F32)
    z_ret = jnp.zeros((bp, RET_HEADS, RET_DK, RET_DV), F32)
    z_ssd = jnp.zeros((bp, SSD_HEADS, SSD_STATE, SSD_HEADDIM), F32)
    z_ssd_conv = jnp.zeros((bp, CONV_W - 1, SSD_CONV_DIM), F32)

    xp = x_prompt.reshape(bp * tp, d)
    xs = x_sample.reshape(bs * ts, d)
    mem = mem_prompt.reshape(bp * N_MEM, d)
    w_in_t = jnp.swapaxes(w_in, 1, 2)
    w_dt_t = jnp.pad(w_in_t[:, OFF_DT:, :], ((0, 0), (0, LANES - SSD_HEADS), (0, 0))).astype(BF16)
    w_in = (_cast_stack_bf16(w_in_t, n_rows=OFF_DT, rows=CAST_ROWS_W_IN), w_dt_t)
    hn_p = _rmsnorm_bf16(xp, norm_mix[0], tm=TM_XATT)
    hn_s = _rmsnorm_bf16(xs, norm_mix[0], tm=TM_XATT)
    st_p = [[] for _ in range(3)]
    st_s = [[] for _ in range(3)]
    ret_p = jnp.zeros((depth,) + z_ret.shape, F32)
    ssd_p = jnp.zeros((depth,) + z_ssd.shape, F32)
    ret_s = jnp.zeros(state_ret.shape, F32)
    ssd_s = jnp.zeros(state_ssd.shape, F32)
    mk_p, mv_p = [], []
    for l in range(depth):
        p = _layer_params(l, norm_mix, w_in, lru_conv_w, lru_conv_b, lru_wa, lru_ba, lru_wx, lru_bx, lru_lambda,
                          lru_norm, ret_norm, ssd_conv_w, ssd_conv_b, ssd_dt_bias, ssd_a_log, ssd_d, ssd_norm, w_out,
                          norm_xattn, norm_mem, w_q_mem, w_k_mem, w_v_mem, w_o_mem)
        final = l == depth - 1
        next_gain = norm_final if final else norm_mix[l + 1]
        xp, st, ret_p, ssd_p = _mixer_sublayer(xp, hn_p, rope_p, z_lru_h, z_lru_conv, z_ret, z_ssd, z_ssd_conv,
                                               ret_p, ssd_p, p, batch=bp, seq=tp)
        mkv = _norm_matmul(mem, p["norm_mem"], p["w_kv"], out_dtype=F32, tm=TM_PROJ, tn=TN_PROJ)
        mk = mkv[:, :XATT_DIM].reshape(bp, N_MEM, XATT_DIM)
        mv = mkv[:, XATT_DIM:].reshape(bp, N_MEM, XATT_DIM)
        res = _cross_attn(xp, mk, mv, p["norm_xattn"], p["w_q"], p["w_o"], next_gain, batch=bp, seq=tp, tm=TM_XATT,
                          final=final)
        xp, hn_p = (res, None) if final else res
        for acc, v in zip(st_p, st):
            acc.append(v)
        mk_p.append(mk.reshape(bp, N_MEM, XATT_HEADS, XATT_HD))
        mv_p.append(mv.reshape(bp, N_MEM, XATT_HEADS, XATT_HD))
        xs, st, ret_s, ssd_s = _mixer_sublayer(xs, hn_s, rope_s, state_lru_h[l], state_lru_conv[l], state_ret[l],
                                               state_ssd[l], state_ssd_conv[l], ret_s, ssd_s, p, batch=bs, seq=ts)
        res = _cross_attn(xs, cache_mem_k[l].reshape(bs, N_MEM, XATT_DIM), cache_mem_v[l].reshape(bs, N_MEM, XATT_DIM),
                          p["norm_xattn"], p["w_q"], p["w_o"], next_gain, batch=bs, seq=ts, tm=TM_XATT, final=final)
        xs, hn_s = (res, None) if final else res
        for acc, v in zip(st_s, st):
            acc.append(v)

    y_prompt = xp.reshape(bp, tp, d)
    y_sample = xs.reshape(bs, ts, d)
    return (y_prompt, y_sample,
            jnp.stack(st_p[0]), jnp.stack(st_p[1]), ret_p, ssd_p, jnp.stack(st_p[2]),
            jnp.stack(mk_p), jnp.stack(mv_p),
            jnp.stack(st_s[0]), jnp.stack(st_s[1]), ret_s, ssd_s, jnp.stack(st_s[2]))
```

```python
import functools
import math
from typing import Callable, NamedTuple

import numpy as np
import jax
import jax.numpy as jnp
from jax import lax
from jax.experimental import pallas as pl
from jax.experimental.pallas import tpu as pltpu

F32 = jnp.float32
BF16 = jnp.bfloat16

EPS = 1e-6
CONV_W = 4
PAST_LEN = 4096
N_MEM = 256
LRU_WIDTH = 1024
LRU_BLOCKS = 4
LRU_BLOCK = LRU_WIDTH // LRU_BLOCKS
LRU_C = 8.0
RET_HEADS = 4
RET_DK = 128
RET_DV = 256
RET_QK = RET_HEADS * RET_DK
RET_V = RET_HEADS * RET_DV
ROPE_BASE = 10000.0
SSD_INNER = 2048
SSD_HEADDIM = 64
SSD_HEADS = SSD_INNER // SSD_HEADDIM
SSD_STATE = 128
SSD_GROUPS = 4
SSD_GROUP_HEADS = SSD_HEADS // SSD_GROUPS
SSD_GROUP_WIDTH = SSD_GROUP_HEADS * SSD_HEADDIM
SSD_BC = SSD_GROUPS * SSD_STATE
SSD_CONV_DIM = SSD_INNER + 2 * SSD_BC
XATT_HEADS = 4
XATT_HD = 128
XATT_DIM = XATT_HEADS * XATT_HD

SUBLANES = 8
LANES = 128
BF16_ROWS = 2 * SUBLANES
VMEM_LIMIT_BYTES = 56 * 1024 * 1024

OFF_LX, OFF_LG, OFF_RQ, OFF_RK, OFF_RV, OFF_RG, OFF_SZ, OFF_SX = 0, 1024, 2048, 2560, 3072, 4096, 5120, 7168
OFF_SB = OFF_SX + SSD_INNER
OFF_SC = OFF_SB + SSD_BC
OFF_DT = OFF_SX + SSD_CONV_DIM

RET_LOG_GAMMA = tuple(math.log1p(-(2.0 ** (-5.0 - h))) for h in range(RET_HEADS))
LOG2E = math.log2(math.e)

TM_PROJ, TN_PROJ = 512, 1024
TM_IN_PROJ, TN_IN_PROJ = 1024, 1280
TM_OUT, TN_OUT = 1024, 1024
CAST_ROWS = 256
CAST_ROWS_W_IN = 512
TM_XATT = 256
CHUNK = 128
CHUNK_RET = 256


def _params(semantics):
    return pltpu.CompilerParams(dimension_semantics=semantics, vmem_limit_bytes=VMEM_LIMIT_BYTES)


def _sigmoid(x):
    return 1.0 / (1.0 + jnp.exp(-x))


def _silu(x):
    return x * _sigmoid(x)


def _softplus(x):
    return jnp.maximum(x, 0.0) + jnp.log1p(jnp.exp(-jnp.abs(x)))


def _dot(a, b):
    return jnp.dot(a, b, preferred_element_type=F32)


def _dot_nt(a, b):
    return lax.dot_general(a, b, (((1,), (1,)), ((), ())), preferred_element_type=F32)


def _dot_tn(a, b):
    return lax.dot_general(a, b, (((0,), (0,)), ((), ())), preferred_element_type=F32)


def _rmsnorm_rows_to(dst_ref, src_ref, gain_ref, part_ref, unroll=False):
    tm, d = src_ref.shape
    rc = BF16_ROWS
    lane_tiles = d // LANES

    def sum_squares(r, carry):
        rows = pl.ds(pl.multiple_of(r * rc, rc), rc)
        accs = [None] * min(4, lane_tiles)
        for j in range(lane_tiles):
            v = src_ref[rows, j * LANES:(j + 1) * LANES].astype(F32)
            k = j % len(accs)
            accs[k] = v * v if accs[k] is None else accs[k] + v * v
        while len(accs) > 1:
            accs = [a + b for a, b in zip(accs[0::2], accs[1::2])]
        part_ref[rows, :] = accs[0]
        return carry

    lax.fori_loop(0, tm // rc, sum_squares, 0, unroll=unroll)
    ms = jnp.sum(part_ref[...], axis=-1, keepdims=True) * (1.0 / d)
    part_ref[...] = jnp.broadcast_to(lax.rsqrt(ms + EPS), part_ref.shape)

    def scale(r, carry):
        rows = pl.ds(pl.multiple_of(r * rc, rc), rc)
        rs = part_ref[rows, :]
        for j in range(lane_tiles):
            sl = slice(j * LANES, (j + 1) * LANES)
            dst_ref[rows, sl] = (src_ref[rows, sl].astype(F32) * rs * gain_ref[:, sl]).astype(dst_ref.dtype)
        return carry

    lax.fori_loop(0, tm // rc, scale, 0, unroll=unroll)


def _cast_body(w_ref, o_ref):
    o_ref[...] = w_ref[...].astype(o_ref.dtype)


def _cast_layer_bf16(w_stack, l):
    _, r, c = w_stack.shape
    rows = min(CAST_ROWS, r)
    assert r % rows == 0
    return pl.pallas_call(
        _cast_body,
        grid=(r // rows,),
        in_specs=[pl.BlockSpec((None, rows, c), lambda i: (l, i, 0))],
        out_specs=pl.BlockSpec((rows, c), lambda i: (i, 0)),
        out_shape=jax.ShapeDtypeStruct((r, c), BF16),
        compiler_params=_params(("arbitrary",)),
        name="cast_bf16",
    )(w_stack)


def _cast_stack_bf16(w_stack, *, n_rows, rows):
    layers, r, c = w_stack.shape
    assert n_rows <= r and n_rows % rows == 0
    spec = pl.BlockSpec((None, rows, c), lambda l, i: (l, i, 0))
    return pl.pallas_call(
        _cast_body,
        grid=(layers, n_rows // rows),
        in_specs=[spec],
        out_specs=spec,
        out_shape=jax.ShapeDtypeStruct((layers, n_rows, c), BF16),
        compiler_params=_params(("arbitrary", "arbitrary")),
        name="cast_stack",
    )(w_stack)


def _cast_xattn_body(wq_ref, wk_ref, wv_ref, wo_ref, oq_ref, okv_ref, oo_ref):
    oq_ref[...] = wq_ref[...].astype(oq_ref.dtype)
    okv_ref[:, 0:XATT_DIM] = wk_ref[...].astype(okv_ref.dtype)
    okv_ref[:, XATT_DIM:2 * XATT_DIM] = wv_ref[...].astype(okv_ref.dtype)
    oo_ref[...] = wo_ref[...].astype(oo_ref.dtype)


def _cast_xattn_weights(w_q, w_k, w_v, w_o, l):
    _, d, n = w_q.shape
    steps = d // CAST_ROWS
    assert d % CAST_ROWS == 0 and n % steps == 0 and (n // steps) % BF16_ROWS == 0
    in_rows = lambda i: (l, i, 0)
    return pl.pallas_call(
        _cast_xattn_body,
        grid=(steps,),
        in_specs=[pl.BlockSpec((None, CAST_ROWS, n), in_rows), pl.BlockSpec((None, CAST_ROWS, n), in_rows),
                  pl.BlockSpec((None, CAST_ROWS, n), in_rows), pl.BlockSpec((None, n // steps, d), in_rows)],
        out_specs=[pl.BlockSpec((CAST_ROWS, n), lambda i: (i, 0)), pl.BlockSpec((CAST_ROWS, 2 * n), lambda i: (i, 0)),
                   pl.BlockSpec((n // steps, d), lambda i: (i, 0))],
        out_shape=[jax.ShapeDtypeStruct((d, n), BF16), jax.ShapeDtypeStruct((d, 2 * n), BF16),
                   jax.ShapeDtypeStruct((n, d), BF16)],
        compiler_params=_params(("arbitrary",)),
        name="cast_xattn",
    )(w_q, w_k, w_v, w_o)


def _norm_matmul_body(x_ref, g_ref, w_ref, o_ref, hn_ref, part_ref):
    @pl.when(pl.program_id(1) == 0)
    def _():
        _rmsnorm_rows_to(hn_ref, x_ref, g_ref, part_ref)

    o_ref[...] = _dot(hn_ref[...], w_ref[...]).astype(o_ref.dtype)


def _norm_matmul(x, g, w, *, out_dtype, tm, tn):
    m, k = x.shape
    n = w.shape[1]
    tm = min(tm, m)
    tn = min(tn, n)
    assert m % tm == 0 and n % tn == 0
    return pl.pallas_call(
        _norm_matmul_body,
        grid=(m // tm, n // tn),
        in_specs=[
            pl.BlockSpec((tm, k), lambda i, j: (i, 0)),
            pl.BlockSpec((1, k), lambda i, j: (0, 0)),
            pl.BlockSpec((k, tn), lambda i, j: (0, j)),
        ],
        out_specs=pl.BlockSpec((tm, tn), lambda i, j: (i, j)),
        out_shape=jax.ShapeDtypeStruct((m, n), out_dtype),
        scratch_shapes=[pltpu.VMEM((tm, k), BF16), pltpu.VMEM((tm, LANES), F32)],
        compiler_params=_params(("arbitrary", "arbitrary")),
        name="norm_matmul",
    )(x, g.reshape(1, k), w)


def _rmsnorm_body(x_ref, g_ref, o_ref, part_ref):
    _rmsnorm_rows_to(o_ref, x_ref, g_ref, part_ref)


def _rmsnorm_bf16(x, g, *, tm):
    m, k = x.shape
    tm = min(tm, m)
    assert m % tm == 0
    return pl.pallas_call(
        _rmsnorm_body,
        grid=(m // tm,),
        in_specs=[pl.BlockSpec((tm, k), lambda i: (i, 0)), pl.BlockSpec((1, k), lambda i: (0, 0))],
        out_specs=pl.BlockSpec((tm, k), lambda i: (i, 0)),
        out_shape=jax.ShapeDtypeStruct((m, k), BF16),
        scratch_shapes=[pltpu.VMEM((tm, LANES), F32)],
        compiler_params=_params(("arbitrary",)),
        name="rmsnorm_bf16",
    )(x, g.reshape(1, k))


def _in_proj_body(hn_ref, wt_ref, w2t_ref, o_ref, o2_ref):
    @pl.when(pl.program_id(1) == 0)
    def _():
        o2_ref[...] = _dot_nt(hn_ref[...], w2t_ref[...])

    o_ref[...] = _dot_nt(hn_ref[...], wt_ref[...]).astype(o_ref.dtype)


def _in_proj(hn, wt, l, w2t, *, n, tm, tn):
    m, k = hn.shape
    n2 = w2t.shape[0]
    tm = min(tm, m)
    assert m % tm == 0 and n % tn == 0 and n <= wt.shape[1]
    w, w2 = wt, w2t
    return pl.pallas_call(
        _in_proj_body,
        grid=(m // tm, n // tn),
        in_specs=[
            pl.BlockSpec((tm, k), lambda i, j: (i, 0)),
            pl.BlockSpec((None, tn, k), lambda i, j: (l, j, 0)),
            pl.BlockSpec((n2, k), lambda i, j: (0, 0)),
        ],
        out_specs=[pl.BlockSpec((tm, tn), lambda i, j: (i, j)), pl.BlockSpec((tm, n2), lambda i, j: (i, 0))],
        out_shape=[jax.ShapeDtypeStruct((m, n), BF16), jax.ShapeDtypeStruct((m, n2), F32)],
        compiler_params=_params(("arbitrary", "arbitrary")),
        name="in_proj",
    )(hn, w, w2)


def _out_proj_body(x_ref, a1_ref, a2_ref, a3_ref, w1_ref, w2_ref, w3_ref, o_ref):
    acc = _dot(a1_ref[...], w1_ref[...])
    acc = acc + _dot(a2_ref[...], w2_ref[...])
    acc = acc + _dot(a3_ref[...], w3_ref[...])
    o_ref[...] = x_ref[...] + acc


def _out_proj(x, lru_out, ret_out, ssd_out, w_out_bf16, *, tm, tn):
    m, n = x.shape
    tm = min(tm, m)
    tn = min(tn, n)
    assert m % tm == 0 and n % tn == 0
    return pl.pallas_call(
        _out_proj_body,
        grid=(m // tm, n // tn),
        in_specs=[
            pl.BlockSpec((tm, tn), lambda i, j: (i, j)),
            pl.BlockSpec((tm, LRU_WIDTH), lambda i, j: (i, 0)),
            pl.BlockSpec((tm, RET_V), lambda i, j: (i, 0)),
            pl.BlockSpec((tm, SSD_INNER), lambda i, j: (i, 0)),
            pl.BlockSpec((LRU_WIDTH, tn), lambda i, j: (0, j)),
            pl.BlockSpec((RET_V, tn), lambda i, j: (1, j)),
            pl.BlockSpec((SSD_INNER, tn), lambda i, j: (1, j)),
        ],
        out_specs=pl.BlockSpec((tm, tn), lambda i, j: (i, j)),
        out_shape=jax.ShapeDtypeStruct((m, n), F32),
        compiler_params=_params(("arbitrary", "arbitrary")),
        name="out_proj",
    )(x, lru_out, ret_out, ssd_out, w_out_bf16, w_out_bf16, w_out_bf16)


HIST_ROWS = BF16_ROWS
N_HIST = CONV_W - 1
HIST_HI, HIST_MID, HIST_LO = 0, N_HIST, SUBLANES
assert HIST_MID + N_HIST <= SUBLANES and HIST_LO + N_HIST <= HIST_ROWS


def _stage_rows(c):
    return -(-(HIST_ROWS + c) // LANES) * LANES


def _shift_matrix(c):
    s = np.zeros((N_HIST, c, _stage_rows(c)), np.float32)
    for k in range(N_HIST):
        back = k + 1
        for t in range(c):
            if t >= back:
                s[k, t, HIST_ROWS + t - back] = 1.0
            else:
                j = N_HIST + t - back
                s[k, t, HIST_HI + j] = s[k, t, HIST_MID + j] = s[k, t, HIST_LO + j] = 1.0
    return jnp.asarray(s.reshape(N_HIST * c, -1), dtype=BF16)


def _history_tile(tail8):
    r = pltpu.roll(tail8, N_HIST, 0)
    hi = r.astype(BF16).astype(F32)
    rest = r - hi
    mid = rest.astype(BF16).astype(F32)
    lo = rest - mid
    rows = lax.broadcasted_iota(jnp.int32, r.shape, 0)
    top = jnp.where(rows < N_HIST, hi, jnp.where(rows < HIST_MID + N_HIST, pltpu.roll(mid, HIST_MID, 0), 0.0))
    bot = jnp.where(rows < N_HIST, lo, 0.0)
    return jnp.concatenate([top, bot], axis=0).astype(BF16)


def _conv_from_stage(stage, shift_ref, x_f32, c, cw_ref, cb_ref):
    sh = _dot(shift_ref[...], stage[...])
    acc = sh[2 * c:3 * c] * cw_ref[0:1, :]
    acc = acc + sh[c:2 * c] * cw_ref[1:2, :]
    acc = acc + sh[0:c] * cw_ref[2:3, :]
    acc = acc + x_f32 * cw_ref[3:4, :]
    return cb_ref[...] + acc


class MixerParts(NamedTuple):
    body: Callable
    args: tuple
    in_specs: list
    out_specs: list
    out_shape: list
    scratch: list
    carried: dict = {}


def _mixers_body(*refs, parts, n_carried):
    n_in = sum(len(p.args) for p in parts)
    n_out = sum(len(p.out_specs) for p in parts)
    ins = refs[:n_in]
    outs = refs[n_in + n_carried:n_in + n_carried + n_out]
    scr = refs[n_in + n_carried + n_out:]
    i = o = s = 0
    for p in parts:
        ni, no, ns = len(p.args), len(p.out_specs), len(p.scratch)
        p.body(*ins[i:i + ni], *outs[o:o + no], *scr[s:s + ns])
        i, o, s = i + ni, o + no, s + ns


def _run_mixers(parts, *, batch, nt):
    flat = lambda field: [x for p in parts for x in getattr(p, field)]
    bodies = tuple(p._replace(args=tuple(None for _ in p.args), out_shape=[], carried={}) for p in parts)
    args, in_specs = flat("args"), flat("in_specs")
    aliases, o = {}, 0
    for p in parts:
        for k, buf in p.carried.items():
            aliases[len(args)] = o + k
            args.append(buf)
            in_specs.append(pl.BlockSpec(memory_space=pl.ANY))
        o += len(p.out_specs)
    outs = pl.pallas_call(
        functools.partial(_mixers_body, parts=bodies, n_carried=len(aliases)),
        grid=(batch, nt),
        in_specs=in_specs,
        out_specs=flat("out_specs"),
        out_shape=flat("out_shape"),
        scratch_shapes=flat("scratch"),
        input_output_aliases=aliases,
        compiler_params=_params(("arbitrary", "arbitrary")),
        name="mixers",
    )(*args)
    res, o = [], 0
    for p in parts:
        res.append(outs[o:o + len(p.out_specs)])
        o += len(p.out_specs)
    return res


def _lru_body(lxg_ref, conv0_ref, h0_ref, shift_ref, cw_ref, cb_ref, wa_ref, ba_ref, wx_ref, bx_ref, lam_ref,
              nrm_ref, out_ref, hlast_ref, ctail_ref, stage, hcar, a_scr, b_scr):
    tb = lxg_ref.shape[0]
    lx_ref = lxg_ref.at[:, 0:LRU_WIDTH]
    lg_ref = lxg_ref.at[:, LRU_WIDTH:2 * LRU_WIDTH]

    @pl.when(pl.program_id(1) == 0)
    def _():
        stage[...] = jnp.zeros(stage.shape, stage.dtype)
        stage[0:HIST_ROWS, :] = _history_tile(conv0_ref[...])
        hcar[...] = h0_ref[...]

    stage[pl.ds(HIST_ROWS, tb), :] = lx_ref[...]
    x_in = lx_ref[...].astype(F32)
    xc = _conv_from_stage(stage, shift_ref, x_in, tb, cw_ref, cb_ref)
    tail = x_in[tb - SUBLANES:tb]
    stage[0:HIST_ROWS, :] = _history_tile(tail)
    ctail_ref[...] = tail

    xb = xc.astype(BF16)
    ra, ri = [], []
    for n in range(LRU_BLOCKS):
        xn = xb[:, n * LRU_BLOCK:(n + 1) * LRU_BLOCK]
        ra.append(_dot(xn, wa_ref[n]))
        ri.append(_dot(xn, wx_ref[n]))
    r = _sigmoid(jnp.concatenate(ra, axis=-1) + ba_ref[...])
    i = _sigmoid(jnp.concatenate(ri, axis=-1) + bx_ref[...])
    log_a = r * ((-LRU_C) * _softplus(-lam_ref[...]))
    a = jnp.exp(log_a)
    th = jnp.tanh(log_a)
    b = jnp.sqrt(-2.0 * th / (1.0 - th)) * (i * xc)

    ng = tb // SUBLANES
    for j in range(LRU_WIDTH // LANES):
        a_scr[j] = a[:, j * LANES:(j + 1) * LANES]
        b_scr[j] = b[:, j * LANES:(j + 1) * LANES]
    for j in range(LRU_WIDTH // LANES):
        pa = a_scr[j, pl.ds(0, ng, stride=SUBLANES), :]
        pb = b_scr[j, pl.ds(0, ng, stride=SUBLANES), :]
        for r in range(1, SUBLANES):
            view = pl.ds(r, ng, stride=SUBLANES)
            ar = a_scr[j, view, :]
            pb = ar * pb + b_scr[j, view, :]
            pa = ar * pa
            a_scr[j, view, :] = pa
            b_scr[j, view, :] = pb
    a3 = jnp.concatenate([a_scr[j] for j in range(LRU_WIDTH // LANES)], axis=-1).reshape(ng, SUBLANES, LRU_WIDTH)
    b3 = jnp.concatenate([b_scr[j] for j in range(LRU_WIDTH // LANES)], axis=-1).reshape(ng, SUBLANES, LRU_WIDTH)
    hprev = hcar[...]
    hs = []
    for g in range(ng):
        hg = b3[g] + a3[g] * hprev
        hprev = hg[SUBLANES - 1:SUBLANES]
        hs.append(hg)
    h = jnp.concatenate(hs, axis=0)
    hcar[...] = hprev
    hlast_ref[...] = hprev

    ms = jnp.mean(h * h, axis=-1, keepdims=True)
    out = h * lax.rsqrt(ms + EPS) * nrm_ref[...]
    out_ref[...] = (out * _silu(lg_ref[...].astype(F32))).astype(out_ref.dtype)


def _lru_parts(proj, conv0, h0, p, *, batch, seq, tb):
    assert seq % tb == 0 and OFF_LG == OFF_LX + LRU_WIDTH and OFF_LX % (2 * LRU_WIDTH) == 0
    nt = seq // tb
    w = LRU_WIDTH
    row = lambda b, t: b * nt + t
    const2 = lambda b, t: (0, 0)
    shift = _shift_matrix(tb)
    return MixerParts(
        body=_lru_body,
        in_specs=[
            pl.BlockSpec((tb, 2 * w), lambda b, t: (row(b, t), OFF_LX // (2 * w))),
            pl.BlockSpec((None, SUBLANES, w), lambda b, t: (b, 0, 0)),
            pl.BlockSpec((None, 1, w), lambda b, t: (b, 0, 0)),
            pl.BlockSpec(shift.shape, const2),
            pl.BlockSpec((CONV_W, w), const2),
            pl.BlockSpec((1, w), const2),
            pl.BlockSpec((LRU_BLOCKS, LRU_BLOCK, LRU_BLOCK), lambda b, t: (0, 0, 0)),
            pl.BlockSpec((1, w), const2),
            pl.BlockSpec((LRU_BLOCKS, LRU_BLOCK, LRU_BLOCK), lambda b, t: (0, 0, 0)),
            pl.BlockSpec((1, w), const2),
            pl.BlockSpec((1, w), const2),
            pl.BlockSpec((1, w), const2),
        ],
        out_specs=[
            pl.BlockSpec((tb, w), lambda b, t: (row(b, t), 0)),
            pl.BlockSpec((None, 1, w), lambda b, t: (b, 0, 0)),
            pl.BlockSpec((None, SUBLANES, w), lambda b, t: (b, 0, 0)),
        ],
        out_shape=[
            jax.ShapeDtypeStruct((batch * seq, w), BF16),
            jax.ShapeDtypeStruct((batch, 1, w), F32),
            jax.ShapeDtypeStruct((batch, SUBLANES, w), F32),
        ],
        scratch=[pltpu.VMEM((_stage_rows(tb), w), BF16), pltpu.VMEM((1, w), F32),
                 pltpu.VMEM((w // LANES, tb, LANES), F32), pltpu.VMEM((w // LANES, tb, LANES), F32)],
        args=(proj, conv0, h0, shift, p["lru_conv_w"], p["lru_conv_b"], p["lru_wa"], p["lru_ba"], p["lru_wx"],
              p["lru_bx"], p["lru_lambda"], p["lru_norm"]))


def _ret_body(qk_ref, v_ref, g_ref, cq_ref, sq_ref, ck_ref, sk_ref, s0_ref, nrm_ref,
              out_ref, snew_ref, state, decay_tab, eg_tab, wt_tab):
    c = qk_ref.shape[0]
    q_ref = qk_ref.at[:, 0:RET_QK]
    k_ref = qk_ref.at[:, RET_QK:2 * RET_QK]

    @pl.when(jnp.logical_and(pl.program_id(0) == 0, pl.program_id(1) == 0))
    def _():
        ii = lax.broadcasted_iota(jnp.int32, (c, c), 0)
        jj = lax.broadcasted_iota(jnp.int32, (c, c), 1)
        dpos = jnp.maximum(ii - jj, 0).astype(F32)
        rpos = lax.broadcasted_iota(jnp.int32, (c, RET_DK), 0).astype(F32)
        for h in range(RET_HEADS):
            lg = RET_LOG_GAMMA[h]
            decay_tab[h] = jnp.where(ii >= jj, jnp.exp(dpos * lg), 0.0)
            eg_tab[h] = jnp.exp((rpos + 1.0) * lg)
            wt_tab[h] = jnp.exp((c - 1.0 - rpos) * lg)

    @pl.when(pl.program_id(1) == 0)
    def _():
        state[...] = s0_ref[...]

    cq, sq, ck, sk = cq_ref[...], sq_ref[...], ck_ref[...], sk_ref[...]
    outs = []
    for h in range(RET_HEADS):
        decay, e_g, w_tail = decay_tab[h], eg_tab[h], wt_tab[h]
        a_chunk = math.exp(c * RET_LOG_GAMMA[h])
        qh = q_ref[:, h * RET_DK:(h + 1) * RET_DK].astype(F32)
        kh = k_ref[:, h * RET_DK:(h + 1) * RET_DK].astype(F32)
        qr = qh * cq + pltpu.roll(qh, RET_DK // 2, 1) * sq
        kr = kh * ck + pltpu.roll(kh, RET_DK // 2, 1) * sk
        vh = v_ref[:, h * RET_DV:(h + 1) * RET_DV]
        sh = state[h]
        scores = _dot_nt(qr.astype(BF16), kr.astype(BF16)) * decay
        y = _dot(scores.astype(BF16), vh) + _dot((qr * e_g).astype(BF16), sh.astype(BF16))
        state[h] = a_chunk * sh + _dot_tn((kr * w_tail).astype(BF16), vh)
        mu = jnp.mean(y, axis=-1, keepdims=True)
        d = y - mu
        var = jnp.mean(d * d, axis=-1, keepdims=True)
        outs.append(d * lax.rsqrt(var + EPS))
    ro = jnp.concatenate(outs, axis=-1) * nrm_ref[...]
    out_ref[...] = (ro * _silu(g_ref[...].astype(F32))).astype(out_ref.dtype)
    snew_ref[...] = state[...]


def _ret_parts(proj, s0, rope, p, states, layer, *, batch, seq, c):
    assert seq % c == 0 and states.shape[1:] == (batch, RET_HEADS, RET_DK, RET_DV)
    nt = seq // c
    row = lambda b, t: b * nt + t
    tab = pl.BlockSpec((c, RET_DK), lambda b, t: (t, 0))
    return MixerParts(
        body=_ret_body,
        in_specs=[
            pl.BlockSpec((c, 2 * RET_QK), lambda b, t: (row(b, t), OFF_RQ // (2 * RET_QK))),
            pl.BlockSpec((c, RET_V), lambda b, t: (row(b, t), OFF_RV // RET_V)),
            pl.BlockSpec((c, RET_V), lambda b, t: (row(b, t), OFF_RG // RET_V)),
            tab, tab, tab, tab,
            pl.BlockSpec((None, RET_HEADS, RET_DK, RET_DV), lambda b, t: (b, 0, 0, 0)),
            pl.BlockSpec((1, RET_V), lambda b, t: (0, 0)),
        ],
        out_specs=[
            pl.BlockSpec((c, RET_V), lambda b, t: (row(b, t), 0)),
            pl.BlockSpec((None, None, RET_HEADS, RET_DK, RET_DV), lambda b, t: (layer, b, 0, 0, 0)),
        ],
        out_shape=[
            jax.ShapeDtypeStruct((batch * seq, RET_V), BF16),
            jax.ShapeDtypeStruct(states.shape, F32),
        ],
        carried={1: states},
        scratch=[
            pltpu.VMEM((RET_HEADS, RET_DK, RET_DV), F32),
            pltpu.VMEM((RET_HEADS, c, c), F32),
            pltpu.VMEM((RET_HEADS, c, RET_DK), F32),
            pltpu.VMEM((RET_HEADS, c, RET_DK), F32),
        ],
        args=(proj, proj, proj, rope[0], rope[1], rope[2], rope[3], s0, p["ret_norm"]))


def _cumsum_rows(x):
    n = x.shape[0]
    rows = lax.broadcasted_iota(jnp.int32, x.shape, 0)
    s = 1
    while s < n:
        x = x + jnp.where(rows >= s, pltpu.roll(x, s, 0), 0.0)
        s *= 2
    return x


def _ssd_body(zxbc_ref, dt_ref, conv0_ref, s0_ref, shift_ref, cw_ref, cb_ref,
              dtb_ref, alog_ref, dexp_ref, nrm_ref, out_ref, snew_ref, ctail_ref, stage, state, xw_scr, u_scr, *,
              n_steps):
    c = zxbc_ref.shape[0]
    gh, hd, gw = SSD_GROUP_HEADS, SSD_HEADDIM, SSD_GROUP_WIDTH
    conv_in = lambda j: zxbc_ref[:, SSD_INNER + j * LANES:SSD_INNER + (j + 1) * LANES]

    @pl.when(pl.program_id(1) == 0)
    def _():
        stage[...] = jnp.zeros(stage.shape, stage.dtype)
        stage[0:HIST_ROWS, :] = _history_tile(conv0_ref[...])
        for g in range(SSD_GROUPS):
            state[g] = jnp.concatenate([s0_ref[g * gh + hh] for hh in range(gh)], axis=-1)

    stage[pl.ds(HIST_ROWS, c), :] = zxbc_ref[:, SSD_INNER:SSD_INNER + SSD_CONV_DIM]
    sh = _dot(shift_ref[...], stage[...])
    tail = stage[pl.ds(HIST_ROWS + c - BF16_ROWS, BF16_ROWS), :].astype(F32)[SUBLANES:]
    stage[0:HIST_ROWS, :] = _history_tile(tail)
    ctail_ref[...] = tail

    cw_half = 0.5 * cw_ref[...]
    cb_half = 0.5 * cb_ref[...]

    def conv_tile(x_bf16, j):
        sl = slice(j * LANES, (j + 1) * LANES)
        acc = sh[2 * c:3 * c, sl] * cw_half[0:1, sl]
        acc = acc + sh[c:2 * c, sl] * cw_half[1:2, sl]
        acc = acc + sh[0:c, sl] * cw_half[2:3, sl]
        acc = acc + x_bf16.astype(F32) * cw_half[3:4, sl]
        h = cb_half[:, sl] + acc
        return h + h * jnp.tanh(h)

    x_tiles = SSD_INNER // LANES
    bc_tiles = SSD_BC // LANES

    dt = _softplus(dt_ref[...] + dtb_ref[...])
    a_neg = -jnp.exp(alog_ref[...])
    g2 = _cumsum_rows(dt * a_neg) * LOG2E
    g2_t = g2.T
    lane_lo = lax.broadcasted_iota(jnp.int32, (c, LANES), 1) < hd
    ii = lax.broadcasted_iota(jnp.int32, (c, c), 0)
    jj = lax.broadcasted_iota(jnp.int32, (c, c), 1)
    tri = ii >= jj

    bgs, cbs, y_inter = [], [], []
    for g in range(SSD_GROUPS):
        bg = conv_tile(conv_in(x_tiles + g), x_tiles + g).astype(BF16)
        cg = conv_tile(conv_in(x_tiles + bc_tiles + g), x_tiles + bc_tiles + g).astype(BF16)
        bgs.append(bg)
        cbs.append(jnp.where(tri, _dot_nt(cg, bg), 0.0))
        y_inter.append(_dot(cg, state[g].astype(BF16)))

    ssq = None
    a_tiles = []
    for p in range(SSD_HEADS // 2):
        g, pp = divmod(p, gh // 2)
        sl = slice(p * LANES, (p + 1) * LANES)
        xs = conv_tile(conv_in(p), p)
        gcol = [jnp.broadcast_to(g2[:, h:h + 1], (c, LANES)) for h in (2 * p, 2 * p + 1)]
        dcol = [jnp.broadcast_to(dt[:, h:h + 1], (c, LANES)) for h in (2 * p, 2 * p + 1)]
        g_x = jnp.where(lane_lo, gcol[0], gcol[1])
        xdt = xs * jnp.where(lane_lo, dcol[0], dcol[1])
        g_last = g_x[c - 1:c, :]
        xw_scr[:, sl] = (xdt * jnp.exp2(g_last - g_x)).astype(BF16)
        a_tiles.append(jnp.exp2(g_last))
        ms = []
        for k, h in enumerate((2 * p, 2 * p + 1)):
            decay = jnp.exp2(jnp.minimum(gcol[k][:, 0:c] - g2_t[h:h + 1, :], 0.0))
            ms.append((cbs[g] * decay).astype(BF16))
        rhs = jnp.concatenate([jnp.where(lane_lo, xdt, 0.0), jnp.where(lane_lo, 0.0, xdt)], axis=0).astype(BF16)
        y = _dot(jnp.concatenate(ms, axis=1), rhs) + jnp.exp2(g_x) * y_inter[g][:, pp * LANES:(pp + 1) * LANES]
        so = y + dexp_ref[:, sl] * xs
        u = so * _silu(zxbc_ref[:, sl].astype(F32))
        ssq = u * u if ssq is None else ssq + u * u
        u_scr[:, sl] = u

    for g in range(SSD_GROUPS):
        gsl = slice(g * gw, (g + 1) * gw)
        a_row = jnp.concatenate(a_tiles[g * (gh // 2):(g + 1) * (gh // 2)], axis=-1)
        state[g] = a_row * state[g] + _dot_tn(bgs[g], xw_scr[:, gsl])

    rs = lax.rsqrt(jnp.sum(ssq, axis=-1, keepdims=True) * (1.0 / SSD_INNER) + EPS)
    out_ref[...] = (u_scr[...] * rs * nrm_ref[...]).astype(out_ref.dtype)

    @pl.when(pl.program_id(1) == n_steps - 1)
    def _():
        for g in range(SSD_GROUPS):
            sg = state[g]
            for hh in range(gh):
                snew_ref[g * gh + hh] = sg[:, hh * hd:(hh + 1) * hd]


def _ssd_parts(proj, dt_raw, conv0, s0, p, states, layer, *, batch, seq, c):
    assert seq % c == 0 and states.shape[1:] == (batch, SSD_HEADS, SSD_STATE, SSD_HEADDIM)
    nt = seq // c
    row = lambda b, t: b * nt + t
    const2 = lambda b, t: (0, 0)
    width = SSD_INNER + SSD_CONV_DIM
    assert OFF_SZ % width == 0 and OFF_SX == OFF_SZ + SSD_INNER
    shift = _shift_matrix(c)
    return MixerParts(
        body=functools.partial(_ssd_body, n_steps=nt),
        in_specs=[
            pl.BlockSpec((c, width), lambda b, t: (row(b, t), OFF_SZ // width)),
            pl.BlockSpec((c, LANES), lambda b, t: (row(b, t), 0)),
            pl.BlockSpec((None, SUBLANES, SSD_CONV_DIM), lambda b, t: (b, 0, 0)),
            pl.BlockSpec((None, SSD_HEADS, SSD_STATE, SSD_HEADDIM), lambda b, t: (b, 0, 0, 0)),
            pl.BlockSpec(shift.shape, const2),
            pl.BlockSpec((CONV_W, SSD_CONV_DIM), const2),
            pl.BlockSpec((1, SSD_CONV_DIM), const2),
            pl.BlockSpec((1, LANES), const2),
            pl.BlockSpec((1, LANES), const2),
            pl.BlockSpec((1, SSD_INNER), const2),
            pl.BlockSpec((1, SSD_INNER), const2),
        ],
        out_specs=[
            pl.BlockSpec((c, SSD_INNER), lambda b, t: (row(b, t), 0)),
            pl.BlockSpec((None, None, SSD_HEADS, SSD_STATE, SSD_HEADDIM), lambda b, t: (layer, b, 0, 0, 0)),
            pl.BlockSpec((None, SUBLANES, SSD_CONV_DIM), lambda b, t: (b, 0, 0)),
        ],
        out_shape=[
            jax.ShapeDtypeStruct((batch * seq, SSD_INNER), BF16),
            jax.ShapeDtypeStruct(states.shape, F32),
            jax.ShapeDtypeStruct((batch, SUBLANES, SSD_CONV_DIM), F32),
        ],
        carried={1: states},
        scratch=[
            pltpu.VMEM((_stage_rows(c), SSD_CONV_DIM), BF16),
            pltpu.VMEM((SSD_GROUPS, SSD_STATE, SSD_GROUP_WIDTH), F32),
            pltpu.VMEM((c, SSD_INNER), BF16),
            pltpu.VMEM((c, SSD_INNER), F32),
        ],
        args=(proj, dt_raw, conv0, s0, shift, p["ssd_conv_w"], p["ssd_conv_b"],
              p["ssd_dt_bias"], p["ssd_a_log"], p["ssd_d_exp"], p["ssd_norm"]))


def _xattn_body(x_ref, nx_ref, wq_ref, mk_ref, mv_ref, wo_ref, nf_ref, o_ref, *rest, final):
    if final:
        hn_ref, part_ref = rest
    else:
        hn_next_ref, hn_ref, part_ref = rest
    _rmsnorm_rows_to(hn_ref, x_ref, nx_ref, part_ref)
    q = _dot(hn_ref[...], wq_ref[...])
    scale = XATT_HD ** -0.5
    n_seqs = mk_ref.shape[0]
    rows = x_ref.shape[0] // n_seqs
    o_rows = []
    for bi in range(n_seqs):
        qb = q[bi * rows:(bi + 1) * rows]
        os = []
        for h in range(XATT_HEADS):
            sl = slice(h * XATT_HD, (h + 1) * XATT_HD)
            s = _dot_nt(qb[:, sl].astype(BF16), mk_ref[bi, :, sl].astype(BF16)) * scale
            mx = jnp.max(s, axis=-1, keepdims=True)
            pr = jnp.exp(s - mx)
            den = jnp.sum(pr, axis=-1, keepdims=True)
            os.append(_dot(pr.astype(BF16), mv_ref[bi, :, sl].astype(BF16)) / den)
        o_rows.append(jnp.concatenate(os, axis=-1))
    o = jnp.concatenate(o_rows, axis=0).astype(BF16)
    o_ref[...] = x_ref[...] + _dot(o, wo_ref[...])
    if final:
        _rmsnorm_rows_to(o_ref, o_ref, nf_ref, part_ref)
    else:
        _rmsnorm_rows_to(hn_next_ref, o_ref, nf_ref, part_ref)


def _cross_attn(x, mem_k, mem_v, nx, wq, wo, nf, *, batch, seq, tm, final):
    d = x.shape[1]
    n_seqs = max(1, min(tm // seq, batch))
    assert batch % n_seqs == 0
    tm = min(tm, seq) * n_seqs
    nt = seq * n_seqs // tm
    const2 = lambda b, t: (0, 0)
    x_spec = pl.BlockSpec((tm, d), lambda b, t: (b * nt + t, 0))
    out_specs = x_spec if final else [x_spec, x_spec]
    out_shape = jax.ShapeDtypeStruct(x.shape, F32)
    if not final:
        out_shape = [out_shape, jax.ShapeDtypeStruct(x.shape, BF16)]
    return pl.pallas_call(
        functools.partial(_xattn_body, final=final),
        grid=(batch // n_seqs, nt),
        in_specs=[
            x_spec,
            pl.BlockSpec((1, d), const2),
            pl.BlockSpec((d, XATT_DIM), const2),
            pl.BlockSpec((n_seqs, N_MEM, XATT_DIM), lambda b, t: (b, 0, 0)),
            pl.BlockSpec((n_seqs, N_MEM, XATT_DIM), lambda b, t: (b, 0, 0)),
            pl.BlockSpec((XATT_DIM, d), const2),
            pl.BlockSpec((1, d), const2),
        ],
        out_specs=out_specs,
        out_shape=out_shape,
        scratch_shapes=[pltpu.VMEM((tm, d), BF16), pltpu.VMEM((tm, LANES), F32)],
        compiler_params=_params(("arbitrary", "arbitrary")),
        name="cross_attn",
    )(x, nx.reshape(1, d), wq, mem_k, mem_v, wo, nf.reshape(1, d))


def _pad_state_rows(s):
    return jnp.pad(s.astype(F32), ((0, 0), (SUBLANES - (CONV_W - 1), 0), (0, 0)))


def _rope_tables(pos):
    half = RET_DK // 2
    inv = ROPE_BASE ** (-jnp.arange(half, dtype=F32) / half)
    ang = pos[:, None] * inv[None, :]
    cos, sin = jnp.cos(ang), jnp.sin(ang)
    cq = jnp.concatenate([cos, cos], axis=-1)
    sq = jnp.concatenate([-sin, sin], axis=-1)
    ks = RET_DK ** -0.5
    return cq, sq, cq * ks, sq * ks


def _layer_params(l, norm_mix, w_in, lru_conv_w, lru_conv_b, lru_wa, lru_ba, lru_wx, lru_bx, lru_lambda, lru_norm,
                  ret_norm, ssd_conv_w, ssd_conv_b, ssd_dt_bias, ssd_a_log, ssd_d, ssd_norm, w_out, norm_xattn,
                  norm_mem, w_q_mem, w_k_mem, w_v_mem, w_o_mem):
    w_in, w_dt_t = w_in
    w_dt = w_dt_t[l]
    w_q, w_kv, w_o = _cast_xattn_weights(w_q_mem, w_k_mem, w_v_mem, w_o_mem, l)
    row = lambda v: v.reshape(1, -1).astype(F32)
    pad_lanes = lambda v: jnp.pad(v.astype(F32), (0, LANES - v.shape[0])).reshape(1, LANES)
    return dict(
        norm_mix=norm_mix[l], w_in=w_in, layer=l, w_dt=w_dt,
        lru_conv_w=lru_conv_w[l], lru_conv_b=row(lru_conv_b[l]), lru_wa=lru_wa[l].astype(BF16), lru_ba=row(lru_ba[l]),
        lru_wx=lru_wx[l].astype(BF16), lru_bx=row(lru_bx[l]), lru_lambda=row(lru_lambda[l]), lru_norm=row(lru_norm[l]),
        ret_norm=row(ret_norm[l]),
        ssd_conv_w=ssd_conv_w[l], ssd_conv_b=row(ssd_conv_b[l]), ssd_dt_bias=pad_lanes(ssd_dt_bias[l]),
        ssd_a_log=pad_lanes(ssd_a_log[l]), ssd_d_exp=row(jnp.repeat(ssd_d[l], SSD_HEADDIM)), ssd_norm=row(ssd_norm[l]),
        w_out=_cast_layer_bf16(w_out, l), norm_xattn=norm_xattn[l], norm_mem=norm_mem[l],
        w_q=w_q, w_kv=w_kv, w_o=w_o)


def _mixer_sublayer(x, hn, rope, lru_h0, lru_conv0, ret_s0, ssd_s0, ssd_conv0, ret_states, ssd_states, p, *,
                    batch, seq):
    layer = p["layer"]
    proj, dt_raw = _in_proj(hn, p["w_in"], layer, p["w_dt"], n=OFF_DT, tm=TM_IN_PROJ, tn=TN_IN_PROJ)
    c = min(CHUNK, seq)
    (lru_out, lru_h1, lru_ct), (ssd_out, ssd_states, ssd_ct) = _run_mixers(
        [_lru_parts(proj, _pad_state_rows(lru_conv0), lru_h0.reshape(batch, 1, LRU_WIDTH), p,
                    batch=batch, seq=seq, tb=c),
         _ssd_parts(proj, dt_raw, _pad_state_rows(ssd_conv0), ssd_s0, p, ssd_states, layer,
                    batch=batch, seq=seq, c=c)],
        batch=batch, nt=seq // c)
    c_ret = min(CHUNK_RET, seq)
    ((ret_out, ret_states),) = _run_mixers(
        [_ret_parts(proj, ret_s0, rope, p, ret_states, layer, batch=batch, seq=seq, c=c_ret)],
        batch=batch, nt=seq // c_ret)
    x1 = _out_proj(x, lru_out, ret_out, ssd_out, p["w_out"], tm=TM_OUT, tn=TN_OUT)
    tail = SUBLANES - (CONV_W - 1)
    small_states = (lru_h1.reshape(batch, LRU_WIDTH), lru_ct[:, tail:, :], ssd_ct[:, tail:, :])
    return x1, small_states, ret_states, ssd_states


def kernel(x_prompt, x_sample, mem_prompt, state_lru_h, state_lru_conv, state_ret, state_ssd, state_ssd_conv, cache_mem_k, cache_mem_v, norm_mix, w_in, lru_conv_w, lru_conv_b, lru_wa, lru_ba, lru_wx, lru_bx, lru_lambda, lru_norm, ret_norm, ssd_conv_w, ssd_conv_b, ssd_dt_bias, ssd_a_log, ssd_d, ssd_norm, w_out, norm_xattn, norm_mem, w_q_mem, w_k_mem, w_v_mem, w_o_mem, norm_final):
    bp, tp, d = x_prompt.shape
    bs, ts, _ = x_sample.shape
    depth = w_in.shape[0]
    assert tp % BF16_ROWS == 0 and ts % BF16_ROWS == 0
    rope_p = _rope_tables(jnp.arange(tp, dtype=F32))
    rope_s = _rope_tables(PAST_LEN + jnp.arange(ts, dtype=F32))
    z_lru_h = jnp.zeros((bp, LRU_WIDTH), F32)
    z_lru_conv = jnp.zeros((bp, CONV_W - 1, LRU_WIDTH), F32)
    z_ret = jnp.zeros((bp, RET_HEADS, RET_DK, RET_DV), F32)
    z_ssd = jnp.zeros((bp, SSD_HEADS, SSD_STATE, SSD_HEADDIM), F32)
    z_ssd_conv = jnp.zeros((bp, CONV_W - 1, SSD_CONV_DIM), F32)

    xp = x_prompt.reshape(bp * tp, d)
    xs = x_sample.reshape(bs * ts, d)
    mem = mem_prompt.reshape(bp * N_MEM, d)
    w_in_t = jnp.swapaxes(w_in, 1, 2)
    w_dt_t = jnp.pad(w_in_t[:, OFF_DT:, :], ((0, 0), (0, LANES - SSD_HEADS), (0, 0))).astype(BF16)
    w_in = (_cast_stack_bf16(w_in_t, n_rows=OFF_DT, rows=CAST_ROWS_W_IN), w_dt_t)
    hn_p = _rmsnorm_bf16(xp, norm_mix[0], tm=TM_XATT)
    hn_s = _rmsnorm_bf16(xs, norm_mix[0], tm=TM_XATT)
    st_p = [[] for _ in range(3)]
    st_s = [[] for _ in range(3)]
    ret_p = jnp.zeros((depth,) + z_ret.shape, F32)
    ssd_p = jnp.zeros((depth,) + z_ssd.shape, F32)
    ret_s = jnp.zeros(state_ret.shape, F32)
    ssd_s = jnp.zeros(state_ssd.shape, F32)
    mk_p, mv_p = [], []
    for l in range(depth):
        p = _layer_params(l, norm_mix, w_in, lru_conv_w, lru_conv_b, lru_wa, lru_ba, lru_wx, lru_bx, lru_lambda,
                          lru_norm, ret_norm, ssd_conv_w, ssd_conv_b, ssd_dt_bias, ssd_a_log, ssd_d, ssd_norm, w_out,
                          norm_xattn, norm_mem, w_q_mem, w_k_mem, w_v_mem, w_o_mem)
        final = l == depth - 1
        next_gain = norm_final if final else norm_mix[l + 1]
        xp, st, ret_p, ssd_p = _mixer_sublayer(xp, hn_p, rope_p, z_lru_h, z_lru_conv, z_ret, z_ssd, z_ssd_conv,
                                               ret_p, ssd_p, p, batch=bp, seq=tp)
        mkv = _norm_matmul(mem, p["norm_mem"], p["w_kv"], out_dtype=F32, tm=TM_PROJ, tn=TN_PROJ)
        mk = mkv[:, :XATT_DIM].reshape(bp, N_MEM, XATT_DIM)
        mv = mkv[:, XATT_DIM:].reshape(bp, N_MEM, XATT_DIM)
        res = _cross_attn(xp, mk, mv, p["norm_xattn"], p["w_q"], p["w_o"], next_gain, batch=bp, seq=tp, tm=TM_XATT,
                          final=final)
        xp, hn_p = (res, None) if final else res
        for acc, v in zip(st_p, st):
            acc.append(v)
        mk_p.append(mk.reshape(bp, N_MEM, XATT_HEADS, XATT_HD))
        mv_p.append(mv.reshape(bp, N_MEM, XATT_HEADS, XATT_HD))
        xs, st, ret_s, ssd_s = _mixer_sublayer(xs, hn_s, rope_s, state_lru_h[l], state_lru_conv[l], state_ret[l],
                                               state_ssd[l], state_ssd_conv[l], ret_s, ssd_s, p, batch=bs, seq=ts)
        res = _cross_attn(xs, cache_mem_k[l].reshape(bs, N_MEM, XATT_DIM), cache_mem_v[l].reshape(bs, N_MEM, XATT_DIM),
                          p["norm_xattn"], p["w_q"], p["w_o"], next_gain, batch=bs, seq=ts, tm=TM_XATT, final=final)
        xs, hn_s = (res, None) if final else res
        for acc, v in zip(st_s, st):
            acc.append(v)

    y_prompt = xp.reshape(bp, tp, d)
    y_sample = xs.reshape(bs, ts, d)
    return (y_prompt, y_sample,
            jnp.stack(st_p[0]), jnp.stack(st_p[1]), ret_p, ssd_p, jnp.stack(st_p[2]),
            jnp.stack(mk_p), jnp.stack(mv_p),
            jnp.stack(st_s[0]), jnp.stack(st_s[1]), ret_s, ssd_s, jnp.stack(st_s[2]))
```

```python
import functools
import math
from typing import Callable, NamedTuple

import numpy as np
import jax
import jax.numpy as jnp
from jax import lax
from jax.experimental import pallas as pl
from jax.experimental.pallas import tpu as pltpu

F32 = jnp.float32
BF16 = jnp.bfloat16

EPS = 1e-6
CONV_W = 4
PAST_LEN = 4096
N_MEM = 256
LRU_WIDTH = 1024
LRU_BLOCKS = 4
LRU_BLOCK = LRU_WIDTH // LRU_BLOCKS
LRU_C = 8.0
RET_HEADS = 4
RET_DK = 128
RET_DV = 256
RET_QK = RET_HEADS * RET_DK
RET_V = RET_HEADS * RET_DV
ROPE_BASE = 10000.0
SSD_INNER = 2048
SSD_HEADDIM = 64
SSD_HEADS = SSD_INNER // SSD_HEADDIM
SSD_STATE = 128
SSD_GROUPS = 4
SSD_GROUP_HEADS = SSD_HEADS // SSD_GROUPS
SSD_GROUP_WIDTH = SSD_GROUP_HEADS * SSD_HEADDIM
SSD_BC = SSD_GROUPS * SSD_STATE
SSD_CONV_DIM = SSD_INNER + 2 * SSD_BC
XATT_HEADS = 4
XATT_HD = 128
XATT_DIM = XATT_HEADS * XATT_HD

SUBLANES = 8
LANES = 128
BF16_ROWS = 2 * SUBLANES
VMEM_LIMIT_BYTES = 56 * 1024 * 1024

OFF_LX, OFF_LG, OFF_RQ, OFF_RK, OFF_RV, OFF_RG, OFF_SZ, OFF_SX = 0, 1024, 2048, 2560, 3072, 4096, 5120, 7168
OFF_SB = OFF_SX + SSD_INNER
OFF_SC = OFF_SB + SSD_BC
OFF_DT = OFF_SX + SSD_CONV_DIM

RET_LOG_GAMMA = tuple(math.log1p(-(2.0 ** (-5.0 - h))) for h in range(RET_HEADS))
LOG2E = math.log2(math.e)

TM_PROJ, TN_PROJ = 512, 1024
TM_IN_PROJ, TN_IN_PROJ = 1024, 1280
TM_OUT, TN_OUT = 1024, 1024
CAST_ROWS = 256
CAST_ROWS_W_IN = 512
TM_XATT = 256
CHUNK = 128
CHUNK_RET = 256


def _params(semantics):
    return pltpu.CompilerParams(dimension_semantics=semantics, vmem_limit_bytes=VMEM_LIMIT_BYTES)


def _sigmoid(x):
    return 1.0 / (1.0 + jnp.exp(-x))


def _silu(x):
    return x * _sigmoid(x)


def _softplus(x):
    return jnp.maximum(x, 0.0) + jnp.log1p(jnp.exp(-jnp.abs(x)))


def _dot(a, b):
    return jnp.dot(a, b, preferred_element_type=F32)


def _dot_nt(a, b):
    return lax.dot_general(a, b, (((1,), (1,)), ((), ())), preferred_element_type=F32)


def _dot_tn(a, b):
    return lax.dot_general(a, b, (((0,), (0,)), ((), ())), preferred_element_type=F32)


def _rmsnorm_rows_to(dst_ref, src_ref, gain_ref, part_ref, unroll=False):
    tm, d = src_ref.shape
    rc = BF16_ROWS
    lane_tiles = d // LANES

    def sum_squares(r, carry):
        rows = pl.ds(pl.multiple_of(r * rc, rc), rc)
        accs = [None] * min(4, lane_tiles)
        for j in range(lane_tiles):
            v = src_ref[rows, j * LANES:(j + 1) * LANES].astype(F32)
            k = j % len(accs)
            accs[k] = v * v if accs[k] is None else accs[k] + v * v
        while len(accs) > 1:
            accs = [a + b for a, b in zip(accs[0::2], accs[1::2])]
        part_ref[rows, :] = accs[0]
        return carry

    lax.fori_loop(0, tm // rc, sum_squares, 0, unroll=unroll)
    ms = jnp.sum(part_ref[...], axis=-1, keepdims=True) * (1.0 / d)
    part_ref[...] = jnp.broadcast_to(lax.rsqrt(ms + EPS), part_ref.shape)

    def scale(r, carry):
        rows = pl.ds(pl.multiple_of(r * rc, rc), rc)
        rs = part_ref[rows, :]
        for j in range(lane_tiles):
            sl = slice(j * LANES, (j + 1) * LANES)
            dst_ref[rows, sl] = (src_ref[rows, sl].astype(F32) * rs * gain_ref[:, sl]).astype(dst_ref.dtype)
        return carry

    lax.fori_loop(0, tm // rc, scale, 0, unroll=unroll)


def _cast_body(w_ref, o_ref):
    o_ref[...] = w_ref[...].astype(o_ref.dtype)


def _cast_layer_bf16(w_stack, l):
    _, r, c = w_stack.shape
    rows = min(CAST_ROWS, r)
    assert r % rows == 0
    return pl.pallas_call(
        _cast_body,
        grid=(r // rows,),
        in_specs=[pl.BlockSpec((None, rows, c), lambda i: (l, i, 0))],
        out_specs=pl.BlockSpec((rows, c), lambda i: (i, 0)),
        out_shape=jax.ShapeDtypeStruct((r, c), BF16),
        compiler_params=_params(("arbitrary",)),
        name="cast_bf16",
    )(w_stack)


def _cast_stack_bf16(w_stack, *, n_rows, rows):
    layers, r, c = w_stack.shape
    assert n_rows <= r and n_rows % rows == 0
    spec = pl.BlockSpec((None, rows, c), lambda l, i: (l, i, 0))
    return pl.pallas_call(
        _cast_body,
        grid=(layers, n_rows // rows),
        in_specs=[spec],
        out_specs=spec,
        out_shape=jax.ShapeDtypeStruct((layers, n_rows, c), BF16),
        compiler_params=_params(("arbitrary", "arbitrary")),
        name="cast_stack",
    )(w_stack)


def _cast_xattn_body(wq_ref, wk_ref, wv_ref, wo_ref, oq_ref, okv_ref, oo_ref):
    oq_ref[...] = wq_ref[...].astype(oq_ref.dtype)
    okv_ref[:, 0:XATT_DIM] = wk_ref[...].astype(okv_ref.dtype)
    okv_ref[:, XATT_DIM:2 * XATT_DIM] = wv_ref[...].astype(okv_ref.dtype)
    oo_ref[...] = wo_ref[...].astype(oo_ref.dtype)


def _cast_xattn_weights(w_q, w_k, w_v, w_o, l):
    _, d, n = w_q.shape
    steps = d // CAST_ROWS
    assert d % CAST_ROWS == 0 and n % steps == 0 and (n // steps) % BF16_ROWS == 0
    in_rows = lambda i: (l, i, 0)
    return pl.pallas_call(
        _cast_xattn_body,
        grid=(steps,),
        in_specs=[pl.BlockSpec((None, CAST_ROWS, n), in_rows), pl.BlockSpec((None, CAST_ROWS, n), in_rows),
                  pl.BlockSpec((None, CAST_ROWS, n), in_rows), pl.BlockSpec((None, n // steps, d), in_rows)],
        out_specs=[pl.BlockSpec((CAST_ROWS, n), lambda i: (i, 0)), pl.BlockSpec((CAST_ROWS, 2 * n), lambda i: (i, 0)),
                   pl.BlockSpec((n // steps, d), lambda i: (i, 0))],
        out_shape=[jax.ShapeDtypeStruct((d, n), BF16), jax.ShapeDtypeStruct((d, 2 * n), BF16),
                   jax.ShapeDtypeStruct((n, d), BF16)],
        compiler_params=_params(("arbitrary",)),
        name="cast_xattn",
    )(w_q, w_k, w_v, w_o)


def _norm_matmul_body(x_ref, g_ref, w_ref, o_ref, hn_ref, part_ref):
    @pl.when(pl.program_id(1) == 0)
    def _():
        _rmsnorm_rows_to(hn_ref, x_ref, g_ref, part_ref)

    o_ref[...] = _dot(hn_ref[...], w_ref[...]).astype(o_ref.dtype)


def _norm_matmul(x, g, w, *, out_dtype, tm, tn):
    m, k = x.shape
    n = w.shape[1]
    tm = min(tm, m)
    tn = min(tn, n)
    assert m % tm == 0 and n % tn == 0
    return pl.pallas_call(
        _norm_matmul_body,
        grid=(m // tm, n // tn),
        in_specs=[
            pl.BlockSpec((tm, k), lambda i, j: (i, 0)),
            pl.BlockSpec((1, k), lambda i, j: (0, 0)),
            pl.BlockSpec((k, tn), lambda i, j: (0, j)),
        ],
        out_specs=pl.BlockSpec((tm, tn), lambda i, j: (i, j)),
        out_shape=jax.ShapeDtypeStruct((m, n), out_dtype),
        scratch_shapes=[pltpu.VMEM((tm, k), BF16), pltpu.VMEM((tm, LANES), F32)],
        compiler_params=_params(("arbitrary", "arbitrary")),
        name="norm_matmul",
    )(x, g.reshape(1, k), w)


def _rmsnorm_body(x_ref, g_ref, o_ref, part_ref):
    _rmsnorm_rows_to(o_ref, x_ref, g_ref, part_ref)


def _rmsnorm_bf16(x, g, *, tm):
    m, k = x.shape
    tm = min(tm, m)
    assert m % tm == 0
    return pl.pallas_call(
        _rmsnorm_body,
        grid=(m // tm,),
        in_specs=[pl.BlockSpec((tm, k), lambda i: (i, 0)), pl.BlockSpec((1, k), lambda i: (0, 0))],
        out_specs=pl.BlockSpec((tm, k), lambda i: (i, 0)),
        out_shape=jax.ShapeDtypeStruct((m, k), BF16),
        scratch_shapes=[pltpu.VMEM((tm, LANES), F32)],
        compiler_params=_params(("arbitrary",)),
        name="rmsnorm_bf16",
    )(x, g.reshape(1, k))


def _in_proj_body(hn_ref, wt_ref, w2t_ref, o_ref, o2_ref):
    @pl.when(pl.program_id(1) == 0)
    def _():
        o2_ref[...] = _dot_nt(hn_ref[...], w2t_ref[...])

    o_ref[...] = _dot_nt(hn_ref[...], wt_ref[...]).astype(o_ref.dtype)


def _in_proj(hn, wt, l, w2t, *, n, tm, tn):
    m, k = hn.shape
    n2 = w2t.shape[0]
    tm = min(tm, m)
    assert m % tm == 0 and n % tn == 0 and n <= wt.shape[1]
    w, w2 = wt, w2t
    return pl.pallas_call(
        _in_proj_body,
        grid=(m // tm, n // tn),
        in_specs=[
            pl.BlockSpec((tm, k), lambda i, j: (i, 0)),
            pl.BlockSpec((None, tn, k), lambda i, j: (l, j, 0)),
            pl.BlockSpec((n2, k), lambda i, j: (0, 0)),
        ],
        out_specs=[pl.BlockSpec((tm, tn), lambda i, j: (i, j)), pl.BlockSpec((tm, n2), lambda i, j: (i, 0))],
        out_shape=[jax.ShapeDtypeStruct((m, n), BF16), jax.ShapeDtypeStruct((m, n2), F32)],
        compiler_params=_params(("arbitrary", "arbitrary")),
        name="in_proj",
    )(hn, w, w2)


def _out_proj_body(x_ref, a1_ref, a2_ref, a3_ref, w1_ref, w2_ref, w3_ref, o_ref):
    acc = _dot(a1_ref[...], w1_ref[...])
    acc = acc + _dot(a2_ref[...], w2_ref[...])
    acc = acc + _dot(a3_ref[...], w3_ref[...])
    o_ref[...] = x_ref[...] + acc


def _out_proj(x, lru_out, ret_out, ssd_out, w_out_bf16, *, tm, tn):
    m, n = x.shape
    tm = min(tm, m)
    tn = min(tn, n)
    assert m % tm == 0 and n % tn == 0
    return pl.pallas_call(
        _out_proj_body,
        grid=(m // tm, n // tn),
        in_specs=[
            pl.BlockSpec((tm, tn), lambda i, j: (i, j)),
            pl.BlockSpec((tm, LRU_WIDTH), lambda i, j: (i, 0)),
            pl.BlockSpec((tm, RET_V), lambda i, j: (i, 0)),
            pl.BlockSpec((tm, SSD_INNER), lambda i, j: (i, 0)),
            pl.BlockSpec((LRU_WIDTH, tn), lambda i, j: (0, j)),
            pl.BlockSpec((RET_V, tn), lambda i, j: (1, j)),
            pl.BlockSpec((SSD_INNER, tn), lambda i, j: (1, j)),
        ],
        out_specs=pl.BlockSpec((tm, tn), lambda i, j: (i, j)),
        out_shape=jax.ShapeDtypeStruct((m, n), F32),
        compiler_params=_params(("arbitrary", "arbitrary")),
        name="out_proj",
    )(x, lru_out, ret_out, ssd_out, w_out_bf16, w_out_bf16, w_out_bf16)


HIST_ROWS = BF16_ROWS
N_HIST = CONV_W - 1
HIST_HI, HIST_MID, HIST_LO = 0, N_HIST, SUBLANES
assert HIST_MID + N_HIST <= SUBLANES and HIST_LO + N_HIST <= HIST_ROWS


def _stage_rows(c):
    return -(-(HIST_ROWS + c) // LANES) * LANES


def _shift_matrix(c):
    s = np.zeros((N_HIST, c, _stage_rows(c)), np.float32)
    for k in range(N_HIST):
        back = k + 1
        for t in range(c):
            if t >= back:
                s[k, t, HIST_ROWS + t - back] = 1.0
            else:
                j = N_HIST + t - back
                s[k, t, HIST_HI + j] = s[k, t, HIST_MID + j] = s[k, t, HIST_LO + j] = 1.0
    return jnp.asarray(s.reshape(N_HIST * c, -1), dtype=BF16)


def _history_tile(tail8):
    r = pltpu.roll(tail8, N_HIST, 0)
    hi = r.astype(BF16).astype(F32)
    rest = r - hi
    mid = rest.astype(BF16).astype(F32)
    lo = rest - mid
    rows = lax.broadcasted_iota(jnp.int32, r.shape, 0)
    top = jnp.where(rows < N_HIST, hi, jnp.where(rows < HIST_MID + N_HIST, pltpu.roll(mid, HIST_MID, 0), 0.0))
    bot = jnp.where(rows < N_HIST, lo, 0.0)
    return jnp.concatenate([top, bot], axis=0).astype(BF16)


def _conv_from_stage(stage, shift_ref, x_f32, c, cw_ref, cb_ref):
    sh = _dot(shift_ref[...], stage[...])
    acc = sh[2 * c:3 * c] * cw_ref[0:1, :]
    acc = acc + sh[c:2 * c] * cw_ref[1:2, :]
    acc = acc + sh[0:c] * cw_ref[2:3, :]
    acc = acc + x_f32 * cw_ref[3:4, :]
    return cb_ref[...] + acc


class MixerParts(NamedTuple):
    body: Callable
    args: tuple
    in_specs: list
    out_specs: list
    out_shape: list
    scratch: list
    carried: dict = {}


def _mixers_body(*refs, parts, n_carried):
    n_in = sum(len(p.args) for p in parts)
    n_out = sum(len(p.out_specs) for p in parts)
    ins = refs[:n_in]
    outs = refs[n_in + n_carried:n_in + n_carried + n_out]
    scr = refs[n_in + n_carried + n_out:]
    i = o = s = 0
    for p in parts:
        ni, no, ns = len(p.args), len(p.out_specs), len(p.scratch)
        p.body(*ins[i:i + ni], *outs[o:o + no], *scr[s:s + ns])
        i, o, s = i + ni, o + no, s + ns


def _run_mixers(parts, *, batch, nt):
    flat = lambda field: [x for p in parts for x in getattr(p, field)]
    bodies = tuple(p._replace(args=tuple(None for _ in p.args), out_shape=[], carried={}) for p in parts)
    args, in_specs = flat("args"), flat("in_specs")
    aliases, o = {}, 0
    for p in parts:
        for k, buf in p.carried.items():
            aliases[len(args)] = o + k
            args.append(buf)
            in_specs.append(pl.BlockSpec(memory_space=pl.ANY))
        o += len(p.out_specs)
    outs = pl.pallas_call(
        functools.partial(_mixers_body, parts=bodies, n_carried=len(aliases)),
        grid=(batch, nt),
        in_specs=in_specs,
        out_specs=flat("out_specs"),
        out_shape=flat("out_shape"),
        scratch_shapes=flat("scratch"),
        input_output_aliases=aliases,
        compiler_params=_params(("arbitrary", "arbitrary")),
        name="mixers",
    )(*args)
    res, o = [], 0
    for p in parts:
        res.append(outs[o:o + len(p.out_specs)])
        o += len(p.out_specs)
    return res


def _lru_body(lxg_ref, conv0_ref, h0_ref, shift_ref, cw_ref, cb_ref, wa_ref, ba_ref, wx_ref, bx_ref, lam_ref,
              nrm_ref, out_ref, hlast_ref, ctail_ref, stage, hcar, a_scr, b_scr):
    tb = lxg_ref.shape[0]
    lx_ref = lxg_ref.at[:, 0:LRU_WIDTH]
    lg_ref = lxg_ref.at[:, LRU_WIDTH:2 * LRU_WIDTH]

    @pl.when(pl.program_id(1) == 0)
    def _():
        stage[...] = jnp.zeros(stage.shape, stage.dtype)
        stage[0:HIST_ROWS, :] = _history_tile(conv0_ref[...])
        hcar[...] = h0_ref[...]

    stage[pl.ds(HIST_ROWS, tb), :] = lx_ref[...]
    x_in = lx_ref[...].astype(F32)
    xc = _conv_from_stage(stage, shift_ref, x_in, tb, cw_ref, cb_ref)
    tail = x_in[tb - SUBLANES:tb]
    stage[0:HIST_ROWS, :] = _history_tile(tail)
    ctail_ref[...] = tail

    xb = xc.astype(BF16)
    ra, ri = [], []
    for n in range(LRU_BLOCKS):
        xn = xb[:, n * LRU_BLOCK:(n + 1) * LRU_BLOCK]
        ra.append(_dot(xn, wa_ref[n]))
        ri.append(_dot(xn, wx_ref[n]))
    r = _sigmoid(jnp.concatenate(ra, axis=-1) + ba_ref[...])
    i = _sigmoid(jnp.concatenate(ri, axis=-1) + bx_ref[...])
    log_a = r * ((-LRU_C) * _softplus(-lam_ref[...]))
    a = jnp.exp(log_a)
    th = jnp.tanh(log_a)
    b = jnp.sqrt(-2.0 * th / (1.0 - th)) * (i * xc)

    ng = tb // SUBLANES
    for j in range(LRU_WIDTH // LANES):
        a_scr[j] = a[:, j * LANES:(j + 1) * LANES]
        b_scr[j] = b[:, j * LANES:(j + 1) * LANES]
    for j in range(LRU_WIDTH // LANES):
        pa = a_scr[j, pl.ds(0, ng, stride=SUBLANES), :]
        pb = b_scr[j, pl.ds(0, ng, stride=SUBLANES), :]
        for r in range(1, SUBLANES):
            view = pl.ds(r, ng, stride=SUBLANES)
            ar = a_scr[j, view, :]
            pb = ar * pb + b_scr[j, view, :]
            pa = ar * pa
            a_scr[j, view, :] = pa
            b_scr[j, view, :] = pb
    a3 = jnp.concatenate([a_scr[j] for j in range(LRU_WIDTH // LANES)], axis=-1).reshape(ng, SUBLANES, LRU_WIDTH)
    b3 = jnp.concatenate([b_scr[j] for j in range(LRU_WIDTH // LANES)], axis=-1).reshape(ng, SUBLANES, LRU_WIDTH)
    hprev = hcar[...]
    hs = []
    for g in range(ng):
        hg = b3[g] + a3[g] * hprev
        hprev = hg[SUBLANES - 1:SUBLANES]
        hs.append(hg)
    h = jnp.concatenate(hs, axis=0)
    hcar[...] = hprev
    hlast_ref[...] = hprev

    ms = jnp.mean(h * h, axis=-1, keepdims=True)
    out = h * lax.rsqrt(ms + EPS) * nrm_ref[...]
    out_ref[...] = (out * _silu(lg_ref[...].astype(F32))).astype(out_ref.dtype)


def _lru_parts(proj, conv0, h0, p, *, batch, seq, tb):
    assert seq % tb == 0 and OFF_LG == OFF_LX + LRU_WIDTH and OFF_LX % (2 * LRU_WIDTH) == 0
    nt = seq // tb
    w = LRU_WIDTH
    row = lambda b, t: b * nt + t
    const2 = lambda b, t: (0, 0)
    shift = _shift_matrix(tb)
    return MixerParts(
        body=_lru_body,
        in_specs=[
            pl.BlockSpec((tb, 2 * w), lambda b, t: (row(b, t), OFF_LX // (2 * w))),
            pl.BlockSpec((None, SUBLANES, w), lambda b, t: (b, 0, 0)),
            pl.BlockSpec((None, 1, w), lambda b, t: (b, 0, 0)),
            pl.BlockSpec(shift.shape, const2),
            pl.BlockSpec((CONV_W, w), const2),
            pl.BlockSpec((1, w), const2),
            pl.BlockSpec((LRU_BLOCKS, LRU_BLOCK, LRU_BLOCK), lambda b, t: (0, 0, 0)),
            pl.BlockSpec((1, w), const2),
            pl.BlockSpec((LRU_BLOCKS, LRU_BLOCK, LRU_BLOCK), lambda b, t: (0, 0, 0)),
            pl.BlockSpec((1, w), const2),
            pl.BlockSpec((1, w), const2),
            pl.BlockSpec((1, w), const2),
        ],
        out_specs=[
            pl.BlockSpec((tb, w), lambda b, t: (row(b, t), 0)),
            pl.BlockSpec((None, 1, w), lambda b, t: (b, 0, 0)),
            pl.BlockSpec((None, SUBLANES, w), lambda b, t: (b, 0, 0)),
        ],
        out_shape=[
            jax.ShapeDtypeStruct((batch * seq, w), BF16),
            jax.ShapeDtypeStruct((batch, 1, w), F32),
            jax.ShapeDtypeStruct((batch, SUBLANES, w), F32),
        ],
        scratch=[pltpu.VMEM((_stage_rows(tb), w), BF16), pltpu.VMEM((1, w), F32),
                 pltpu.VMEM((w // LANES, tb, LANES), F32), pltpu.VMEM((w // LANES, tb, LANES), F32)],
        args=(proj, conv0, h0, shift, p["lru_conv_w"], p["lru_conv_b"], p["lru_wa"], p["lru_ba"], p["lru_wx"],
              p["lru_bx"], p["lru_lambda"], p["lru_norm"]))


def _ret_body(qk_ref, v_ref, g_ref, cq_ref, sq_ref, ck_ref, sk_ref, s0_ref, nrm_ref,
              out_ref, snew_ref, state, decay_tab, eg_tab, wt_tab):
    c = qk_ref.shape[0]
    q_ref = qk_ref.at[:, 0:RET_QK]
    k_ref = qk_ref.at[:, RET_QK:2 * RET_QK]

    @pl.when(jnp.logical_and(pl.program_id(0) == 0, pl.program_id(1) == 0))
    def _():
        ii = lax.broadcasted_iota(jnp.int32, (c, c), 0)
        jj = lax.broadcasted_iota(jnp.int32, (c, c), 1)
        dpos = jnp.maximum(ii - jj, 0).astype(F32)
        rpos = lax.broadcasted_iota(jnp.int32, (c, RET_DK), 0).astype(F32)
        for h in range(RET_HEADS):
            lg = RET_LOG_GAMMA[h]
            decay_tab[h] = jnp.where(ii >= jj, jnp.exp(dpos * lg), 0.0)
            eg_tab[h] = jnp.exp((rpos + 1.0) * lg)
            wt_tab[h] = jnp.exp((c - 1.0 - rpos) * lg)

    @pl.when(pl.program_id(1) == 0)
    def _():
        state[...] = s0_ref[...]

    cq, sq, ck, sk = cq_ref[...], sq_ref[...], ck_ref[...], sk_ref[...]
    outs = []
    for h in range(RET_HEADS):
        decay, e_g, w_tail = decay_tab[h], eg_tab[h], wt_tab[h]
        a_chunk = math.exp(c * RET_LOG_GAMMA[h])
        qh = q_ref[:, h * RET_DK:(h + 1) * RET_DK].astype(F32)
        kh = k_ref[:, h * RET_DK:(h + 1) * RET_DK].astype(F32)
        qr = qh * cq + pltpu.roll(qh, RET_DK // 2, 1) * sq
        kr = kh * ck + pltpu.roll(kh, RET_DK // 2, 1) * sk
        vh = v_ref[:, h * RET_DV:(h + 1) * RET_DV]
        sh = state[h]
        scores = _dot_nt(qr.astype(BF16), kr.astype(BF16)) * decay
        y = _dot(scores.astype(BF16), vh) + _dot((qr * e_g).astype(BF16), sh.astype(BF16))
        state[h] = a_chunk * sh + _dot_tn((kr * w_tail).astype(BF16), vh)
        mu = jnp.mean(y, axis=-1, keepdims=True)
        d = y - mu
        var = jnp.mean(d * d, axis=-1, keepdims=True)
        outs.append(d * lax.rsqrt(var + EPS))
    ro = jnp.concatenate(outs, axis=-1) * nrm_ref[...]
    out_ref[...] = (ro * _silu(g_ref[...].astype(F32))).astype(out_ref.dtype)
    snew_ref[...] = state[...]


def _ret_parts(proj, s0, rope, p, states, layer, *, batch, seq, c):
    assert seq % c == 0 and states.shape[1:] == (batch, RET_HEADS, RET_DK, RET_DV)
    nt = seq // c
    row = lambda b, t: b * nt + t
    tab = pl.BlockSpec((c, RET_DK), lambda b, t: (t, 0))
    return MixerParts(
        body=_ret_body,
        in_specs=[
            pl.BlockSpec((c, 2 * RET_QK), lambda b, t: (row(b, t), OFF_RQ // (2 * RET_QK))),
            pl.BlockSpec((c, RET_V), lambda b, t: (row(b, t), OFF_RV // RET_V)),
            pl.BlockSpec((c, RET_V), lambda b, t: (row(b, t), OFF_RG // RET_V)),
            tab, tab, tab, tab,
            pl.BlockSpec((None, RET_HEADS, RET_DK, RET_DV), lambda b, t: (b, 0, 0, 0)),
            pl.BlockSpec((1, RET_V), lambda b, t: (0, 0)),
        ],
        out_specs=[
            pl.BlockSpec((c, RET_V), lambda b, t: (row(b, t), 0)),
            pl.BlockSpec((None, None, RET_HEADS, RET_DK, RET_DV), lambda b, t: (layer, b, 0, 0, 0)),
        ],
        out_shape=[
            jax.ShapeDtypeStruct((batch * seq, RET_V), BF16),
            jax.ShapeDtypeStruct(states.shape, F32),
        ],
        carried={1: states},
        scratch=[
            pltpu.VMEM((RET_HEADS, RET_DK, RET_DV), F32),
            pltpu.VMEM((RET_HEADS, c, c), F32),
            pltpu.VMEM((RET_HEADS, c, RET_DK), F32),
            pltpu.VMEM((RET_HEADS, c, RET_DK), F32),
        ],
        args=(proj, proj, proj, rope[0], rope[1], rope[2], rope[3], s0, p["ret_norm"]))


def _cumsum_rows(x):
    n = x.shape[0]
    rows = lax.broadcasted_iota(jnp.int32, x.shape, 0)
    s = 1
    while s < n:
        x = x + jnp.where(rows >= s, pltpu.roll(x, s, 0), 0.0)
        s *= 2
    return x


def _ssd_body(zxbc_ref, dt_ref, conv0_ref, s0_ref, shift_ref, cw_ref, cb_ref,
              dtb_ref, alog_ref, dexp_ref, nrm_ref, out_ref, snew_ref, ctail_ref, stage, state, xw_scr, u_scr, *,
              n_steps):
    c = zxbc_ref.shape[0]
    gh, hd, gw = SSD_GROUP_HEADS, SSD_HEADDIM, SSD_GROUP_WIDTH
    conv_in = lambda j: zxbc_ref[:, SSD_INNER + j * LANES:SSD_INNER + (j + 1) * LANES]

    @pl.when(pl.program_id(1) == 0)
    def _():
        stage[...] = jnp.zeros(stage.shape, stage.dtype)
        stage[0:HIST_ROWS, :] = _history_tile(conv0_ref[...])
        for g in range(SSD_GROUPS):
            state[g] = jnp.concatenate([s0_ref[g * gh + hh] for hh in range(gh)], axis=-1)

    stage[pl.ds(HIST_ROWS, c), :] = zxbc_ref[:, SSD_INNER:SSD_INNER + SSD_CONV_DIM]
    sh = _dot(shift_ref[...], stage[...])
    tail = stage[pl.ds(HIST_ROWS + c - BF16_ROWS, BF16_ROWS), :].astype(F32)[SUBLANES:]
    stage[0:HIST_ROWS, :] = _history_tile(tail)
    ctail_ref[...] = tail

    cw_half = 0.5 * cw_ref[...]
    cb_half = 0.5 * cb_ref[...]

    def conv_tile(x_bf16, j):
        sl = slice(j * LANES, (j + 1) * LANES)
        acc = sh[2 * c:3 * c, sl] * cw_half[0:1, sl]
        acc = acc + sh[c:2 * c, sl] * cw_half[1:2, sl]
        acc = acc + sh[0:c, sl] * cw_half[2:3, sl]
        acc = acc + x_bf16.astype(F32) * cw_half[3:4, sl]
        h = cb_half[:, sl] + acc
        return h + h * jnp.tanh(h)

    x_tiles = SSD_INNER // LANES
    bc_tiles = SSD_BC // LANES

    dt = _softplus(dt_ref[...] + dtb_ref[...])
    a_neg = -jnp.exp(alog_ref[...])
    g2 = _cumsum_rows(dt * a_neg) * LOG2E
    g2_t = g2.T
    lane_lo = lax.broadcasted_iota(jnp.int32, (c, LANES), 1) < hd
    ii = lax.broadcasted_iota(jnp.int32, (c, c), 0)
    jj = lax.broadcasted_iota(jnp.int32, (c, c), 1)
    tri = ii >= jj

    bgs, cbs, y_inter = [], [], []
    for g in range(SSD_GROUPS):
        bg = conv_tile(conv_in(x_tiles + g), x_tiles + g).astype(BF16)
        cg = conv_tile(conv_in(x_tiles + bc_tiles + g), x_tiles + bc_tiles + g).astype(BF16)
        bgs.append(bg)
        cbs.append(jnp.where(tri, _dot_nt(cg, bg), 0.0))
        y_inter.append(_dot(cg, state[g].astype(BF16)))

    ssq = None
    a_tiles = []
    for p in range(SSD_HEADS // 2):
        g, pp = divmod(p, gh // 2)
        sl = slice(p * LANES, (p + 1) * LANES)
        xs = conv_tile(conv_in(p), p)
        gcol = [jnp.broadcast_to(g2[:, h:h + 1], (c, LANES)) for h in (2 * p, 2 * p + 1)]
        dcol = [jnp.broadcast_to(dt[:, h:h + 1], (c, LANES)) for h in (2 * p, 2 * p + 1)]
        g_x = jnp.where(lane_lo, gcol[0], gcol[1])
        xdt = xs * jnp.where(lane_lo, dcol[0], dcol[1])
        g_last = g_x[c - 1:c, :]
        xw_scr[:, sl] = (xdt * jnp.exp2(g_last - g_x)).astype(BF16)
        a_tiles.append(jnp.exp2(g_last))
        ms = []
        for k, h in enumerate((2 * p, 2 * p + 1)):
            decay = jnp.exp2(jnp.minimum(gcol[k][:, 0:c] - g2_t[h:h + 1, :], 0.0))
            ms.append((cbs[g] * decay).astype(BF16))
        rhs = jnp.concatenate([jnp.where(lane_lo, xdt, 0.0), jnp.where(lane_lo, 0.0, xdt)], axis=0).astype(BF16)
        y = _dot(jnp.concatenate(ms, axis=1), rhs) + jnp.exp2(g_x) * y_inter[g][:, pp * LANES:(pp + 1) * LANES]
        so = y + dexp_ref[:, sl] * xs
        u = so * _silu(zxbc_ref[:, sl].astype(F32))
        ssq = u * u if ssq is None else ssq + u * u
        u_scr[:, sl] = u

    for g in range(SSD_GROUPS):
        gsl = slice(g * gw, (g + 1) * gw)
        a_row = jnp.concatenate(a_tiles[g * (gh // 2):(g + 1) * (gh // 2)], axis=-1)
        state[g] = a_row * state[g] + _dot_tn(bgs[g], xw_scr[:, gsl])

    rs = lax.rsqrt(jnp.sum(ssq, axis=-1, keepdims=True) * (1.0 / SSD_INNER) + EPS)
    out_ref[...] = (u_scr[...] * rs * nrm_ref[...]).astype(out_ref.dtype)

    @pl.when(pl.program_id(1) == n_steps - 1)
    def _():
        for g in range(SSD_GROUPS):
            sg = state[g]
            for hh in range(gh):
                snew_ref[g * gh + hh] = sg[:, hh * hd:(hh + 1) * hd]


def _ssd_parts(proj, dt_raw, conv0, s0, p, states, layer, *, batch, seq, c):
    assert seq % c == 0 and states.shape[1:] == (batch, SSD_HEADS, SSD_STATE, SSD_HEADDIM)
    nt = seq // c
    row = lambda b, t: b * nt + t
    const2 = lambda b, t: (0, 0)
    width = SSD_INNER + SSD_CONV_DIM
    assert OFF_SZ % width == 0 and OFF_SX == OFF_SZ + SSD_INNER
    shift = _shift_matrix(c)
    return MixerParts(
        body=functools.partial(_ssd_body, n_steps=nt),
        in_specs=[
            pl.BlockSpec((c, width), lambda b, t: (row(b, t), OFF_SZ // width)),
            pl.BlockSpec((c, LANES), lambda b, t: (row(b, t), 0)),
            pl.BlockSpec((None, SUBLANES, SSD_CONV_DIM), lambda b, t: (b, 0, 0)),
            pl.BlockSpec((None, SSD_HEADS, SSD_STATE, SSD_HEADDIM), lambda b, t: (b, 0, 0, 0)),
            pl.BlockSpec(shift.shape, const2),
            pl.BlockSpec((CONV_W, SSD_CONV_DIM), const2),
            pl.BlockSpec((1, SSD_CONV_DIM), const2),
            pl.BlockSpec((1, LANES), const2),
            pl.BlockSpec((1, LANES), const2),
            pl.BlockSpec((1, SSD_INNER), const2),
            pl.BlockSpec((1, SSD_INNER), const2),
        ],
        out_specs=[
            pl.BlockSpec((c, SSD_INNER), lambda b, t: (row(b, t), 0)),
            pl.BlockSpec((None, None, SSD_HEADS, SSD_STATE, SSD_HEADDIM), lambda b, t: (layer, b, 0, 0, 0)),
            pl.BlockSpec((None, SUBLANES, SSD_CONV_DIM), lambda b, t: (b, 0, 0)),
        ],
        out_shape=[
            jax.ShapeDtypeStruct((batch * seq, SSD_INNER), BF16),
            jax.ShapeDtypeStruct(states.shape, F32),
            jax.ShapeDtypeStruct((batch, SUBLANES, SSD_CONV_DIM), F32),
        ],
        carried={1: states},
        scratch=[
            pltpu.VMEM((_stage_rows(c), SSD_CONV_DIM), BF16),
            pltpu.VMEM((SSD_GROUPS, SSD_STATE, SSD_GROUP_WIDTH), F32),
            pltpu.VMEM((c, SSD_INNER), BF16),
            pltpu.VMEM((c, SSD_INNER), F32),
        ],
        args=(proj, dt_raw, conv0, s0, shift, p["ssd_conv_w"], p["ssd_conv_b"],
              p["ssd_dt_bias"], p["ssd_a_log"], p["ssd_d_exp"], p["ssd_norm"]))


def _xattn_body(x_ref, nx_ref, wq_ref, mk_ref, mv_ref, wo_ref, nf_ref, o_ref, *rest, final):
    if final:
        hn_ref, part_ref = rest
    else:
        hn_next_ref, hn_ref, part_ref = rest
    _rmsnorm_rows_to(hn_ref, x_ref, nx_ref, part_ref)
    q = _dot(hn_ref[...], wq_ref[...])
    scale = XATT_HD ** -0.5
    n_seqs = mk_ref.shape[0]
    rows = x_ref.shape[0] // n_seqs
    o_rows = []
    for bi in range(n_seqs):
        qb = q[bi * rows:(bi + 1) * rows]
        os = []
        for h in range(XATT_HEADS):
            sl = slice(h * XATT_HD, (h + 1) * XATT_HD)
            s = _dot_nt(qb[:, sl].astype(BF16), mk_ref[bi, :, sl].astype(BF16)) * scale
            mx = jnp.max(s, axis=-1, keepdims=True)
            pr = jnp.exp(s - mx)
            den = jnp.sum(pr, axis=-1, keepdims=True)
            os.append(_dot(pr.astype(BF16), mv_ref[bi, :, sl].astype(BF16)) / den)
        o_rows.append(jnp.concatenate(os, axis=-1))
    o = jnp.concatenate(o_rows, axis=0).astype(BF16)
    o_ref[...] = x_ref[...] + _dot(o, wo_ref[...])
    if final:
        _rmsnorm_rows_to(o_ref, o_ref, nf_ref, part_ref)
    else:
        _rmsnorm_rows_to(hn_next_ref, o_ref, nf_ref, part_ref)


def _cross_attn(x, mem_k, mem_v, nx, wq, wo, nf, *, batch, seq, tm, final):
    d = x.shape[1]
    n_seqs = max(1, min(tm // seq, batch))
    assert batch % n_seqs == 0
    tm = min(tm, seq) * n_seqs
    nt = seq * n_seqs // tm
    const2 = lambda b, t: (0, 0)
    x_spec = pl.BlockSpec((tm, d), lambda b, t: (b * nt + t, 0))
    out_specs = x_spec if final else [x_spec, x_spec]
    out_shape = jax.ShapeDtypeStruct(x.shape, F32)
    if not final:
        out_shape = [out_shape, jax.ShapeDtypeStruct(x.shape, BF16)]
    return pl.pallas_call(
        functools.partial(_xattn_body, final=final),
        grid=(batch // n_seqs, nt),
        in_specs=[
            x_spec,
            pl.BlockSpec((1, d), const2),
            pl.BlockSpec((d, XATT_DIM), const2, pipeline_mode=pl.Buffered(1)),
            pl.BlockSpec((n_seqs, N_MEM, XATT_DIM), lambda b, t: (b, 0, 0)),
            pl.BlockSpec((n_seqs, N_MEM, XATT_DIM), lambda b, t: (b, 0, 0)),
            pl.BlockSpec((XATT_DIM, d), const2, pipeline_mode=pl.Buffered(1)),
            pl.BlockSpec((1, d), const2),
        ],
        out_specs=out_specs,
        out_shape=out_shape,
        scratch_shapes=[pltpu.VMEM((tm, d), BF16), pltpu.VMEM((tm, LANES), F32)],
        compiler_params=_params(("arbitrary", "arbitrary")),
        name="cross_attn",
    )(x, nx.reshape(1, d), wq, mem_k, mem_v, wo, nf.reshape(1, d))


def _pad_state_rows(s):
    return jnp.pad(s.astype(F32), ((0, 0), (SUBLANES - (CONV_W - 1), 0), (0, 0)))


def _rope_tables(pos):
    half = RET_DK // 2
    inv = ROPE_BASE ** (-jnp.arange(half, dtype=F32) / half)
    ang = pos[:, None] * inv[None, :]
    cos, sin = jnp.cos(ang), jnp.sin(ang)
    cq = jnp.concatenate([cos, cos], axis=-1)
    sq = jnp.concatenate([-sin, sin], axis=-1)
    ks = RET_DK ** -0.5
    return cq, sq, cq * ks, sq * ks


def _layer_params(l, norm_mix, w_in, lru_conv_w, lru_conv_b, lru_wa, lru_ba, lru_wx, lru_bx, lru_lambda, lru_norm,
                  ret_norm, ssd_conv_w, ssd_conv_b, ssd_dt_bias, ssd_a_log, ssd_d, ssd_norm, w_out, norm_xattn,
                  norm_mem, w_q_mem, w_k_mem, w_v_mem, w_o_mem):
    w_in, w_dt_t = w_in
    w_dt = w_dt_t[l]
    w_q, w_kv, w_o = _cast_xattn_weights(w_q_mem, w_k_mem, w_v_mem, w_o_mem, l)
    row = lambda v: v.reshape(1, -1).astype(F32)
    pad_lanes = lambda v: jnp.pad(v.astype(F32), (0, LANES - v.shape[0])).reshape(1, LANES)
    return dict(
        norm_mix=norm_mix[l], w_in=w_in, layer=l, w_dt=w_dt,
        lru_conv_w=lru_conv_w[l], lru_conv_b=row(lru_conv_b[l]), lru_wa=lru_wa[l].astype(BF16), lru_ba=row(lru_ba[l]),
        lru_wx=lru_wx[l].astype(BF16), lru_bx=row(lru_bx[l]), lru_lambda=row(lru_lambda[l]), lru_norm=row(lru_norm[l]),
        ret_norm=row(ret_norm[l]),
        ssd_conv_w=ssd_conv_w[l], ssd_conv_b=row(ssd_conv_b[l]), ssd_dt_bias=pad_lanes(ssd_dt_bias[l]),
        ssd_a_log=pad_lanes(ssd_a_log[l]), ssd_d_exp=row(jnp.repeat(ssd_d[l], SSD_HEADDIM)), ssd_norm=row(ssd_norm[l]),
        w_out=_cast_layer_bf16(w_out, l), norm_xattn=norm_xattn[l], norm_mem=norm_mem[l],
        w_q=w_q, w_kv=w_kv, w_o=w_o)


def _mixer_sublayer(x, hn, rope, lru_h0, lru_conv0, ret_s0, ssd_s0, ssd_conv0, ret_states, ssd_states, p, *,
                    batch, seq):
    layer = p["layer"]
    proj, dt_raw = _in_proj(hn, p["w_in"], layer, p["w_dt"], n=OFF_DT, tm=TM_IN_PROJ, tn=TN_IN_PROJ)
    c = min(CHUNK, seq)
    (lru_out, lru_h1, lru_ct), (ssd_out, ssd_states, ssd_ct) = _run_mixers(
        [_lru_parts(proj, _pad_state_rows(lru_conv0), lru_h0.reshape(batch, 1, LRU_WIDTH), p,
                    batch=batch, seq=seq, tb=c),
         _ssd_parts(proj, dt_raw, _pad_state_rows(ssd_conv0), ssd_s0, p, ssd_states, layer,
                    batch=batch, seq=seq, c=c)],
        batch=batch, nt=seq // c)
    c_ret = min(CHUNK_RET, seq)
    ((ret_out, ret_states),) = _run_mixers(
        [_ret_parts(proj, ret_s0, rope, p, ret_states, layer, batch=batch, seq=seq, c=c_ret)],
        batch=batch, nt=seq // c_ret)
    x1 = _out_proj(x, lru_out, ret_out, ssd_out, p["w_out"], tm=TM_OUT, tn=TN_OUT)
    tail = SUBLANES - (CONV_W - 1)
    small_states = (lru_h1.reshape(batch, LRU_WIDTH), lru_ct[:, tail:, :], ssd_ct[:, tail:, :])
    return x1, small_states, ret_states, ssd_states


def kernel(x_prompt, x_sample, mem_prompt, state_lru_h, state_lru_conv, state_ret, state_ssd, state_ssd_conv, cache_mem_k, cache_mem_v, norm_mix, w_in, lru_conv_w, lru_conv_b, lru_wa, lru_ba, lru_wx, lru_bx, lru_lambda, lru_norm, ret_norm, ssd_conv_w, ssd_conv_b, ssd_dt_bias, ssd_a_log, ssd_d, ssd_norm, w_out, norm_xattn, norm_mem, w_q_mem, w_k_mem, w_v_mem, w_o_mem, norm_final):
    bp, tp, d = x_prompt.shape
    bs, ts, _ = x_sample.shape
    depth = w_in.shape[0]
    assert tp % BF16_ROWS == 0 and ts % BF16_ROWS == 0
    rope_p = _rope_tables(jnp.arange(tp, dtype=F32))
    rope_s = _rope_tables(PAST_LEN + jnp.arange(ts, dtype=F32))
    z_lru_h = jnp.zeros((bp, LRU_WIDTH), F32)
    z_lru_conv = jnp.zeros((bp, CONV_W - 1, LRU_WIDTH), F32)
    z_ret = jnp.zeros((bp, RET_HEADS, RET_DK, RET_DV), F32)
    z_ssd = jnp.zeros((bp, SSD_HEADS, SSD_STATE, SSD_HEADDIM), F32)
    z_ssd_conv = jnp.zeros((bp, CONV_W - 1, SSD_CONV_DIM), F32)

    xp = x_prompt.reshape(bp * tp, d)
    xs = x_sample.reshape(bs * ts, d)
    mem = mem_prompt.reshape(bp * N_MEM, d)
    w_in_t = jnp.swapaxes(w_in, 1, 2)
    w_dt_t = jnp.pad(w_in_t[:, OFF_DT:, :], ((0, 0), (0, LANES - SSD_HEADS), (0, 0))).astype(BF16)
    w_in = (_cast_stack_bf16(w_in_t, n_rows=OFF_DT, rows=CAST_ROWS_W_IN), w_dt_t)
    hn_p = _rmsnorm_bf16(xp, norm_mix[0], tm=TM_XATT)
    hn_s = _rmsnorm_bf16(xs, norm_mix[0], tm=TM_XATT)
    st_p = [[] for _ in range(3)]
    st_s = [[] for _ in range(3)]
    ret_p = jnp.zeros((depth,) + z_ret.shape, F32)
    ssd_p = jnp.zeros((depth,) + z_ssd.shape, F32)
    ret_s = jnp.zeros(state_ret.shape, F32)
    ssd_s = jnp.zeros(state_ssd.shape, F32)
    mk_p, mv_p = [], []
    for l in range(depth):
        p = _layer_params(l, norm_mix, w_in, lru_conv_w, lru_conv_b, lru_wa, lru_ba, lru_wx, lru_bx, lru_lambda,
                          lru_norm, ret_norm, ssd_conv_w, ssd_conv_b, ssd_dt_bias, ssd_a_log, ssd_d, ssd_norm, w_out,
                          norm_xattn, norm_mem, w_q_mem, w_k_mem, w_v_mem, w_o_mem)
        final = l == depth - 1
        next_gain = norm_final if final else norm_mix[l + 1]
        xp, st, ret_p, ssd_p = _mixer_sublayer(xp, hn_p, rope_p, z_lru_h, z_lru_conv, z_ret, z_ssd, z_ssd_conv,
                                               ret_p, ssd_p, p, batch=bp, seq=tp)
        mkv = _norm_matmul(mem, p["norm_mem"], p["w_kv"], out_dtype=F32, tm=TM_PROJ, tn=TN_PROJ)
        mk = mkv[:, :XATT_DIM].reshape(bp, N_MEM, XATT_DIM)
        mv = mkv[:, XATT_DIM:].reshape(bp, N_MEM, XATT_DIM)
        res = _cross_attn(xp, mk, mv, p["norm_xattn"], p["w_q"], p["w_o"], next_gain, batch=bp, seq=tp, tm=TM_XATT,
                          final=final)
        xp, hn_p = (res, None) if final else res
        for acc, v in zip(st_p, st):
            acc.append(v)
        mk_p.append(mk.reshape(bp, N_MEM, XATT_HEADS, XATT_HD))
        mv_p.append(mv.reshape(bp, N_MEM, XATT_HEADS, XATT_HD))
        xs, st, ret_s, ssd_s = _mixer_sublayer(xs, hn_s, rope_s, state_lru_h[l], state_lru_conv[l], state_ret[l],
                                               state_ssd[l], state_ssd_conv[l], ret_s, ssd_s, p, batch=bs, seq=ts)
        res = _cross_attn(xs, cache_mem_k[l].reshape(bs, N_MEM, XATT_DIM), cache_mem_v[l].reshape(bs, N_MEM, XATT_DIM),
                          p["norm_xattn"], p["w_q"], p["w_o"], next_gain, batch=bs, seq=ts, tm=TM_XATT, final=final)
        xs, hn_s = (res, None) if final else res
        for acc, v in zip(st_s, st):
            acc.append(v)

    y_prompt = xp.reshape(bp, tp, d)
    y_sample = xs.reshape(bs, ts, d)
    return (y_prompt, y_sample,
            jnp.stack(st_p[0]), jnp.stack(st_p[1]), ret_p, ssd_p, jnp.stack(st_p[2]),
            jnp.stack(mk_p), jnp.stack(mv_p),
            jnp.stack(st_s[0]), jnp.stack(st_s[1]), ret_s, ssd_s, jnp.stack(st_s[2]))
```
